```python
import jax, jax.numpy as jnp
from jax import lax
import numpy as np

D_MODEL = 2048
BATCH = 8
SEQ = 2048
DEPTH = 2
DEC_BATCH = 32
DEC_SEQ = 1
PAST_LEN = 8192
PAGE_SIZE = 128

MIX_DIM = D_MODEL
HEAD_DIM = 128
ATT_HEADS = (MIX_DIM // 2) // HEAD_DIM
ATT_DIM = ATT_HEADS * HEAD_DIM
CONV_DIM = MIX_DIM - ATT_DIM
CONV_GROUPS = CONV_DIM // HEAD_DIM
CONV_W = 3
IN_COLS = 3 * ATT_DIM + ATT_HEADS + 3 * CONV_DIM
D_FF = 5632
N_EXPERTS = 8
TOP_K = 2
EXPERT_FF = D_FF // 2
N_DENSE = (DEPTH + 1) // 2
N_MOE = DEPTH // 2
Q_BLOCK = 128
ATT_SCALE = HEAD_DIM ** -0.5
RMS_EPS = 1e-6

kernel_name = "hybrid_fox_shortconv_decode_step"


def rmsnorm(x, g):
    xf = x.astype(jnp.float32)
    xf = xf * lax.rsqrt(jnp.mean(xf * xf, axis=-1, keepdims=True) + RMS_EPS)
    return (xf * g.astype(jnp.float32)).astype(x.dtype)


def groupwise_rmsnorm(y, g, n_groups):
    lead = y.shape[:-1]
    yg = y.reshape(lead + (n_groups, y.shape[-1] // n_groups))
    return rmsnorm(yg, g.reshape(n_groups, -1)).reshape(y.shape)


def split_in_proj(z):
    sizes = (ATT_DIM, ATT_DIM, ATT_DIM, ATT_HEADS, CONV_DIM, CONV_DIM, CONV_DIM)
    out, start = [], 0
    for s in sizes:
        out.append(z[..., start:start + s])
        start += s
    return out


def short_conv(u, prev, w):
    T = u.shape[1]
    up = jnp.concatenate([prev.astype(u.dtype), u], axis=1)
    y = up[:, 0:T] * w[0]
    for j in range(1, CONV_W):
        y = y + up[:, j:j + T] * w[j]
    return y, up[:, T:]


def forget_logits(q, k, cq, ck):
    s = jnp.einsum("bqhd,bkhd->bhqk", q, k, preferred_element_type=jnp.float32) * ATT_SCALE
    return s + jnp.swapaxes(cq, 1, 2)[:, :, :, None] - jnp.swapaxes(ck, 1, 2)[:, :, None, :]


def prompt_attention(q, k, v, logf):
    n, S, H, Dh = q.shape
    c = jnp.cumsum(logf.astype(jnp.float32), axis=1)
    nblk = S // Q_BLOCK
    qb = q.reshape(n, nblk, Q_BLOCK, H, Dh).transpose(1, 0, 2, 3, 4)
    cb = c.reshape(n, nblk, Q_BLOCK, H).transpose(1, 0, 2, 3)
    kpos = jnp.arange(S)

    def one_block(args):
        q_i, c_i, i = args
        s = forget_logits(q_i, k, c_i, c)
        qpos = i * Q_BLOCK + jnp.arange(Q_BLOCK)
        s = jnp.where(kpos[None, :] <= qpos[:, None], s, -jnp.inf)
        p = jax.nn.softmax(s, axis=-1)
        return jnp.einsum("bhqk,bkhd->bqhd", p.astype(v.dtype), v)

    o = lax.map(one_block, (qb, cb, jnp.arange(nblk)))
    return o.transpose(1, 0, 2, 3, 4).reshape(n, S, H * Dh)


def sample_attention(q, k, v, logf, k_past, v_past, logf_past):
    n, T, H, Dh = q.shape
    P = k_past.shape[1]
    c_past = jnp.cumsum(logf_past.astype(jnp.float32), axis=1)
    c_new = c_past[:, -1:, :] + jnp.cumsum(logf.astype(jnp.float32), axis=1)
    s_past = forget_logits(q, k_past, c_new, c_past)
    s_new = forget_logits(q, k, c_new, c_new)
    causal = jnp.arange(T)[None, :] <= jnp.arange(T)[:, None]
    s_new = jnp.where(causal, s_new, -jnp.inf)
    p = jax.nn.softmax(jnp.concatenate([s_past, s_new], axis=-1), axis=-1).astype(v.dtype)
    o = (jnp.einsum("bhqk,bkhd->bqhd", p[..., :P], v_past)
         + jnp.einsum("bhqk,bkhd->bqhd", p[..., P:], v))
    return o.reshape(n, T, H * Dh)


def mixer(xn, w_in, b_f, conv_w, g_att, g_conv, w_out, conv_prev, attend):
    n, T, _ = xn.shape
    q, k, v, f_logit, gate_b, gate_c, h = split_in_proj(xn @ w_in)
    q = q.reshape(n, T, ATT_HEADS, HEAD_DIM)
    k = k.reshape(n, T, ATT_HEADS, HEAD_DIM)
    v = v.reshape(n, T, ATT_HEADS, HEAD_DIM)
    logf = jax.nn.log_sigmoid(f_logit.astype(jnp.float32) + b_f.astype(jnp.float32))
    att = attend(q, k, v, logf)
    conv_out, conv_state = short_conv(gate_c * h, conv_prev, conv_w)
    yc = gate_b * conv_out
    merged = jnp.concatenate([groupwise_rmsnorm(att, g_att, ATT_HEADS),
                              groupwise_rmsnorm(yc, g_conv, CONV_GROUPS)], axis=-1)
    return merged @ w_out, k, v, logf.astype(xn.dtype), conv_state


def swiglu(x, wg, wu, wd):
    return (jax.nn.silu(x @ wg) * (x @ wu)) @ wd


def moe_swiglu(x, router, wg, wu, wd):
    n, T, D = x.shape
    xt = x.reshape(n * T, D)
    logits = (xt @ router).astype(jnp.float32)
    top_v, top_i = lax.top_k(logits, TOP_K)
    gates = jax.nn.softmax(top_v, axis=-1)
    dense_gate = jnp.sum(jax.nn.one_hot(top_i, N_EXPERTS, dtype=jnp.float32) * gates[..., None], axis=1)
    y = jnp.zeros_like(xt)
    for e in range(N_EXPERTS):
        y = y + dense_gate[:, e:e + 1].astype(xt.dtype) * swiglu(xt, wg[e], wu[e], wd[e])
    return y.reshape(n, T, D)


def setup_inputs(seed: int = 0) -> dict:
    key = jax.random.key(seed)
    ks = jax.random.split(key, 32)
    n_pages = PAST_LEN // PAGE_SIZE
    n_used = DEC_BATCH * n_pages
    n_pool = n_used + max(1, n_used // 4)

    def nrm(k, shape, scale):
        return jax.random.normal(k, shape, jnp.float32) * scale

    def gain(k, shape):
        return 1.0 + nrm(k, shape, 0.02)

    x_prompt = nrm(ks[0], (BATCH, SEQ, D_MODEL), 1.0)
    x_sample = nrm(ks[1], (DEC_BATCH, DEC_SEQ, D_MODEL), 1.0)
    cache_k = nrm(ks[2], (DEPTH, n_pool, PAGE_SIZE, ATT_HEADS, HEAD_DIM), 1.0)
    cache_v = nrm(ks[3], (DEPTH, n_pool, PAGE_SIZE, ATT_HEADS, HEAD_DIM), 1.0)
    cache_logf = jax.nn.log_sigmoid(
        jax.random.uniform(ks[4], (DEPTH, n_pool, PAGE_SIZE, ATT_HEADS), jnp.float32, 3.0, 6.0)
        + nrm(ks[5], (DEPTH, n_pool, PAGE_SIZE, ATT_HEADS), 1.0))
    state_conv = nrm(ks[6], (DEPTH, DEC_BATCH, CONV_W - 1, CONV_DIM), 1.0)
    page_table = jax.random.permutation(ks[7], n_pool)[:n_used].reshape(DEC_BATCH, n_pages).astype(jnp.int32)

    return {
        "x_prompt": x_prompt,
        "x_sample": x_sample,
        "cache_k": cache_k,
        "cache_v": cache_v,
        "cache_logf": cache_logf,
        "state_conv": state_conv,
        "page_table": page_table,
        "norm_mix_g": gain(ks[8], (DEPTH, D_MODEL)),
        "w_in": nrm(ks[9], (DEPTH, D_MODEL, IN_COLS), D_MODEL ** -0.5),
        "b_forget": jax.random.uniform(ks[10], (DEPTH, ATT_HEADS), jnp.float32, 3.0, 6.0),
        "conv_w": nrm(ks[11], (DEPTH, CONV_W, CONV_DIM), CONV_W ** -0.5),
        "g_att_out": gain(ks[12], (DEPTH, ATT_DIM)),
        "g_conv_out": gain(ks[13], (DEPTH, CONV_DIM)),
        "w_out": nrm(ks[14], (DEPTH, MIX_DIM, D_MODEL), MIX_DIM ** -0.5),
        "norm_ffn_g": gain(ks[15], (DEPTH, D_MODEL)),
        "dense_w_gate": nrm(ks[16], (N_DENSE, D_MODEL, D_FF), D_MODEL ** -0.5),
        "dense_w_up": nrm(ks[17], (N_DENSE, D_MODEL, D_FF), D_MODEL ** -0.5),
        "dense_w_down": nrm(ks[18], (N_DENSE, D_FF, D_MODEL), D_FF ** -0.5),
        "moe_router": nrm(ks[19], (N_MOE, D_MODEL, N_EXPERTS), D_MODEL ** -0.5),
        "moe_w_gate": nrm(ks[20], (N_MOE, N_EXPERTS, D_MODEL, EXPERT_FF), D_MODEL ** -0.5),
        "moe_w_up": nrm(ks[21], (N_MOE, N_EXPERTS, D_MODEL, EXPERT_FF), D_MODEL ** -0.5),
        "moe_w_down": nrm(ks[22], (N_MOE, N_EXPERTS, EXPERT_FF, D_MODEL), EXPERT_FF ** -0.5),
        "final_norm_g": gain(ks[23], (D_MODEL,)),
    }


def reference(x_prompt, x_sample, cache_k, cache_v, cache_logf, state_conv, page_table,
              norm_mix_g, w_in, b_forget, conv_w, g_att_out, g_conv_out, w_out, norm_ffn_g,
              dense_w_gate, dense_w_up, dense_w_down, moe_router, moe_w_gate, moe_w_up,
              moe_w_down, final_norm_g):
    n_dec = x_sample.shape[0]
    past = page_table.shape[1] * cache_k.shape[2]
    xp, xs = x_prompt, x_sample
    kp_l, vp_l, fp_l, cp_l, ks_l, vs_l, fs_l, cs_l = [], [], [], [], [], [], [], []

    for l in range(DEPTH):
        k_past = cache_k[l][page_table].reshape(n_dec, past, ATT_HEADS, HEAD_DIM)
        v_past = cache_v[l][page_table].reshape(n_dec, past, ATT_HEADS, HEAD_DIM)
        f_past = cache_logf[l][page_table].reshape(n_dec, past, ATT_HEADS)

        def attend_sample(q, k, v, lf, k_past=k_past, v_past=v_past, f_past=f_past):
            return sample_attention(q, k, v, lf, k_past, v_past, f_past)

        zeros_prev = jnp.zeros((xp.shape[0], CONV_W - 1, CONV_DIM), xp.dtype)
        mp, k_p, v_p, f_p, c_p = mixer(rmsnorm(xp, norm_mix_g[l]), w_in[l], b_forget[l], conv_w[l],
                                       g_att_out[l], g_conv_out[l], w_out[l], zeros_prev,
                                       prompt_attention)
        ms, k_s, v_s, f_s, c_s = mixer(rmsnorm(xs, norm_mix_g[l]), w_in[l], b_forget[l], conv_w[l],
                                       g_att_out[l], g_conv_out[l], w_out[l], state_conv[l],
                                       attend_sample)
        xp = xp + mp
        xs = xs + ms
        kp_l.append(k_p); vp_l.append(v_p); fp_l.append(f_p); cp_l.append(c_p)
        ks_l.append(k_s); vs_l.append(v_s); fs_l.append(f_s); cs_l.append(c_s)

        hp = rmsnorm(xp, norm_ffn_g[l])
        hs = rmsnorm(xs, norm_ffn_g[l])
        i = l // 2
        if l % 2 == 0:
            xp = xp + swiglu(hp, dense_w_gate[i], dense_w_up[i], dense_w_down[i])
            xs = xs + swiglu(hs, dense_w_gate[i], dense_w_up[i], dense_w_down[i])
        else:
            xp = xp + moe_swiglu(hp, moe_router[i], moe_w_gate[i], moe_w_up[i], moe_w_down[i])
            xs = xs + moe_swiglu(hs, moe_router[i], moe_w_gate[i], moe_w_up[i], moe_w_down[i])

    y_prompt = rmsnorm(xp, final_norm_g)
    y_sample = rmsnorm(xs, final_norm_g)
    return (y_prompt, y_sample,
            jnp.stack(kp_l), jnp.stack(vp_l), jnp.stack(fp_l), jnp.stack(cp_l),
            jnp.stack(ks_l), jnp.stack(vs_l), jnp.stack(fs_l), jnp.stack(cs_l))
```

```python
import functools

import jax
import jax.numpy as jnp
from jax import lax
from jax.experimental import pallas as pl
from jax.experimental.pallas import tpu as pltpu

D_MODEL = 2048
HEAD_DIM = 128
ATT_HEADS = 8
ATT_DIM = ATT_HEADS * HEAD_DIM
CONV_DIM = D_MODEL - ATT_DIM
CONV_W = 3
N_EXPERTS = 8
ATT_SCALE = HEAD_DIM ** -0.5
RMS_EPS = 1e-6
LANES = 128
NEG_BIG = -1e30
PAGES_PER_STEP = 8
VMEM_LIMIT = 56 * 1024 * 1024

F32 = jnp.float32
BF16 = jnp.bfloat16


def _cp(n_axes):
    return pltpu.CompilerParams(dimension_semantics=("arbitrary",) * n_axes,
                                vmem_limit_bytes=VMEM_LIMIT)


def _rms(x, g):
    return x * lax.rsqrt(jnp.mean(x * x, axis=-1, keepdims=True) + RMS_EPS) * g


def _dot(a, b):
    return jnp.dot(a, b, preferred_element_type=F32)


def _rmsnorm_kernel(x_ref, g_ref, o_ref):
    o_ref[...] = _rms(x_ref[...], g_ref[...]).astype(o_ref.dtype)


def _rmsnorm(x, g, tm, out_dtype=BF16):
    m, d = x.shape
    return pl.pallas_call(
        _rmsnorm_kernel, grid=(m // tm,),
        in_specs=[pl.BlockSpec((tm, d), lambda i: (i, 0)),
                  pl.BlockSpec((1, d), lambda i: (0, 0))],
        out_specs=pl.BlockSpec((tm, d), lambda i: (i, 0)),
        out_shape=jax.ShapeDtypeStruct((m, d), out_dtype),
        compiler_params=_cp(1), name="rmsnorm")(x, g.reshape(1, d))


def _mm_kernel(x_ref, w_ref, o_ref):
    o_ref[...] = _dot(x_ref[...], w_ref[...]).astype(o_ref.dtype)


def _mm(x, w, out_dtype, tm, tn, name):
    m, k = x.shape
    n = w.shape[1]
    return pl.pallas_call(
        _mm_kernel, grid=(m // tm, n // tn),
        in_specs=[pl.BlockSpec((tm, k), lambda i, j: (i, 0)),
                  pl.BlockSpec((k, tn), lambda i, j: (0, j))],
        out_specs=pl.BlockSpec((tm, tn), lambda i, j: (i, j)),
        out_shape=jax.ShapeDtypeStruct((m, n), out_dtype),
        compiler_params=_cp(2), name=name)(x, w)


def _swiglu_kernel(x_ref, wg_ref, wu_ref, o_ref):
    x = x_ref[...]
    a = _dot(x, wg_ref[...])
    b = _dot(x, wu_ref[...])
    o_ref[...] = (a * (1.0 / (1.0 + jnp.exp(-a))) * b).astype(o_ref.dtype)


def _swiglu(x, wg, wu, tm, tn):
    m, k = x.shape
    n = wg.shape[1]
    return pl.pallas_call(
        _swiglu_kernel, grid=(m // tm, n // tn),
        in_specs=[pl.BlockSpec((tm, k), lambda i, j: (i, 0)),
                  pl.BlockSpec((k, tn), lambda i, j: (0, j)),
                  pl.BlockSpec((k, tn), lambda i, j: (0, j))],
        out_specs=pl.BlockSpec((tm, tn), lambda i, j: (i, j)),
        out_shape=jax.ShapeDtypeStruct((m, n), BF16),
        compiler_params=_cp(2), name="swiglu_up")(x, wg, wu)


def _down_kernel(*refs, use_scale, norm):
    h_ref, w_ref, res_ref = refs[:3]
    rest = list(refs[3:])
    sc_ref = rest.pop(0) if use_scale else None
    g_ref = rest.pop(0) if norm else None
    xo_ref = rest.pop(0)
    xn_ref = rest.pop(0) if norm else None
    acc_scr = rest.pop(0)
    k = pl.program_id(1)

    @pl.when(k == 0)
    def _():
        acc_scr[...] = jnp.zeros_like(acc_scr)

    acc_scr[...] += _dot(h_ref[...], w_ref[...])

    @pl.when(k == pl.num_programs(1) - 1)
    def _():
        y = acc_scr[...]
        if use_scale:
            y = y * sc_ref[...]
        xo = res_ref[...] + y
        xo_ref[...] = xo
        if norm:
            xn_ref[...] = _rms(xo, g_ref[...]).astype(xn_ref.dtype)


def _down(h, w, res, tm, tk, scale=None, norm_g=None, norm_dtype=BF16):
    m, kdim = h.shape
    d = w.shape[1]
    use_scale, norm = scale is not None, norm_g is not None
    in_specs = [pl.BlockSpec((tm, tk), lambda i, k: (i, k)),
                pl.BlockSpec((tk, d), lambda i, k: (k, 0)),
                pl.BlockSpec((tm, d), lambda i, k: (i, 0))]
    args = [h, w, res]
    if use_scale:
        in_specs.append(pl.BlockSpec((tm, 1), lambda i, k: (i, 0)))
        args.append(scale)
    if norm:
        in_specs.append(pl.BlockSpec((1, d), lambda i, k: (0, 0)))
        args.append(norm_g.reshape(1, d))
    out_specs = [pl.BlockSpec((tm, d), lambda i, k: (i, 0))]
    out_shape = [jax.ShapeDtypeStruct((m, d), F32)]
    if norm:
        out_specs.append(pl.BlockSpec((tm, d), lambda i, k: (i, 0)))
        out_shape.append(jax.ShapeDtypeStruct((m, d), norm_dtype))
    outs = pl.pallas_call(
        functools.partial(_down_kernel, use_scale=use_scale, norm=norm),
        grid=(m // tm, kdim // tk), in_specs=in_specs, out_specs=out_specs,
        out_shape=out_shape, scratch_shapes=[pltpu.VMEM((tm, d), F32)],
        compiler_params=_cp(2), name="ffn_down")(*args)
    return (outs[0], outs[1]) if norm else (outs[0], None)


def _outproj_kernel(a_ref, c_ref, wa_ref, wc_ref, res_ref, g_ref, xo_ref, xn_ref):
    xo = res_ref[...] + (_dot(a_ref[...], wa_ref[...]) + _dot(c_ref[...], wc_ref[...]))
    xo_ref[...] = xo
    xn_ref[...] = _rms(xo, g_ref[...]).astype(xn_ref.dtype)


def _outproj(a, c, wa, wc, res, g, tm):
    m, d = res.shape
    ka, kc = a.shape[1], c.shape[1]
    row = lambda i: (i, 0)
    fixed = lambda i: (0, 0)
    return pl.pallas_call(
        _outproj_kernel, grid=(m // tm,),
        in_specs=[pl.BlockSpec((tm, ka), row), pl.BlockSpec((tm, kc), row),
                  pl.BlockSpec((ka, d), fixed), pl.BlockSpec((kc, d), fixed),
                  pl.BlockSpec((tm, d), row), pl.BlockSpec((1, d), fixed)],
        out_specs=[pl.BlockSpec((tm, d), row), pl.BlockSpec((tm, d), row)],
        out_shape=[jax.ShapeDtypeStruct((m, d), F32), jax.ShapeDtypeStruct((m, d), BF16)],
        compiler_params=_cp(1), name="out_proj")(a, c, wa, wc, res, g.reshape(1, d))


def _router_kernel(x_ref, r_ref, o_ref):
    logits = _dot(x_ref[...], r_ref[...])
    lane = lax.broadcasted_iota(jnp.int32, logits.shape, 1).astype(F32)
    lg = jnp.where(lane < N_EXPERTS, logits, -jnp.inf)
    m1 = jnp.max(lg, axis=1, keepdims=True)
    i1 = jnp.min(jnp.where(lg == m1, lane, float(LANES)), axis=1, keepdims=True)
    lg2 = jnp.where(lane == i1, -jnp.inf, lg)
    m2 = jnp.max(lg2, axis=1, keepdims=True)
    i2 = jnp.min(jnp.where(lg2 == m2, lane, float(LANES)), axis=1, keepdims=True)
    e2 = jnp.exp(m2 - m1)
    den = 1.0 + e2
    o_ref[...] = jnp.where(lane == i1, 1.0 / den, 0.0) + jnp.where(lane == i2, e2 / den, 0.0)


def _router(hn, r_pad, tm):
    m, d = hn.shape
    return pl.pallas_call(
        _router_kernel, grid=(m // tm,),
        in_specs=[pl.BlockSpec((tm, d), lambda i: (i, 0)),
                  pl.BlockSpec((d, LANES), lambda i: (0, 0))],
        out_specs=pl.BlockSpec((tm, LANES), lambda i: (i, 0)),
        out_shape=jax.ShapeDtypeStruct((m, LANES), F32),
        compiler_params=_cp(1), name="router")(hn, r_pad)


def _forget_kernel(x_ref, w_ref, b_ref, lf_ref, *ct_ref, cumsum):
    z = _dot(x_ref[...], w_ref[...]) + b_ref[...]
    lf = jnp.minimum(z, 0.0) - jnp.log1p(jnp.exp(-jnp.abs(z)))
    lf_ref[...] = lf
    if cumsum:
        r = lax.broadcasted_iota(jnp.int32, (LANES, LANES), 0)
        c = lax.broadcasted_iota(jnp.int32, (LANES, LANES), 1)
        tri = (c <= r).astype(F32)
        carry = jnp.zeros((1, LANES), F32)
        for blk in range(lf.shape[0] // LANES):
            cb = jnp.dot(tri, lf[blk * LANES:(blk + 1) * LANES, :], preferred_element_type=F32,
                         precision=lax.Precision.HIGHEST) + carry
            carry = cb[LANES - 1:LANES, :]
            ct_ref[0][0, :, blk * LANES:(blk + 1) * LANES] = cb.T[:ATT_HEADS, :]


def _forget(xn, wf_pad, bf_pad, rows, cumsum):
    m, d = xn.shape
    n = m // rows
    out_specs = [pl.BlockSpec((rows, LANES), lambda b: (b, 0))]
    out_shape = [jax.ShapeDtypeStruct((m, LANES), F32)]
    if cumsum:
        out_specs.append(pl.BlockSpec((1, ATT_HEADS, rows), lambda b: (b, 0, 0)))
        out_shape.append(jax.ShapeDtypeStruct((n, ATT_HEADS, rows), F32))
    return pl.pallas_call(
        functools.partial(_forget_kernel, cumsum=cumsum), grid=(n,),
        in_specs=[pl.BlockSpec((rows, d), lambda b: (b, 0)),
                  pl.BlockSpec((d, LANES), lambda b: (0, 0)),
                  pl.BlockSpec((1, LANES), lambda b: (0, 0))],
        out_specs=out_specs, out_shape=out_shape,
        compiler_params=_cp(1), name="forget_gate")(xn, wf_pad, bf_pad)


def _group_norm_store(yc, g_ref, o_ref):
    for gi in range(yc.shape[1] // HEAD_DIM):
        sl = slice(gi * HEAD_DIM, (gi + 1) * HEAD_DIM)
        yg = yc[:, sl]
        ms = jnp.mean(yg * yg, axis=-1, keepdims=True)
        o_ref[:, sl] = (yg * lax.rsqrt(ms + RMS_EPS) * g_ref[:, sl]).astype(o_ref.dtype)


def _conv_kernel(x_ref, wb_ref, wc_ref, wh_ref, cw_ref, g_ref, o_ref, tail_ref, halo_scr,
                 *, tiles_per_seq):
    i, j = pl.program_id(0), pl.program_id(1)
    x = x_ref[...]
    gate_b = _dot(x, wb_ref[...])
    u = _dot(x, wc_ref[...]) * _dot(x, wh_ref[...])
    tm = u.shape[0]

    @pl.when(i % tiles_per_seq == 0)
    def _():
        halo_scr[j] = jnp.zeros(halo_scr.shape[1:], F32)

    prev = halo_scr[j]
    row = lax.broadcasted_iota(jnp.int32, u.shape, 0)
    u1 = jnp.where(row == 0, prev[1:2, :], pltpu.roll(u, 1, axis=0))
    u2 = jnp.where(row == 0, prev[0:1, :],
                   jnp.where(row == 1, prev[1:2, :], pltpu.roll(u, 2, axis=0)))
    cw = cw_ref[...]
    y = u2 * cw[0:1, :] + u1 * cw[1:2, :] + u * cw[2:3, :]
    _group_norm_store(gate_b * y, g_ref, o_ref)
    tail = u[tm - (CONV_W - 1):tm, :]
    halo_scr[j] = tail
    tail_ref[0] = tail


def _conv_branch(xn, wb, wc, wh, cw, g, seq, tm, tc):
    m, d = xn.shape
    n_i, n_j = m // tm, CONV_DIM // tc
    col = lambda i, j: (0, j)
    return pl.pallas_call(
        functools.partial(_conv_kernel, tiles_per_seq=seq // tm), grid=(n_i, n_j),
        in_specs=[pl.BlockSpec((tm, d), lambda i, j: (i, 0)),
                  pl.BlockSpec((d, tc), col), pl.BlockSpec((d, tc), col),
                  pl.BlockSpec((d, tc), col), pl.BlockSpec((CONV_W, tc), col),
                  pl.BlockSpec((1, tc), col)],
        out_specs=[pl.BlockSpec((tm, tc), lambda i, j: (i, j)),
                   pl.BlockSpec((1, CONV_W - 1, tc), lambda i, j: (i, 0, j))],
        out_shape=[jax.ShapeDtypeStruct((m, CONV_DIM), BF16),
                   jax.ShapeDtypeStruct((n_i, CONV_W - 1, CONV_DIM), F32)],
        scratch_shapes=[pltpu.VMEM((n_j, CONV_W - 1, tc), F32)],
        compiler_params=_cp(2), name="conv_branch")(xn, wb, wc, wh, cw, g.reshape(1, CONV_DIM))


def _conv_step_kernel(x_ref, wb_ref, wc_ref, wh_ref, cw_ref, g_ref, p0_ref, p1_ref, o_ref, u_ref):
    x = x_ref[...]
    gate_b = _dot(x, wb_ref[...])
    u = _dot(x, wc_ref[...]) * _dot(x, wh_ref[...])
    cw = cw_ref[...]
    y = p0_ref[...] * cw[0:1, :] + p1_ref[...] * cw[1:2, :] + u * cw[2:3, :]
    _group_norm_store(gate_b * y, g_ref, o_ref)
    u_ref[...] = u


def _conv_step(xn, wb, wc, wh, cw, g, p0, p1, tc):
    m, d = xn.shape
    col = lambda j: (0, j)
    return pl.pallas_call(
        _conv_step_kernel, grid=(CONV_DIM // tc,),
        in_specs=[pl.BlockSpec((m, d), lambda j: (0, 0)),
                  pl.BlockSpec((d, tc), col), pl.BlockSpec((d, tc), col),
                  pl.BlockSpec((d, tc), col), pl.BlockSpec((CONV_W, tc), col),
                  pl.BlockSpec((1, tc), col), pl.BlockSpec((m, tc), col),
                  pl.BlockSpec((m, tc), col)],
        out_specs=[pl.BlockSpec((m, tc), col), pl.BlockSpec((m, tc), col)],
        out_shape=[jax.ShapeDtypeStruct((m, CONV_DIM), BF16),
                   jax.ShapeDtypeStruct((m, CONV_DIM), F32)],
        compiler_params=_cp(1), name="conv_step")(xn, wb, wc, wh, cw, g.reshape(1, CONV_DIM), p0, p1)


def _prompt_attn_kernel(q_ref, k_ref, v_ref, ct_ref, g_ref, o_ref, m_scr, l_scr, acc_scr):
    qi, ki = pl.program_id(1), pl.program_id(2)
    tq, tk = q_ref.shape[1], k_ref.shape[1]

    @pl.when(ki == 0)
    def _():
        m_scr[...] = jnp.full(m_scr.shape, NEG_BIG, F32)
        l_scr[...] = jnp.zeros_like(l_scr)
        acc_scr[...] = jnp.zeros_like(acc_scr)

    def update(masked):
        if masked:
            row = lax.broadcasted_iota(jnp.int32, (tq, tk), 0)
            col = lax.broadcasted_iota(jnp.int32, (tq, tk), 1)
            keep = col <= row
        for h in range(ATT_HEADS):
            sl = slice(h * HEAD_DIM, (h + 1) * HEAD_DIM)
            q = q_ref[0, :, sl]
            k = k_ref[0, :, sl].astype(BF16)
            s = lax.dot_general(q, k, (((1,), (1,)), ((), ())), preferred_element_type=F32)
            s = s * ATT_SCALE - ct_ref[0, h:h + 1, :]
            if masked:
                s = jnp.where(keep, s, NEG_BIG)
            m_prev = m_scr[h]
            m_new = jnp.maximum(m_prev, jnp.max(s, axis=1, keepdims=True))
            alpha = jnp.exp(m_prev - m_new)
            p = jnp.exp(s - m_new)
            l_scr[h] = alpha * l_scr[h] + jnp.sum(p, axis=1, keepdims=True)
            acc_scr[h] = alpha * acc_scr[h] + _dot(p.astype(BF16), v_ref[0, :, sl].astype(BF16))
            m_scr[h] = m_new

    @pl.when(ki < qi)
    def _():
        update(False)

    @pl.when(ki == qi)
    def _():
        update(True)
        for h in range(ATT_HEADS):
            sl = slice(h * HEAD_DIM, (h + 1) * HEAD_DIM)
            o = acc_scr[h] / l_scr[h]
            ms = jnp.mean(o * o, axis=-1, keepdims=True)
            o_ref[0, :, sl] = (o * lax.rsqrt(ms + RMS_EPS) * g_ref[:, sl]).astype(o_ref.dtype)


def _prompt_attention(q, k, v, ct, g, blk):
    n, s, _ = q.shape
    nb = s // blk
    kv_map = lambda b, qi, ki: (b, jnp.minimum(ki, qi), 0)
    return pl.pallas_call(
        _prompt_attn_kernel, grid=(n, nb, nb),
        in_specs=[pl.BlockSpec((1, blk, ATT_DIM), lambda b, qi, ki: (b, qi, 0)),
                  pl.BlockSpec((1, blk, ATT_DIM), kv_map),
                  pl.BlockSpec((1, blk, ATT_DIM), kv_map),
                  pl.BlockSpec((1, ATT_HEADS, blk), lambda b, qi, ki: (b, 0, jnp.minimum(ki, qi))),
                  pl.BlockSpec((1, ATT_DIM), lambda b, qi, ki: (0, 0))],
        out_specs=pl.BlockSpec((1, blk, ATT_DIM), lambda b, qi, ki: (b, qi, 0)),
        out_shape=jax.ShapeDtypeStruct((n, s, ATT_DIM), BF16),
        scratch_shapes=[pltpu.VMEM((ATT_HEADS, blk, 1), F32),
                        pltpu.VMEM((ATT_HEADS, blk, 1), F32),
                        pltpu.VMEM((ATT_HEADS, blk, HEAD_DIM), F32)],
        compiler_params=_cp(3), name="prompt_attention")(q, k, v, ct, g.reshape(1, ATT_DIM))


def _decode_attn_kernel(pt_ref, q_ref, kn_ref, vn_ref, lfn_ref, g_ref, *rest):
    del pt_ref
    npg = PAGES_PER_STEP
    k_refs, v_refs, lf_refs = rest[:npg], rest[npg:2 * npg], rest[2 * npg:3 * npg]
    o_ref, m_scr, l_scr, acc_scr, carry_scr, lf_scr = rest[3 * npg:]
    c = pl.program_id(1)
    width = ATT_HEADS * HEAD_DIM

    @pl.when(c == 0)
    def _():
        m_scr[...] = jnp.full(m_scr.shape, NEG_BIG, F32)
        l_scr[...] = jnp.zeros_like(l_scr)
        acc_scr[...] = jnp.zeros_like(acc_scr)
        carry_scr[...] = lfn_ref[0]

    lane = lax.broadcasted_iota(jnp.int32, (npg, width), 1)
    for p in range(npg):
        lf_scr[p:p + 1, :] = lf_refs[p][0, 0]
    lf = lf_scr[...]
    y = lf
    sh = ATT_HEADS
    while sh < width:
        y = y + jnp.where(lane < width - sh, pltpu.roll(y, width - sh, axis=1), 0.0)
        sh *= 2
    excl = y - lf
    z = jnp.where(lane < ATT_HEADS, y, 0.0)
    sh = ATT_HEADS
    while sh < width:
        z = z + pltpu.roll(z, sh, axis=1)
        sh *= 2
    carry = carry_scr[...]
    bias = [None] * npg
    for p in reversed(range(npg)):
        bias[p] = carry + excl[p:p + 1, :]
        carry = carry + z[p:p + 1, :]
    carry_scr[...] = carry

    q = q_ref[0]
    hrow = lax.broadcasted_iota(jnp.int32, (ATT_HEADS, width), 0)
    hlane = lax.broadcasted_iota(jnp.int32, (ATT_HEADS, width), 1)
    own = jnp.bitwise_and(hlane, ATT_HEADS - 1) == hrow
    s_all = []
    for p in range(npg):
        kp = k_refs[p][0, 0].astype(BF16)
        s = lax.dot_general(q, kp, (((1,), (1,)), ((), ())), preferred_element_type=F32)
        s_all.append(jnp.where(own, s * ATT_SCALE + bias[p], NEG_BIG))
    m_prev = m_scr[...]
    m_new = m_prev
    for s in s_all:
        m_new = jnp.maximum(m_new, jnp.max(s, axis=1, keepdims=True))
    alpha = jnp.exp(m_prev - m_new)
    l_new = alpha * l_scr[...]
    acc = alpha * acc_scr[...]
    for p in range(npg):
        pr = jnp.exp(s_all[p] - m_new)
        l_new = l_new + jnp.sum(pr, axis=1, keepdims=True)
        acc = acc + _dot(pr.astype(BF16), v_refs[p][0, 0].astype(BF16))
    m_scr[...] = m_new
    l_scr[...] = l_new
    acc_scr[...] = acc

    @pl.when(c == pl.num_programs(1) - 1)
    def _():
        kn = kn_ref[0].astype(BF16).astype(F32)
        vn = vn_ref[0].astype(BF16).astype(F32)
        s_new = jnp.sum(q.astype(F32) * kn, axis=1, keepdims=True) * ATT_SCALE
        m_fin = jnp.maximum(m_new, s_new)
        a = jnp.exp(m_new - m_fin)
        pn = jnp.exp(s_new - m_fin)
        o = (a * acc + pn * vn) / (a * l_new + pn)
        ms = jnp.mean(o * o, axis=-1, keepdims=True)
        o_ref[0] = (o * lax.rsqrt(ms + RMS_EPS) * g_ref[...]).astype(o_ref.dtype)


def _decode_attention(layer, page_table, q, kn, vn, lfn, g, cache_k, cache_v, cache_logf):
    n, n_pages = page_table.shape
    depth, n_pool, page, heads, hd = cache_k.shape
    width = page * heads
    ck = cache_k.reshape(depth, n_pool, width, hd)
    cv = cache_v.reshape(depth, n_pool, width, hd)
    cl = cache_logf.reshape(depth, n_pool, 1, width)
    n_chunks = n_pages // PAGES_PER_STEP

    def page_map(p):
        return lambda b, c, pt: (layer, pt[b, (n_chunks - 1 - c) * PAGES_PER_STEP + p], 0, 0)

    per_seq = lambda b, c, pt: (b, 0, 0)
    in_specs = [pl.BlockSpec((1, heads, hd), per_seq), pl.BlockSpec((1, heads, hd), per_seq),
                pl.BlockSpec((1, heads, hd), per_seq), pl.BlockSpec((1, 1, width), per_seq),
                pl.BlockSpec((heads, hd), lambda b, c, pt: (0, 0))]
    in_specs += [pl.BlockSpec((1, 1, width, hd), page_map(p)) for p in range(PAGES_PER_STEP)]
    in_specs += [pl.BlockSpec((1, 1, width, hd), page_map(p)) for p in range(PAGES_PER_STEP)]
    in_specs += [pl.BlockSpec((1, 1, 1, width), page_map(p)) for p in range(PAGES_PER_STEP)]
    grid_spec = pltpu.PrefetchScalarGridSpec(
        num_scalar_prefetch=1, grid=(n, n_chunks), in_specs=in_specs,
        out_specs=pl.BlockSpec((1, heads, hd), per_seq),
        scratch_shapes=[pltpu.VMEM((heads, 1), F32), pltpu.VMEM((heads, 1), F32),
                        pltpu.VMEM((heads, hd), F32), pltpu.VMEM((1, width), F32),
                        pltpu.VMEM((PAGES_PER_STEP, width), F32)])
    return pl.pallas_call(
        _decode_attn_kernel, grid_spec=grid_spec,
        out_shape=jax.ShapeDtypeStruct((n, heads, hd), BF16),
        compiler_params=_cp(2), name="decode_attention")(
            page_table, q, kn, vn, lfn, g.reshape(heads, hd),
            *([ck] * PAGES_PER_STEP), *([cv] * PAGES_PER_STEP), *([cl] * PAGES_PER_STEP))


def _pad_lanes(w):
    return jnp.pad(w, ((0, 0), (0, LANES - w.shape[1])))


def kernel(x_prompt, x_sample, cache_k, cache_v, cache_logf, state_conv, page_table, norm_mix_g,
           w_in, b_forget, conv_w, g_att_out, g_conv_out, w_out, norm_ffn_g, dense_w_gate,
           dense_w_up, dense_w_down, moe_router, moe_w_gate, moe_w_up, moe_w_down, final_norm_g):
    n, seq, d = x_prompt.shape
    n_dec, dec_seq, _ = x_sample.shape
    assert dec_seq == 1 and d == D_MODEL
    depth = w_in.shape[0]
    mp = n * seq
    tm_p, tm_s = 512, n_dec

    xp = x_prompt.reshape(mp, d)
    xs = x_sample.reshape(n_dec, d)
    xnp = _rmsnorm(xp, norm_mix_g[0], tm_p)
    xns = _rmsnorm(xs, norm_mix_g[0], tm_s)

    outs = {key: [] for key in ("kp", "vp", "fp", "cp", "ks", "vs", "fs", "cs")}
    yp = ys = None
    for l in range(depth):
        wl = w_in[l]
        o = 0
        wq = wl[:, o:o + ATT_DIM].astype(BF16); o += ATT_DIM
        wk = wl[:, o:o + ATT_DIM].astype(BF16); o += ATT_DIM
        wv = wl[:, o:o + ATT_DIM].astype(BF16); o += ATT_DIM
        wf = _pad_lanes(wl[:, o:o + ATT_HEADS]).astype(BF16); o += ATT_HEADS
        wb = wl[:, o:o + CONV_DIM].astype(BF16); o += CONV_DIM
        wc = wl[:, o:o + CONV_DIM].astype(BF16); o += CONV_DIM
        wh = wl[:, o:o + CONV_DIM].astype(BF16)
        bf = _pad_lanes(b_forget[l].reshape(1, ATT_HEADS))
        wo_a = w_out[l][:ATT_DIM].astype(BF16)
        wo_c = w_out[l][ATT_DIM:].astype(BF16)

        q_p = _mm(xnp, wq, BF16, 1024, 512, "q_proj")
        k_p = _mm(xnp, wk, F32, 1024, 512, "k_proj")
        v_p = _mm(xnp, wv, F32, 1024, 512, "v_proj")
        lf_p, ct_p = _forget(xnp, wf, bf, seq, True)
        conv_p, tails = _conv_branch(xnp, wb, wc, wh, conv_w[l], g_conv_out[l], seq, 512, 512)
        att_p = _prompt_attention(q_p.reshape(n, seq, ATT_DIM), k_p.reshape(n, seq, ATT_DIM),
                                  v_p.reshape(n, seq, ATT_DIM), ct_p, g_att_out[l], 512)
        xp, hnp = _outproj(att_p.reshape(mp, ATT_DIM), conv_p, wo_a, wo_c, xp, norm_ffn_g[l], tm_p)
        outs["kp"].append(k_p.reshape(n, seq, ATT_HEADS, HEAD_DIM))
        outs["vp"].append(v_p.reshape(n, seq, ATT_HEADS, HEAD_DIM))
        outs["fp"].append(lf_p[:, :ATT_HEADS].reshape(n, seq, ATT_HEADS))
        tiles_per_seq = seq // 512
        outs["cp"].append(tails[tiles_per_seq - 1::tiles_per_seq])

        q_s = _mm(xns, wq, BF16, tm_s, 512, "q_proj_s")
        k_s = _mm(xns, wk, F32, tm_s, 512, "k_proj_s")
        v_s = _mm(xns, wv, F32, tm_s, 512, "v_proj_s")
        lf_s = _forget(xns, wf, bf, n_dec, False)[0]
        st = state_conv[l]
        conv_s, u_s = _conv_step(xns, wb, wc, wh, conv_w[l], g_conv_out[l], st[:, 0], st[:, 1], 512)
        lfn = jnp.tile(lf_s[:, :ATT_HEADS], (1, cache_k.shape[2])).reshape(n_dec, 1, -1)
        att_s = _decode_attention(l, page_table, q_s.reshape(n_dec, ATT_HEADS, HEAD_DIM),
                                  k_s.reshape(n_dec, ATT_HEADS, HEAD_DIM),
                                  v_s.reshape(n_dec, ATT_HEADS, HEAD_DIM), lfn, g_att_out[l],
                                  cache_k, cache_v, cache_logf)
        xs, hns = _outproj(att_s.reshape(n_dec, ATT_DIM), conv_s, wo_a, wo_c, xs, norm_ffn_g[l], tm_s)
        outs["ks"].append(k_s.reshape(n_dec, 1, ATT_HEADS, HEAD_DIM))
        outs["vs"].append(v_s.reshape(n_dec, 1, ATT_HEADS, HEAD_DIM))
        outs["fs"].append(lf_s[:, :ATT_HEADS].reshape(n_dec, 1, ATT_HEADS))
        outs["cs"].append(jnp.stack([st[:, 1], u_s], axis=1))

        last = l == depth - 1
        next_g = final_norm_g if last else norm_mix_g[l + 1]
        next_dtype = F32 if last else BF16
        i = l // 2
        if l % 2 == 0:
            wg = dense_w_gate[i].astype(BF16)
            wu = dense_w_up[i].astype(BF16)
            wd = dense_w_down[i].astype(BF16)
            hp = _swiglu(hnp, wg, wu, 1024, 512)
            xp, nxp = _down(hp, wd, xp, tm_p, 1408, norm_g=next_g, norm_dtype=next_dtype)
            hs = _swiglu(hns, wg, wu, tm_s, 512)
            xs, nxs = _down(hs, wd, xs, tm_s, 1408, norm_g=next_g, norm_dtype=next_dtype)
        else:
            r_pad = _pad_lanes(moe_router[i]).astype(BF16)
            gates_p = _router(hnp, r_pad, tm_p)
            gates_s = _router(hns, r_pad, tm_s)
            nxp = nxs = None
            for e in range(N_EXPERTS):
                wg = moe_w_gate[i, e].astype(BF16)
                wu = moe_w_up[i, e].astype(BF16)
                wd = moe_w_down[i, e].astype(BF16)
                fin = e == N_EXPERTS - 1
                hp = _swiglu(hnp, wg, wu, 512, 1408)
                xp, nxp = _down(hp, wd, xp, tm_p, 1408, scale=gates_p[:, e:e + 1],
                                norm_g=next_g if fin else None, norm_dtype=next_dtype)
                hs = _swiglu(hns, wg, wu, tm_s, 1408)
                xs, nxs = _down(hs, wd, xs, tm_s, 1408, scale=gates_s[:, e:e + 1],
                                norm_g=next_g if fin else None, norm_dtype=next_dtype)
        if last:
            yp, ys = nxp, nxs
        else:
            xnp, xns = nxp, nxs

    return (yp.reshape(n, seq, d), ys.reshape(n_dec, 1, d),
            jnp.stack(outs["kp"]), jnp.stack(outs["vp"]), jnp.stack(outs["fp"]), jnp.stack(outs["cp"]),
            jnp.stack(outs["ks"]), jnp.stack(outs["vs"]), jnp.stack(outs["fs"]), jnp.stack(outs["cs"]))
```

```python
import functools

import jax
import jax.numpy as jnp
from jax import lax
from jax.experimental import pallas as pl
from jax.experimental.pallas import tpu as pltpu

D_MODEL = 2048
HEAD_DIM = 128
ATT_HEADS = 8
ATT_DIM = ATT_HEADS * HEAD_DIM
CONV_DIM = D_MODEL - ATT_DIM
CONV_W = 3
N_EXPERTS = 8
TOP_K = 2
ATT_SCALE = HEAD_DIM ** -0.5
RMS_EPS = 1e-6
LANES = 128
NEG_BIG = -1e30
PAGES_PER_STEP = 8
VMEM_LIMIT = 56 * 1024 * 1024

F32 = jnp.float32
BF16 = jnp.bfloat16


def _cp(n_axes):
    return pltpu.CompilerParams(dimension_semantics=("arbitrary",) * n_axes,
                                vmem_limit_bytes=VMEM_LIMIT)


def _rms(x, g):
    return x * lax.rsqrt(jnp.mean(x * x, axis=-1, keepdims=True) + RMS_EPS) * g


def _dot(a, b):
    return jnp.dot(a, b, preferred_element_type=F32)


def _rmsnorm_kernel(x_ref, g_ref, o_ref):
    o_ref[...] = _rms(x_ref[...], g_ref[...]).astype(o_ref.dtype)


def _rmsnorm(x, g, tm, out_dtype=BF16):
    m, d = x.shape
    return pl.pallas_call(
        _rmsnorm_kernel, grid=(m // tm,),
        in_specs=[pl.BlockSpec((tm, d), lambda i: (i, 0)),
                  pl.BlockSpec((1, d), lambda i: (0, 0))],
        out_specs=pl.BlockSpec((tm, d), lambda i: (i, 0)),
        out_shape=jax.ShapeDtypeStruct((m, d), out_dtype),
        compiler_params=_cp(1), name="rmsnorm")(x, g.reshape(1, d))


def _mm_kernel(x_ref, w_ref, o_ref):
    o_ref[...] = _dot(x_ref[...], w_ref[...]).astype(o_ref.dtype)


def _mm(x, w, out_dtype, tm, tn, name):
    m, k = x.shape
    n = w.shape[1]
    return pl.pallas_call(
        _mm_kernel, grid=(m // tm, n // tn),
        in_specs=[pl.BlockSpec((tm, k), lambda i, j: (i, 0)),
                  pl.BlockSpec((k, tn), lambda i, j: (0, j))],
        out_specs=pl.BlockSpec((tm, tn), lambda i, j: (i, j)),
        out_shape=jax.ShapeDtypeStruct((m, n), out_dtype),
        compiler_params=_cp(2), name=name)(x, w)


def _swiglu_kernel(x_ref, wg_ref, wu_ref, o_ref):
    x = x_ref[...]
    a = _dot(x, wg_ref[...])
    b = _dot(x, wu_ref[...])
    o_ref[...] = (a * (1.0 / (1.0 + jnp.exp(-a))) * b).astype(o_ref.dtype)


def _swiglu(x, wg, wu, tm, tn):
    m, k = x.shape
    n = wg.shape[1]
    return pl.pallas_call(
        _swiglu_kernel, grid=(m // tm, n // tn),
        in_specs=[pl.BlockSpec((tm, k), lambda i, j: (i, 0)),
                  pl.BlockSpec((k, tn), lambda i, j: (0, j)),
                  pl.BlockSpec((k, tn), lambda i, j: (0, j))],
        out_specs=pl.BlockSpec((tm, tn), lambda i, j: (i, j)),
        out_shape=jax.ShapeDtypeStruct((m, n), BF16),
        compiler_params=_cp(2), name="swiglu_up")(x, wg, wu)


def _down_kernel(h_ref, w_ref, res_ref, g_ref, *rest, emit_x):
    xn_ref, acc_scr = rest[-2:]
    k = pl.program_id(1)

    @pl.when(k == 0)
    def _():
        acc_scr[...] = jnp.zeros_like(acc_scr)

    acc_scr[...] += _dot(h_ref[...], w_ref[...])

    @pl.when(k == pl.num_programs(1) - 1)
    def _():
        xo = res_ref[...] + acc_scr[...]
        if emit_x:
            rest[0][...] = xo
        xn_ref[...] = _rms(xo, g_ref[...]).astype(xn_ref.dtype)


def _down(h, w, res, norm_g, tm, tk, emit_x, norm_dtype):
    m, kdim = h.shape
    d = w.shape[1]
    row = lambda i, k: (i, 0)
    out_specs = [pl.BlockSpec((tm, d), row)]
    out_shape = [jax.ShapeDtypeStruct((m, d), norm_dtype)]
    if emit_x:
        out_specs.insert(0, pl.BlockSpec((tm, d), row))
        out_shape.insert(0, jax.ShapeDtypeStruct((m, d), F32))
    outs = pl.pallas_call(
        functools.partial(_down_kernel, emit_x=emit_x), grid=(m // tm, kdim // tk),
        in_specs=[pl.BlockSpec((tm, tk), lambda i, k: (i, k)),
                  pl.BlockSpec((tk, d), lambda i, k: (k, 0)),
                  pl.BlockSpec((tm, d), row), pl.BlockSpec((1, d), lambda i, k: (0, 0))],
        out_specs=out_specs, out_shape=out_shape, scratch_shapes=[pltpu.VMEM((tm, d), F32)],
        compiler_params=_cp(2), name="ffn_down")(h, w, res, norm_g.reshape(1, d))
    return (outs[0], outs[1]) if emit_x else (None, outs[0])


def _outproj_kernel(a_ref, c_ref, wa_ref, wc_ref, res_ref, g_ref, xo_ref, xn_ref):
    xo = res_ref[...] + (_dot(a_ref[...], wa_ref[...]) + _dot(c_ref[...], wc_ref[...]))
    xo_ref[...] = xo
    xn_ref[...] = _rms(xo, g_ref[...]).astype(xn_ref.dtype)


def _outproj(a, c, wa, wc, res, g, tm):
    m, d = res.shape
    ka, kc = a.shape[1], c.shape[1]
    row = lambda i: (i, 0)
    fixed = lambda i: (0, 0)
    return pl.pallas_call(
        _outproj_kernel, grid=(m // tm,),
        in_specs=[pl.BlockSpec((tm, ka), row), pl.BlockSpec((tm, kc), row),
                  pl.BlockSpec((ka, d), fixed), pl.BlockSpec((kc, d), fixed),
                  pl.BlockSpec((tm, d), row), pl.BlockSpec((1, d), fixed)],
        out_specs=[pl.BlockSpec((tm, d), row), pl.BlockSpec((tm, d), row)],
        out_shape=[jax.ShapeDtypeStruct((m, d), F32), jax.ShapeDtypeStruct((m, d), BF16)],
        compiler_params=_cp(1), name="out_proj")(a, c, wa, wc, res, g.reshape(1, d))


SEL_E1, SEL_E2, SEL_G1, SEL_G2, SEL_R1, SEL_R2 = range(6)


def _router_kernel(x_ref, r_ref, cin_ref, sel_ref, cnt_ref, carry_scr, *, n_valid):
    i = pl.program_id(0)
    tm = x_ref.shape[0]

    @pl.when(i == 0)
    def _():
        carry_scr[...] = cin_ref[...]

    logits = _dot(x_ref[...], r_ref[...])
    lane = lax.broadcasted_iota(jnp.int32, logits.shape, 1).astype(F32)
    lg = jnp.where(lane < N_EXPERTS, logits, -jnp.inf)
    m1 = jnp.max(lg, axis=1, keepdims=True)
    i1 = jnp.min(jnp.where(lg == m1, lane, float(LANES)), axis=1, keepdims=True)
    lg2 = jnp.where(lane == i1, -jnp.inf, lg)
    m2 = jnp.max(lg2, axis=1, keepdims=True)
    i2 = jnp.min(jnp.where(lg2 == m2, lane, float(LANES)), axis=1, keepdims=True)
    e2 = jnp.exp(m2 - m1)
    den = 1.0 + e2
    row = lax.broadcasted_iota(jnp.int32, logits.shape, 0) + i * tm
    hit = jnp.where(((lane == i1) | (lane == i2)) & (row < n_valid), 1.0, 0.0)
    r = lax.broadcasted_iota(jnp.int32, (tm, tm), 0)
    c = lax.broadcasted_iota(jnp.int32, (tm, tm), 1)
    before = (c < r).astype(BF16)
    rank = _dot(before, hit.astype(BF16)) + carry_scr[...]
    carry_scr[...] = carry_scr[...] + jnp.sum(hit, axis=0, keepdims=True)
    cnt_ref[...] = carry_scr[...]
    rank1 = jnp.sum(jnp.where(lane == i1, rank, 0.0), axis=1, keepdims=True)
    rank2 = jnp.sum(jnp.where(lane == i2, rank, 0.0), axis=1, keepdims=True)
    sel = jnp.zeros_like(logits)
    for idx, val in ((SEL_E1, i1), (SEL_E2, i2), (SEL_G1, 1.0 / den), (SEL_G2, e2 / den),
                     (SEL_R1, rank1), (SEL_R2, rank2)):
        sel = jnp.where(lane == idx, val, sel)
    sel_ref[...] = sel


def _router(hn, r_pad, count_in, tm, n_valid):
    m, d = hn.shape
    fixed = lambda i: (0, 0)
    return pl.pallas_call(
        functools.partial(_router_kernel, n_valid=n_valid), grid=(m // tm,),
        in_specs=[pl.BlockSpec((tm, d), lambda i: (i, 0)), pl.BlockSpec((d, LANES), fixed),
                  pl.BlockSpec((1, LANES), fixed)],
        out_specs=[pl.BlockSpec((tm, LANES), lambda i: (i, 0)), pl.BlockSpec((1, LANES), fixed)],
        out_shape=[jax.ShapeDtypeStruct((m, LANES), F32), jax.ShapeDtypeStruct((1, LANES), F32)],
        scratch_shapes=[pltpu.VMEM((1, LANES), F32)],
        compiler_params=_cp(1), name="router")(hn, r_pad, count_in)


def _dispatch_kernel(d1_ref, d2_ref, ps_ref, pc_ref, x_ref, *rest, first):
    xs_ref, zero_scr, sem = rest[-3:]
    tt = x_ref.shape[0]
    base = pl.program_id(0) * tt

    def row_copy(src_row, dst_row):
        return pltpu.make_async_copy(src_row, xs_ref.at[pl.ds(dst_row, 1), :], sem)

    def issue(r, carry):
        src = x_ref.at[pl.ds(r, 1), :]
        row_copy(src, d1_ref[base + r]).start()
        row_copy(src, d2_ref[base + r]).start()
        return carry

    def drain(r, carry):
        row_copy(x_ref.at[pl.ds(0, 1), :], 0).wait()
        row_copy(x_ref.at[pl.ds(0, 1), :], 0).wait()
        return carry

    lax.fori_loop(0, tt, issue, 0, unroll=8)
    lax.fori_loop(0, tt, drain, 0, unroll=8)

    if first:
        @pl.when(pl.program_id(0) == 0)
        def _():
            zero_scr[...] = jnp.zeros_like(zero_scr)
            zrow = zero_scr.at[pl.ds(0, 1), :]
            for e in range(N_EXPERTS):
                def fill(r, carry, e=e):
                    row_copy(zrow, ps_ref[e] + r).start()
                    return carry

                def fill_done(r, carry):
                    row_copy(zrow, 0).wait()
                    return carry

                lax.fori_loop(0, pc_ref[e], fill, 0)
                lax.fori_loop(0, pc_ref[e], fill_done, 0)


def _dispatch(d1, d2, pad_start, pad_count, x, xs, tt, n_rows):
    m, d = x.shape
    first = xs is None
    in_specs = [pl.BlockSpec((tt, d), lambda i, *_: (i, 0))]
    args = [d1, d2, pad_start, pad_count, x]
    aliases = {}
    if not first:
        in_specs.append(pl.BlockSpec(memory_space=pl.ANY))
        args.append(xs)
        aliases = {5: 0}
    grid_spec = pltpu.PrefetchScalarGridSpec(
        num_scalar_prefetch=4, grid=(m // tt,), in_specs=in_specs,
        out_specs=pl.BlockSpec(memory_space=pl.ANY),
        scratch_shapes=[pltpu.VMEM((8, d), F32), pltpu.SemaphoreType.DMA(())])
    return pl.pallas_call(
        functools.partial(_dispatch_kernel, first=first), grid_spec=grid_spec,
        out_shape=jax.ShapeDtypeStruct((n_rows, d), F32), input_output_aliases=aliases,
        compiler_params=_cp(1), name="moe_dispatch")(*args)


def _moe_up_kernel(te_ref, na_ref, x_ref, g_ref, wg_ref, wu_ref, o_ref):
    del te_ref

    @pl.when(pl.program_id(1) < na_ref[0])
    def _():
        xn = _rms(x_ref[...], g_ref[...]).astype(BF16)
        a = _dot(xn, wg_ref[0])
        b = _dot(xn, wu_ref[0])
        o_ref[...] = (a * (1.0 / (1.0 + jnp.exp(-a))) * b).astype(o_ref.dtype)


def _moe_up(tile_expert, n_active, xs, g, wg, wu, tm, tn):
    r, d = xs.shape
    f = wg.shape[2]
    tile = lambda t, na: jnp.minimum(t, na[0] - 1)
    w_map = lambda j, t, te, na: (te[tile(t, na)], 0, j)
    grid_spec = pltpu.PrefetchScalarGridSpec(
        num_scalar_prefetch=2, grid=(f // tn, r // tm),
        in_specs=[pl.BlockSpec((tm, d), lambda j, t, te, na: (tile(t, na), 0)),
                  pl.BlockSpec((1, d), lambda j, t, te, na: (0, 0)),
                  pl.BlockSpec((1, d, tn), w_map), pl.BlockSpec((1, d, tn), w_map)],
        out_specs=pl.BlockSpec((tm, tn), lambda j, t, te, na: (tile(t, na), j)))
    return pl.pallas_call(
        _moe_up_kernel, grid_spec=grid_spec, out_shape=jax.ShapeDtypeStruct((r, f), BF16),
        compiler_params=_cp(2), name="moe_up")(tile_expert, n_active, xs, g.reshape(1, d), wg, wu)


def _moe_down_kernel(te_ref, na_ref, h_ref, w_ref, o_ref):
    del te_ref

    @pl.when(pl.program_id(0) < na_ref[0])
    def _():
        o_ref[...] = _dot(h_ref[...], w_ref[0])


def _moe_down(tile_expert, n_active, h, wd, tm):
    r, f = h.shape
    d = wd.shape[2]
    tile = lambda t, na: jnp.minimum(t, na[0] - 1)
    grid_spec = pltpu.PrefetchScalarGridSpec(
        num_scalar_prefetch=2, grid=(r // tm,),
        in_specs=[pl.BlockSpec((tm, f), lambda t, te, na: (tile(t, na), 0)),
                  pl.BlockSpec((1, f, d), lambda t, te, na: (te[tile(t, na)], 0, 0))],
        out_specs=pl.BlockSpec((tm, d), lambda t, te, na: (tile(t, na), 0)))
    return pl.pallas_call(
        _moe_down_kernel, grid_spec=grid_spec, out_shape=jax.ShapeDtypeStruct((r, d), F32),
        compiler_params=_cp(1), name="moe_down")(tile_expert, n_active, h, wd)


def _combine_kernel(d1_ref, d2_ref, ys_ref, res_ref, sel_ref, g_ref, *rest, emit_x):
    y1_scr, y2_scr, sem = rest[-3:]
    outs = rest[:-3]
    tt = res_ref.shape[0]
    base = pl.program_id(0) * tt

    def row_copy(src_row, dst):
        return pltpu.make_async_copy(ys_ref.at[pl.ds(src_row, 1), :], dst, sem)

    def issue(r, carry):
        row_copy(d1_ref[base + r], y1_scr.at[pl.ds(r, 1), :]).start()
        row_copy(d2_ref[base + r], y2_scr.at[pl.ds(r, 1), :]).start()
        return carry

    def drain(r, carry):
        row_copy(0, y1_scr.at[pl.ds(0, 1), :]).wait()
        row_copy(0, y2_scr.at[pl.ds(0, 1), :]).wait()
        return carry

    lax.fori_loop(0, tt, issue, 0, unroll=8)
    lax.fori_loop(0, tt, drain, 0, unroll=8)
    sel = sel_ref[...]
    g1 = sel[:, SEL_G1:SEL_G1 + 1]
    g2 = sel[:, SEL_G2:SEL_G2 + 1]
    xo = res_ref[...] + (g1 * y1_scr[...] + g2 * y2_scr[...])
    if emit_x:
        outs[0][...] = xo
    outs[-1][...] = _rms(xo, g_ref[...]).astype(outs[-1].dtype)


def _combine(d1, d2, ys, res, sel, g, tt, emit_x, norm_dtype):
    m, d = res.shape
    row = lambda i, *_: (i, 0)
    out_specs = [pl.BlockSpec((tt, d), row)]
    out_shape = [jax.ShapeDtypeStruct((m, d), norm_dtype)]
    if emit_x:
        out_specs.insert(0, pl.BlockSpec((tt, d), row))
        out_shape.insert(0, jax.ShapeDtypeStruct((m, d), F32))
    grid_spec = pltpu.PrefetchScalarGridSpec(
        num_scalar_prefetch=2, grid=(m // tt,),
        in_specs=[pl.BlockSpec(memory_space=pl.ANY), pl.BlockSpec((tt, d), row),
                  pl.BlockSpec((tt, LANES), row), pl.BlockSpec((1, d), lambda i, *_: (0, 0))],
        out_specs=out_specs,
        scratch_shapes=[pltpu.VMEM((tt, d), F32), pltpu.VMEM((tt, d), F32),
                        pltpu.SemaphoreType.DMA(())])
    outs = pl.pallas_call(
        functools.partial(_combine_kernel, emit_x=emit_x), grid_spec=grid_spec,
        out_shape=out_shape, compiler_params=_cp(1), name="moe_combine")(
            d1, d2, ys, res, sel, g.reshape(1, d))
    return (outs[0], outs[1]) if emit_x else (None, outs[0])


def _moe_ffn(x_p, hn_p, x_s, hn_s, ffn_g, router, wg, wu, wd, next_g, emit_x, norm_dtype,
             tm=512, tn=1408, t_route=512, t_move=512, t_comb=256):
    mp, d = x_p.shape
    ms = x_s.shape[0]
    n_tiles = (TOP_K * (mp + ms)) // tm + N_EXPERTS
    r_pad = _pad_lanes(router).astype(BF16)
    ms_pad = -(-ms // LANES) * LANES
    hn_s_pad = jnp.pad(hn_s, ((0, ms_pad - ms), (0, 0)))
    sel_p, cnt_p = _router(hn_p, r_pad, jnp.zeros((1, LANES), F32), t_route, mp)
    sel_s, cnt = _router(hn_s_pad, r_pad, cnt_p, ms_pad, ms)

    count = cnt[0, :N_EXPERTS].astype(jnp.int32)
    padded = (count + tm - 1) // tm * tm
    g_end = jnp.cumsum(padded)
    g_start = g_end - padded
    tile_expert = jnp.minimum(jnp.searchsorted(g_end // tm, jnp.arange(n_tiles), side="right"),
                              N_EXPERTS - 1).astype(jnp.int32)
    n_active = (g_end[-1:] // tm).astype(jnp.int32)

    def dests(sel, rows):
        e1, e2 = sel[:rows, SEL_E1].astype(jnp.int32), sel[:rows, SEL_E2].astype(jnp.int32)
        return (g_start[e1] + sel[:rows, SEL_R1].astype(jnp.int32),
                g_start[e2] + sel[:rows, SEL_R2].astype(jnp.int32))

    d1_p, d2_p = dests(sel_p, mp)
    d1_s, d2_s = dests(sel_s, ms)
    pad_start = g_start + count
    pad_count = padded - count
    xs = _dispatch(d1_p, d2_p, pad_start, pad_count, x_p, None, t_move, n_tiles * tm)
    xs = _dispatch(d1_s, d2_s, pad_start, pad_count, x_s, xs, ms, n_tiles * tm)
    h = _moe_up(tile_expert, n_active, xs, ffn_g, wg, wu, tm, tn)
    ys = _moe_down(tile_expert, n_active, h, wd, tm)
    xo_p, xn_p = _combine(d1_p, d2_p, ys, x_p, sel_p, next_g, t_comb, emit_x, norm_dtype)
    xo_s, xn_s = _combine(d1_s, d2_s, ys, x_s, sel_s, next_g, ms, emit_x, norm_dtype)
    return xo_p, xn_p, xo_s, xn_s


def _forget_kernel(x_ref, w_ref, b_ref, lf_ref, *c_ref, cumsum):
    z = _dot(x_ref[...], w_ref[...]) + b_ref[...]
    lf = jnp.minimum(z, 0.0) - jnp.log1p(jnp.exp(-jnp.abs(z)))
    lf_ref[...] = lf
    if cumsum:
        r = lax.broadcasted_iota(jnp.int32, (LANES, LANES), 0)
        c = lax.broadcasted_iota(jnp.int32, (LANES, LANES), 1)
        tri = (c <= r).astype(F32)
        carry = jnp.zeros((1, LANES), F32)
        for blk in range(lf.shape[0] // LANES):
            rows = slice(blk * LANES, (blk + 1) * LANES)
            cb = jnp.dot(tri, lf[rows, :], preferred_element_type=F32,
                         precision=lax.Precision.HIGHEST) + carry
            carry = cb[LANES - 1:LANES, :]
            c_ref[0][rows, :] = cb


def _forget(xn, wf_pad, bf_pad, rows, cumsum):
    m, d = xn.shape
    n = m // rows
    out_specs = [pl.BlockSpec((rows, LANES), lambda b: (b, 0))]
    out_shape = [jax.ShapeDtypeStruct((m, LANES), F32)]
    if cumsum:
        out_specs.append(pl.BlockSpec((rows, LANES), lambda b: (b, 0)))
        out_shape.append(jax.ShapeDtypeStruct((m, LANES), F32))
    return pl.pallas_call(
        functools.partial(_forget_kernel, cumsum=cumsum), grid=(n,),
        in_specs=[pl.BlockSpec((rows, d), lambda b: (b, 0)),
                  pl.BlockSpec((d, LANES), lambda b: (0, 0)),
                  pl.BlockSpec((1, LANES), lambda b: (0, 0))],
        out_specs=out_specs, out_shape=out_shape,
        compiler_params=_cp(1), name="forget_gate")(xn, wf_pad, bf_pad)


def _group_norm_store(yc, g_ref, o_ref):
    for gi in range(yc.shape[1] // HEAD_DIM):
        sl = slice(gi * HEAD_DIM, (gi + 1) * HEAD_DIM)
        yg = yc[:, sl]
        ms = jnp.mean(yg * yg, axis=-1, keepdims=True)
        o_ref[:, sl] = (yg * lax.rsqrt(ms + RMS_EPS) * g_ref[:, sl]).astype(o_ref.dtype)


def _conv_kernel(x_ref, wb_ref, wc_ref, wh_ref, cw_ref, g_ref, o_ref, tail_ref, halo_scr,
                 *, tiles_per_seq):
    i, j = pl.program_id(0), pl.program_id(1)
    x = x_ref[...]
    gate_b = _dot(x, wb_ref[...])
    u = _dot(x, wc_ref[...]) * _dot(x, wh_ref[...])
    tm = u.shape[0]

    @pl.when(i % tiles_per_seq == 0)
    def _():
        halo_scr[j] = jnp.zeros(halo_scr.shape[1:], F32)

    prev = halo_scr[j]
    row = lax.broadcasted_iota(jnp.int32, u.shape, 0)
    u1 = jnp.where(row == 0, prev[1:2, :], pltpu.roll(u, 1, axis=0))
    u2 = jnp.where(row == 0, prev[0:1, :],
                   jnp.where(row == 1, prev[1:2, :], pltpu.roll(u, 2, axis=0)))
    cw = cw_ref[...]
    y = u2 * cw[0:1, :] + u1 * cw[1:2, :] + u * cw[2:3, :]
    _group_norm_store(gate_b * y, g_ref, o_ref)
    tail = u[tm - (CONV_W - 1):tm, :]
    halo_scr[j] = tail
    tail_ref[0] = tail


def _conv_branch(xn, wb, wc, wh, cw, g, seq, tm, tc):
    m, d = xn.shape
    n_i, n_j = m // tm, CONV_DIM // tc
    col = lambda i, j: (0, j)
    return pl.pallas_call(
        functools.partial(_conv_kernel, tiles_per_seq=seq // tm), grid=(n_i, n_j),
        in_specs=[pl.BlockSpec((tm, d), lambda i, j: (i, 0)),
                  pl.BlockSpec((d, tc), col), pl.BlockSpec((d, tc), col),
                  pl.BlockSpec((d, tc), col), pl.BlockSpec((CONV_W, tc), col),
                  pl.BlockSpec((1, tc), col)],
        out_specs=[pl.BlockSpec((tm, tc), lambda i, j: (i, j)),
                   pl.BlockSpec((1, CONV_W - 1, tc), lambda i, j: (i, 0, j))],
        out_shape=[jax.ShapeDtypeStruct((m, CONV_DIM), BF16),
                   jax.ShapeDtypeStruct((n_i, CONV_W - 1, CONV_DIM), F32)],
        scratch_shapes=[pltpu.VMEM((n_j, CONV_W - 1, tc), F32)],
        compiler_params=_cp(2), name="conv_branch")(xn, wb, wc, wh, cw, g.reshape(1, CONV_DIM))


def _conv_step_kernel(x_ref, wb_ref, wc_ref, wh_ref, cw_ref, g_ref, p0_ref, p1_ref, o_ref, u_ref):
    x = x_ref[...]
    gate_b = _dot(x, wb_ref[...])
    u = _dot(x, wc_ref[...]) * _dot(x, wh_ref[...])
    cw = cw_ref[...]
    y = p0_ref[...] * cw[0:1, :] + p1_ref[...] * cw[1:2, :] + u * cw[2:3, :]
    _group_norm_store(gate_b * y, g_ref, o_ref)
    u_ref[...] = u


def _conv_step(xn, wb, wc, wh, cw, g, p0, p1, tc):
    m, d = xn.shape
    col = lambda j: (0, j)
    return pl.pallas_call(
        _conv_step_kernel, grid=(CONV_DIM // tc,),
        in_specs=[pl.BlockSpec((m, d), lambda j: (0, 0)),
                  pl.BlockSpec((d, tc), col), pl.BlockSpec((d, tc), col),
                  pl.BlockSpec((d, tc), col), pl.BlockSpec((CONV_W, tc), col),
                  pl.BlockSpec((1, tc), col), pl.BlockSpec((m, tc), col),
                  pl.BlockSpec((m, tc), col)],
        out_specs=[pl.BlockSpec((m, tc), col), pl.BlockSpec((m, tc), col)],
        out_shape=[jax.ShapeDtypeStruct((m, CONV_DIM), BF16),
                   jax.ShapeDtypeStruct((m, CONV_DIM), F32)],
        compiler_params=_cp(1), name="conv_step")(xn, wb, wc, wh, cw, g.reshape(1, CONV_DIM), p0, p1)


def _prompt_attn_kernel(q_ref, k_ref, v_ref, c_ref, g_ref, o_ref, m_scr, l_scr, acc_scr):
    qi, ki = pl.program_id(1), pl.program_id(2)
    tq, tk = q_ref.shape[1], k_ref.shape[1]

    @pl.when(ki == 0)
    def _():
        m_scr[...] = jnp.full(m_scr.shape, NEG_BIG, F32)
        l_scr[...] = jnp.zeros_like(l_scr)
        acc_scr[...] = jnp.zeros_like(acc_scr)

    def update(masked):
        if masked:
            key = lax.broadcasted_iota(jnp.int32, (tk, tq), 0)
            qry = lax.broadcasted_iota(jnp.int32, (tk, tq), 1)
            keep = key <= qry
        for h in range(ATT_HEADS):
            sl = slice(h * HEAD_DIM, (h + 1) * HEAD_DIM)
            q = q_ref[0, :, sl]
            k = k_ref[0, :, sl].astype(BF16)
            st = lax.dot_general(k, q, (((1,), (1,)), ((), ())), preferred_element_type=F32)
            st = st * ATT_SCALE - c_ref[0, :, h:h + 1]
            if masked:
                st = jnp.where(keep, st, NEG_BIG)
            m_prev = m_scr[h]
            m_new = jnp.maximum(m_prev, jnp.max(st, axis=0, keepdims=True))
            alpha = jnp.exp(m_prev - m_new)
            p = jnp.exp(st - m_new)
            l_scr[h] = alpha * l_scr[h] + jnp.sum(p, axis=0, keepdims=True)
            vt = v_ref[0, :, sl].T.astype(BF16)
            acc_scr[h] = alpha * acc_scr[h] + _dot(vt, p.astype(BF16))
            m_scr[h] = m_new

    @pl.when(ki < qi)
    def _():
        update(False)

    @pl.when(ki == qi)
    def _():
        update(True)
        for h in range(ATT_HEADS):
            sl = slice(h * HEAD_DIM, (h + 1) * HEAD_DIM)
            ot = acc_scr[h] / l_scr[h]
            ms = jnp.mean(ot * ot, axis=0, keepdims=True)
            o = (ot * lax.rsqrt(ms + RMS_EPS)).T
            o_ref[0, :, sl] = (o * g_ref[:, sl]).astype(o_ref.dtype)


def _prompt_attention(q, k, v, c, g, blk):
    n, s, _ = q.shape
    nb = s // blk
    kv_map = lambda b, qi, ki: (b, jnp.minimum(ki, qi), 0)
    return pl.pallas_call(
        _prompt_attn_kernel, grid=(n, nb, nb),
        in_specs=[pl.BlockSpec((1, blk, ATT_DIM), lambda b, qi, ki: (b, qi, 0)),
                  pl.BlockSpec((1, blk, ATT_DIM), kv_map),
                  pl.BlockSpec((1, blk, ATT_DIM), kv_map),
                  pl.BlockSpec((1, blk, LANES), kv_map),
                  pl.BlockSpec((1, ATT_DIM), lambda b, qi, ki: (0, 0))],
        out_specs=pl.BlockSpec((1, blk, ATT_DIM), lambda b, qi, ki: (b, qi, 0)),
        out_shape=jax.ShapeDtypeStruct((n, s, ATT_DIM), BF16),
        scratch_shapes=[pltpu.VMEM((ATT_HEADS, 1, blk), F32),
                        pltpu.VMEM((ATT_HEADS, 1, blk), F32),
                        pltpu.VMEM((ATT_HEADS, HEAD_DIM, blk), F32)],
        compiler_params=_cp(3), name="prompt_attention")(q, k, v, c, g.reshape(1, ATT_DIM))


def _decode_attn_kernel(pt_ref, q_ref, kn_ref, vn_ref, lfn_ref, g_ref, *rest):
    del pt_ref
    npg = PAGES_PER_STEP
    k_refs, v_refs, lf_refs = rest[:npg], rest[npg:2 * npg], rest[2 * npg:3 * npg]
    o_ref, m_scr, l_scr, acc_scr, carry_scr, lf_scr = rest[3 * npg:]
    c = pl.program_id(1)
    width = ATT_HEADS * HEAD_DIM

    @pl.when(c == 0)
    def _():
        m_scr[...] = jnp.full(m_scr.shape, NEG_BIG, F32)
        l_scr[...] = jnp.zeros_like(l_scr)
        acc_scr[...] = jnp.zeros_like(acc_scr)
        carry_scr[...] = lfn_ref[0]

    lane = lax.broadcasted_iota(jnp.int32, (npg, width), 1)
    for p in range(npg):
        lf_scr[p:p + 1, :] = lf_refs[p][0, 0]
    lf = lf_scr[...]
    y = lf
    sh = ATT_HEADS
    while sh < width:
        y = y + jnp.where(lane < width - sh, pltpu.roll(y, width - sh, axis=1), 0.0)
        sh *= 2
    excl = y - lf
    z = jnp.where(lane < ATT_HEADS, y, 0.0)
    sh = ATT_HEADS
    while sh < width:
        z = z + pltpu.roll(z, sh, axis=1)
        sh *= 2
    carry = carry_scr[...]
    bias = [None] * npg
    for p in reversed(range(npg)):
        bias[p] = carry + excl[p:p + 1, :]
        carry = carry + z[p:p + 1, :]
    carry_scr[...] = carry

    q = q_ref[0]
    hrow = lax.broadcasted_iota(jnp.int32, (ATT_HEADS, width), 0)
    hlane = lax.broadcasted_iota(jnp.int32, (ATT_HEADS, width), 1)
    own = jnp.bitwise_and(hlane, ATT_HEADS - 1) == hrow
    s_all = []
    for p in range(npg):
        kp = k_refs[p][0, 0].astype(BF16)
        s = lax.dot_general(q, kp, (((1,), (1,)), ((), ())), preferred_element_type=F32)
        s_all.append(jnp.where(own, s * ATT_SCALE + bias[p], NEG_BIG))
    m_prev = m_scr[...]
    m_new = m_prev
    for s in s_all:
        m_new = jnp.maximum(m_new, jnp.max(s, axis=1, keepdims=True))
    alpha = jnp.exp(m_prev - m_new)
    l_new = alpha * l_scr[...]
    acc = alpha * acc_scr[...]
    for p in range(npg):
        pr = jnp.exp(s_all[p] - m_new)
        l_new = l_new + jnp.sum(pr, axis=1, keepdims=True)
        acc = acc + _dot(pr.astype(BF16), v_refs[p][0, 0].astype(BF16))
    m_scr[...] = m_new
    l_scr[...] = l_new
    acc_scr[...] = acc

    @pl.when(c == pl.num_programs(1) - 1)
    def _():
        kn = kn_ref[0].astype(BF16).astype(F32)
        vn = vn_ref[0].astype(BF16).astype(F32)
        s_new = jnp.sum(q.astype(F32) * kn, axis=1, keepdims=True) * ATT_SCALE
        m_fin = jnp.maximum(m_new, s_new)
        a = jnp.exp(m_new - m_fin)
        pn = jnp.exp(s_new - m_fin)
        o = (a * acc + pn * vn) / (a * l_new + pn)
        ms = jnp.mean(o * o, axis=-1, keepdims=True)
        o_ref[0] = (o * lax.rsqrt(ms + RMS_EPS) * g_ref[...]).astype(o_ref.dtype)


def _decode_attention(layer, page_table, q, kn, vn, lfn, g, cache_k, cache_v, cache_logf):
    n, n_pages = page_table.shape
    depth, n_pool, page, heads, hd = cache_k.shape
    width = page * heads
    ck = cache_k.reshape(depth, n_pool, width, hd)
    cv = cache_v.reshape(depth, n_pool, width, hd)
    cl = cache_logf.reshape(depth, n_pool, 1, width)
    n_chunks = n_pages // PAGES_PER_STEP

    def page_map(p):
        return lambda b, c, pt: (layer, pt[b, (n_chunks - 1 - c) * PAGES_PER_STEP + p], 0, 0)

    per_seq = lambda b, c, pt: (b, 0, 0)
    in_specs = [pl.BlockSpec((1, heads, hd), per_seq), pl.BlockSpec((1, heads, hd), per_seq),
                pl.BlockSpec((1, heads, hd), per_seq), pl.BlockSpec((1, 1, width), per_seq),
                pl.BlockSpec((heads, hd), lambda b, c, pt: (0, 0))]
    in_specs += [pl.BlockSpec((1, 1, width, hd), page_map(p)) for p in range(PAGES_PER_STEP)]
    in_specs += [pl.BlockSpec((1, 1, width, hd), page_map(p)) for p in range(PAGES_PER_STEP)]
    in_specs += [pl.BlockSpec((1, 1, 1, width), page_map(p)) for p in range(PAGES_PER_STEP)]
    grid_spec = pltpu.PrefetchScalarGridSpec(
        num_scalar_prefetch=1, grid=(n, n_chunks), in_specs=in_specs,
        out_specs=pl.BlockSpec((1, heads, hd), per_seq),
        scratch_shapes=[pltpu.VMEM((heads, 1), F32), pltpu.VMEM((heads, 1), F32),
                        pltpu.VMEM((heads, hd), F32), pltpu.VMEM((1, width), F32),
                        pltpu.VMEM((PAGES_PER_STEP, width), F32)])
    return pl.pallas_call(
        _decode_attn_kernel, grid_spec=grid_spec,
        out_shape=jax.ShapeDtypeStruct((n, heads, hd), BF16),
        compiler_params=_cp(2), name="decode_attention")(
            page_table, q, kn, vn, lfn, g.reshape(heads, hd),
            *([ck] * PAGES_PER_STEP), *([cv] * PAGES_PER_STEP), *([cl] * PAGES_PER_STEP))


def _pad_lanes(w):
    return jnp.pad(w, ((0, 0), (0, LANES - w.shape[1])))


def kernel(x_prompt, x_sample, cache_k, cache_v, cache_logf, state_conv, page_table, norm_mix_g,
           w_in, b_forget, conv_w, g_att_out, g_conv_out, w_out, norm_ffn_g, dense_w_gate,
           dense_w_up, dense_w_down, moe_router, moe_w_gate, moe_w_up, moe_w_down, final_norm_g):
    n, seq, d = x_prompt.shape
    n_dec, dec_seq, _ = x_sample.shape
    assert dec_seq == 1 and d == D_MODEL
    depth = w_in.shape[0]
    mp = n * seq
    tm_p, tm_s = 512, n_dec

    xp = x_prompt.reshape(mp, d)
    xs = x_sample.reshape(n_dec, d)
    xnp = _rmsnorm(xp, norm_mix_g[0], tm_p)
    xns = _rmsnorm(xs, norm_mix_g[0], tm_s)

    outs = {key: [] for key in ("kp", "vp", "fp", "cp", "ks", "vs", "fs", "cs")}
    yp = ys = None
    for l in range(depth):
        wl = w_in[l]
        o = 0
        wq = wl[:, o:o + ATT_DIM].astype(BF16); o += ATT_DIM
        wk = wl[:, o:o + ATT_DIM].astype(BF16); o += ATT_DIM
        wv = wl[:, o:o + ATT_DIM].astype(BF16); o += ATT_DIM
        wf = _pad_lanes(wl[:, o:o + ATT_HEADS]).astype(BF16); o += ATT_HEADS
        wb = wl[:, o:o + CONV_DIM].astype(BF16); o += CONV_DIM
        wc = wl[:, o:o + CONV_DIM].astype(BF16); o += CONV_DIM
        wh = wl[:, o:o + CONV_DIM].astype(BF16)
        bf = _pad_lanes(b_forget[l].reshape(1, ATT_HEADS))
        wo_a = w_out[l][:ATT_DIM].astype(BF16)
        wo_c = w_out[l][ATT_DIM:].astype(BF16)

        q_p = _mm(xnp, wq, BF16, 1024, 512, "q_proj")
        k_p = _mm(xnp, wk, F32, 1024, 512, "k_proj")
        v_p = _mm(xnp, wv, F32, 1024, 512, "v_proj")
        lf_p, c_p = _forget(xnp, wf, bf, seq, True)
        conv_p, tails = _conv_branch(xnp, wb, wc, wh, conv_w[l], g_conv_out[l], seq, 512, 512)
        att_p = _prompt_attention(q_p.reshape(n, seq, ATT_DIM), k_p.reshape(n, seq, ATT_DIM),
                                  v_p.reshape(n, seq, ATT_DIM), c_p.reshape(n, seq, LANES),
                                  g_att_out[l], 512)
        xp, hnp = _outproj(att_p.reshape(mp, ATT_DIM), conv_p, wo_a, wo_c, xp, norm_ffn_g[l], tm_p)
        outs["kp"].append(k_p.reshape(n, seq, ATT_HEADS, HEAD_DIM))
        outs["vp"].append(v_p.reshape(n, seq, ATT_HEADS, HEAD_DIM))
        outs["fp"].append(lf_p[:, :ATT_HEADS].reshape(n, seq, ATT_HEADS))
        tiles_per_seq = seq // 512
        outs["cp"].append(tails[tiles_per_seq - 1::tiles_per_seq])

        q_s = _mm(xns, wq, BF16, tm_s, 512, "q_proj_s")
        k_s = _mm(xns, wk, F32, tm_s, 512, "k_proj_s")
        v_s = _mm(xns, wv, F32, tm_s, 512, "v_proj_s")
        lf_s = _forget(xns, wf, bf, n_dec, False)[0]
        st = state_conv[l]
        conv_s, u_s = _conv_step(xns, wb, wc, wh, conv_w[l], g_conv_out[l], st[:, 0], st[:, 1], 512)
        lfn = jnp.tile(lf_s[:, :ATT_HEADS], (1, cache_k.shape[2])).reshape(n_dec, 1, -1)
        att_s = _decode_attention(l, page_table, q_s.reshape(n_dec, ATT_HEADS, HEAD_DIM),
                                  k_s.reshape(n_dec, ATT_HEADS, HEAD_DIM),
                                  v_s.reshape(n_dec, ATT_HEADS, HEAD_DIM), lfn, g_att_out[l],
                                  cache_k, cache_v, cache_logf)
        xs, hns = _outproj(att_s.reshape(n_dec, ATT_DIM), conv_s, wo_a, wo_c, xs, norm_ffn_g[l], tm_s)
        outs["ks"].append(k_s.reshape(n_dec, 1, ATT_HEADS, HEAD_DIM))
        outs["vs"].append(v_s.reshape(n_dec, 1, ATT_HEADS, HEAD_DIM))
        outs["fs"].append(lf_s[:, :ATT_HEADS].reshape(n_dec, 1, ATT_HEADS))
        outs["cs"].append(jnp.stack([st[:, 1], u_s], axis=1))

        last = l == depth - 1
        next_g = final_norm_g if last else norm_mix_g[l + 1]
        next_dtype = F32 if last else BF16
        i = l // 2
        if l % 2 == 0:
            wg = dense_w_gate[i].astype(BF16)
            wu = dense_w_up[i].astype(BF16)
            wd = dense_w_down[i].astype(BF16)
            hp = _swiglu(hnp, wg, wu, 1024, 512)
            xp, nxp = _down(hp, wd, xp, next_g, tm_p, 1408, not last, next_dtype)
            hs = _swiglu(hns, wg, wu, tm_s, 512)
            xs, nxs = _down(hs, wd, xs, next_g, tm_s, 1408, not last, next_dtype)
        else:
            xp, nxp, xs, nxs = _moe_ffn(
                xp, hnp, xs, hns, norm_ffn_g[l], moe_router[i], moe_w_gate[i].astype(BF16),
                moe_w_up[i].astype(BF16), moe_w_down[i].astype(BF16), next_g, not last, next_dtype)
        if last:
            yp, ys = nxp, nxs
        else:
            xnp, xns = nxp, nxs

    return (yp.reshape(n, seq, d), ys.reshape(n_dec, 1, d),
            jnp.stack(outs["kp"]), jnp.stack(outs["vp"]), jnp.stack(outs["fp"]), jnp.stack(outs["cp"]),
            jnp.stack(outs["ks"]), jnp.stack(outs["vs"]), jnp.stack(outs["fs"]), jnp.stack(outs["cs"]))
```

```python
import functools

import jax
import jax.numpy as jnp
from jax import lax
from jax.experimental import pallas as pl
from jax.experimental.pallas import tpu as pltpu

D_MODEL = 2048
HEAD_DIM = 128
ATT_HEADS = 8
ATT_DIM = ATT_HEADS * HEAD_DIM
CONV_DIM = D_MODEL - ATT_DIM
CONV_W = 3
N_EXPERTS = 8
TOP_K = 2
ATT_SCALE = HEAD_DIM ** -0.5
RMS_EPS = 1e-6
LANES = 128
NEG_BIG = -1e30
PAGES_PER_STEP = 8
VMEM_LIMIT = 56 * 1024 * 1024

F32 = jnp.float32
BF16 = jnp.bfloat16


def _cp(n_axes):
    return pltpu.CompilerParams(dimension_semantics=("arbitrary",) * n_axes,
                                vmem_limit_bytes=VMEM_LIMIT)


def _rms(x, g):
    return x * lax.rsqrt(jnp.mean(x * x, axis=-1, keepdims=True) + RMS_EPS) * g


def _dot(a, b):
    return jnp.dot(a, b, preferred_element_type=F32)


def _rmsnorm_kernel(x_ref, g_ref, o_ref):
    o_ref[...] = _rms(x_ref[...], g_ref[...]).astype(o_ref.dtype)


def _rmsnorm(x, g, tm, out_dtype=BF16):
    m, d = x.shape
    return pl.pallas_call(
        _rmsnorm_kernel, grid=(m // tm,),
        in_specs=[pl.BlockSpec((tm, d), lambda i: (i, 0)),
                  pl.BlockSpec((1, d), lambda i: (0, 0))],
        out_specs=pl.BlockSpec((tm, d), lambda i: (i, 0)),
        out_shape=jax.ShapeDtypeStruct((m, d), out_dtype),
        compiler_params=_cp(1), name="rmsnorm")(x, g.reshape(1, d))


def _mm_kernel(x_ref, w_ref, *rest):
    o_ref = rest[-1]
    o_ref[...] = _dot(x_ref[...], w_ref[...]).astype(o_ref.dtype).reshape(o_ref.shape)


def _mm_layer(x, w, stacked, layer, depth, tm, tn, name):
    m, k = x.shape
    n = w.shape[1]
    in_specs = [pl.BlockSpec((tm, k), lambda i, j: (i, 0)),
                pl.BlockSpec((k, tn), lambda i, j: (0, j))]
    args, aliases = [x, w], {}
    if stacked is not None:
        in_specs.append(pl.BlockSpec(memory_space=pl.ANY))
        args.append(stacked)
        aliases = {2: 0}
    return pl.pallas_call(
        _mm_kernel, grid=(m // tm, n // tn), in_specs=in_specs,
        out_specs=pl.BlockSpec((1, tm, tn), lambda i, j: (layer, i, j)),
        out_shape=jax.ShapeDtypeStruct((depth, m, n), F32), input_output_aliases=aliases,
        compiler_params=_cp(2), name=name)(*args)


def _mm(x, w, out_dtype, tm, tn, name):
    m, k = x.shape
    n = w.shape[1]
    return pl.pallas_call(
        _mm_kernel, grid=(m // tm, n // tn),
        in_specs=[pl.BlockSpec((tm, k), lambda i, j: (i, 0)),
                  pl.BlockSpec((k, tn), lambda i, j: (0, j))],
        out_specs=pl.BlockSpec((tm, tn), lambda i, j: (i, j)),
        out_shape=jax.ShapeDtypeStruct((m, n), out_dtype),
        compiler_params=_cp(2), name=name)(x, w)


def _swiglu_kernel(x_ref, wg_ref, wu_ref, o_ref):
    x = x_ref[...]
    a = _dot(x, wg_ref[...])
    b = _dot(x, wu_ref[...])
    o_ref[...] = (a * (1.0 / (1.0 + jnp.exp(-a))) * b).astype(o_ref.dtype)


def _swiglu(x, wg, wu, tm, tn):
    m, k = x.shape
    n = wg.shape[1]
    return pl.pallas_call(
        _swiglu_kernel, grid=(m // tm, n // tn),
        in_specs=[pl.BlockSpec((tm, k), lambda i, j: (i, 0)),
                  pl.BlockSpec((k, tn), lambda i, j: (0, j)),
                  pl.BlockSpec((k, tn), lambda i, j: (0, j))],
        out_specs=pl.BlockSpec((tm, tn), lambda i, j: (i, j)),
        out_shape=jax.ShapeDtypeStruct((m, n), BF16),
        compiler_params=_cp(2), name="swiglu_up")(x, wg, wu)


def _down_kernel(h_ref, w_ref, res_ref, g_ref, *rest, emit_x):
    xn_ref, acc_scr = rest[-2:]
    k = pl.program_id(1)

    @pl.when(k == 0)
    def _():
        acc_scr[...] = jnp.zeros_like(acc_scr)

    acc_scr[...] += _dot(h_ref[...], w_ref[...])

    @pl.when(k == pl.num_programs(1) - 1)
    def _():
        xo = res_ref[...] + acc_scr[...]
        if emit_x:
            rest[0][...] = xo
        xn_ref[...] = _rms(xo, g_ref[...]).astype(xn_ref.dtype)


def _down(h, w, res, norm_g, tm, tk, emit_x, norm_dtype):
    m, kdim = h.shape
    d = w.shape[1]
    row = lambda i, k: (i, 0)
    out_specs = [pl.BlockSpec((tm, d), row)]
    out_shape = [jax.ShapeDtypeStruct((m, d), norm_dtype)]
    if emit_x:
        out_specs.insert(0, pl.BlockSpec((tm, d), row))
        out_shape.insert(0, jax.ShapeDtypeStruct((m, d), F32))
    outs = pl.pallas_call(
        functools.partial(_down_kernel, emit_x=emit_x), grid=(m // tm, kdim // tk),
        in_specs=[pl.BlockSpec((tm, tk), lambda i, k: (i, k)),
                  pl.BlockSpec((tk, d), lambda i, k: (k, 0)),
                  pl.BlockSpec((tm, d), row), pl.BlockSpec((1, d), lambda i, k: (0, 0))],
        out_specs=out_specs, out_shape=out_shape, scratch_shapes=[pltpu.VMEM((tm, d), F32)],
        compiler_params=_cp(2), name="ffn_down")(h, w, res, norm_g.reshape(1, d))
    return (outs[0], outs[1]) if emit_x else (None, outs[0])


def _outproj_kernel(a_ref, c_ref, wa_ref, wc_ref, res_ref, g_ref, xo_ref, xn_ref):
    xo = res_ref[...] + (_dot(a_ref[...], wa_ref[...]) + _dot(c_ref[...], wc_ref[...]))
    xo_ref[...] = xo
    xn_ref[...] = _rms(xo, g_ref[...]).astype(xn_ref.dtype)


def _outproj(a, c, wa, wc, res, g, tm):
    m, d = res.shape
    ka, kc = a.shape[1], c.shape[1]
    row = lambda i: (i, 0)
    fixed = lambda i: (0, 0)
    return pl.pallas_call(
        _outproj_kernel, grid=(m // tm,),
        in_specs=[pl.BlockSpec((tm, ka), row), pl.BlockSpec((tm, kc), row),
                  pl.BlockSpec((ka, d), fixed), pl.BlockSpec((kc, d), fixed),
                  pl.BlockSpec((tm, d), row), pl.BlockSpec((1, d), fixed)],
        out_specs=[pl.BlockSpec((tm, d), row), pl.BlockSpec((tm, d), row)],
        out_shape=[jax.ShapeDtypeStruct((m, d), F32), jax.ShapeDtypeStruct((m, d), BF16)],
        compiler_params=_cp(1), name="out_proj")(a, c, wa, wc, res, g.reshape(1, d))


SEL_E1, SEL_E2, SEL_G1, SEL_G2, SEL_R1, SEL_R2 = range(6)


def _router_kernel(x_ref, r_ref, cin_ref, sel_ref, cnt_ref, carry_scr, *, n_valid):
    i = pl.program_id(0)
    tm = x_ref.shape[0]

    @pl.when(i == 0)
    def _():
        carry_scr[...] = cin_ref[...]

    logits = _dot(x_ref[...], r_ref[...])
    lane = lax.broadcasted_iota(jnp.int32, logits.shape, 1).astype(F32)
    lg = jnp.where(lane < N_EXPERTS, logits, -jnp.inf)
    m1 = jnp.max(lg, axis=1, keepdims=True)
    i1 = jnp.min(jnp.where(lg == m1, lane, float(LANES)), axis=1, keepdims=True)
    lg2 = jnp.where(lane == i1, -jnp.inf, lg)
    m2 = jnp.max(lg2, axis=1, keepdims=True)
    i2 = jnp.min(jnp.where(lg2 == m2, lane, float(LANES)), axis=1, keepdims=True)
    e2 = jnp.exp(m2 - m1)
    den = 1.0 + e2
    row = lax.broadcasted_iota(jnp.int32, logits.shape, 0) + i * tm
    hit = jnp.where(((lane == i1) | (lane == i2)) & (row < n_valid), 1.0, 0.0)
    r = lax.broadcasted_iota(jnp.int32, (tm, tm), 0)
    c = lax.broadcasted_iota(jnp.int32, (tm, tm), 1)
    before = (c < r).astype(BF16)
    rank = _dot(before, hit.astype(BF16)) + carry_scr[...]
    carry_scr[...] = carry_scr[...] + jnp.sum(hit, axis=0, keepdims=True)
    cnt_ref[...] = carry_scr[...]
    rank1 = jnp.sum(jnp.where(lane == i1, rank, 0.0), axis=1, keepdims=True)
    rank2 = jnp.sum(jnp.where(lane == i2, rank, 0.0), axis=1, keepdims=True)
    sel = jnp.zeros_like(logits)
    for idx, val in ((SEL_E1, i1), (SEL_E2, i2), (SEL_G1, 1.0 / den), (SEL_G2, e2 / den),
                     (SEL_R1, rank1), (SEL_R2, rank2)):
        sel = jnp.where(lane == idx, val, sel)
    sel_ref[...] = sel


def _router(hn, r_pad, count_in, tm, n_valid):
    m, d = hn.shape
    fixed = lambda i: (0, 0)
    return pl.pallas_call(
        functools.partial(_router_kernel, n_valid=n_valid), grid=(m // tm,),
        in_specs=[pl.BlockSpec((tm, d), lambda i: (i, 0)), pl.BlockSpec((d, LANES), fixed),
                  pl.BlockSpec((1, LANES), fixed)],
        out_specs=[pl.BlockSpec((tm, LANES), lambda i: (i, 0)), pl.BlockSpec((1, LANES), fixed)],
        out_shape=[jax.ShapeDtypeStruct((m, LANES), F32), jax.ShapeDtypeStruct((1, LANES), F32)],
        scratch_shapes=[pltpu.VMEM((1, LANES), F32)],
        compiler_params=_cp(1), name="router")(hn, r_pad, count_in)


def _dispatch_kernel(d1_ref, d2_ref, ps_ref, pc_ref, x_ref, *rest, first):
    xs_ref, zero_scr, sem = rest[-3:]
    tt = x_ref.shape[0]
    base = pl.program_id(0) * tt

    def row_copy(src_row, dst_row):
        return pltpu.make_async_copy(src_row, xs_ref.at[pl.ds(dst_row, 1), :], sem)

    def issue(r, carry):
        src = x_ref.at[pl.ds(r, 1), :]
        row_copy(src, d1_ref[base + r]).start()
        row_copy(src, d2_ref[base + r]).start(priority=1)
        return carry

    def drain(r, carry):
        row_copy(x_ref.at[pl.ds(0, 1), :], 0).wait()
        row_copy(x_ref.at[pl.ds(0, 1), :], 0).wait()
        return carry

    lax.fori_loop(0, tt, issue, 0, unroll=8)
    lax.fori_loop(0, tt, drain, 0, unroll=8)

    if first:
        @pl.when(pl.program_id(0) == 0)
        def _():
            zero_scr[...] = jnp.zeros_like(zero_scr)
            zrow = zero_scr.at[pl.ds(0, 1), :]
            for e in range(N_EXPERTS):
                def fill(r, carry, e=e):
                    row_copy(zrow, ps_ref[e] + r).start()
                    return carry

                def fill_done(r, carry):
                    row_copy(zrow, 0).wait()
                    return carry

                lax.fori_loop(0, pc_ref[e], fill, 0)
                lax.fori_loop(0, pc_ref[e], fill_done, 0)


def _dispatch(d1, d2, pad_start, pad_count, x, xs, tt, n_rows):
    m, d = x.shape
    first = xs is None
    in_specs = [pl.BlockSpec((tt, d), lambda i, *_: (i, 0))]
    args = [d1, d2, pad_start, pad_count, x]
    aliases = {}
    if not first:
        in_specs.append(pl.BlockSpec(memory_space=pl.ANY))
        args.append(xs)
        aliases = {5: 0}
    grid_spec = pltpu.PrefetchScalarGridSpec(
        num_scalar_prefetch=4, grid=(m // tt,), in_specs=in_specs,
        out_specs=pl.BlockSpec(memory_space=pl.ANY),
        scratch_shapes=[pltpu.VMEM((8, d), F32), pltpu.SemaphoreType.DMA(())])
    return pl.pallas_call(
        functools.partial(_dispatch_kernel, first=first), grid_spec=grid_spec,
        out_shape=jax.ShapeDtypeStruct((n_rows, d), F32), input_output_aliases=aliases,
        compiler_params=_cp(1), name="moe_dispatch")(*args)


def _moe_up_kernel(te_ref, na_ref, x_ref, g_ref, wg_ref, wu_ref, o_ref):
    del te_ref

    @pl.when(pl.program_id(1) < na_ref[0])
    def _():
        xn = _rms(x_ref[...], g_ref[...]).astype(BF16)
        a = _dot(xn, wg_ref[0])
        b = _dot(xn, wu_ref[0])
        o_ref[...] = (a * (1.0 / (1.0 + jnp.exp(-a))) * b).astype(o_ref.dtype)


def _moe_up(tile_expert, n_active, xs, g, wg, wu, tm, tn):
    r, d = xs.shape
    f = wg.shape[2]
    tile = lambda t, na: jnp.minimum(t, na[0] - 1)
    w_map = lambda j, t, te, na: (te[tile(t, na)], 0, j)
    grid_spec = pltpu.PrefetchScalarGridSpec(
        num_scalar_prefetch=2, grid=(f // tn, r // tm),
        in_specs=[pl.BlockSpec((tm, d), lambda j, t, te, na: (tile(t, na), 0)),
                  pl.BlockSpec((1, d), lambda j, t, te, na: (0, 0)),
                  pl.BlockSpec((1, d, tn), w_map), pl.BlockSpec((1, d, tn), w_map)],
        out_specs=pl.BlockSpec((tm, tn), lambda j, t, te, na: (tile(t, na), j)))
    return pl.pallas_call(
        _moe_up_kernel, grid_spec=grid_spec, out_shape=jax.ShapeDtypeStruct((r, f), BF16),
        compiler_params=_cp(2), name="moe_up")(tile_expert, n_active, xs, g.reshape(1, d), wg, wu)


def _moe_down_kernel(te_ref, na_ref, h_ref, w_ref, o_ref):
    del te_ref

    @pl.when(pl.program_id(0) < na_ref[0])
    def _():
        o_ref[...] = _dot(h_ref[...], w_ref[0])


def _moe_down(tile_expert, n_active, h, wd, tm):
    r, f = h.shape
    d = wd.shape[2]
    tile = lambda t, na: jnp.minimum(t, na[0] - 1)
    grid_spec = pltpu.PrefetchScalarGridSpec(
        num_scalar_prefetch=2, grid=(r // tm,),
        in_specs=[pl.BlockSpec((tm, f), lambda t, te, na: (tile(t, na), 0)),
                  pl.BlockSpec((1, f, d), lambda t, te, na: (te[tile(t, na)], 0, 0))],
        out_specs=pl.BlockSpec((tm, d), lambda t, te, na: (tile(t, na), 0)))
    return pl.pallas_call(
        _moe_down_kernel, grid_spec=grid_spec, out_shape=jax.ShapeDtypeStruct((r, d), F32),
        compiler_params=_cp(1), name="moe_down")(tile_expert, n_active, h, wd)


def _combine_kernel(d1_ref, d2_ref, ys_ref, res_ref, sel_ref, g_ref, *rest, emit_x):
    y1_scr, y2_scr, sem = rest[-3:]
    outs = rest[:-3]
    tt = res_ref.shape[0]
    base = pl.program_id(0) * tt

    def row_copy(src_row, dst):
        return pltpu.make_async_copy(ys_ref.at[pl.ds(src_row, 1), :], dst, sem)

    def issue(r, carry):
        row_copy(d1_ref[base + r], y1_scr.at[pl.ds(r, 1), :]).start()
        row_copy(d2_ref[base + r], y2_scr.at[pl.ds(r, 1), :]).start(priority=1)
        return carry

    def drain(r, carry):
        row_copy(0, y1_scr.at[pl.ds(0, 1), :]).wait()
        row_copy(0, y2_scr.at[pl.ds(0, 1), :]).wait()
        return carry

    lax.fori_loop(0, tt, issue, 0, unroll=8)
    lax.fori_loop(0, tt, drain, 0, unroll=8)
    sel = sel_ref[...]
    g1 = sel[:, SEL_G1:SEL_G1 + 1]
    g2 = sel[:, SEL_G2:SEL_G2 + 1]
    xo = res_ref[...] + (g1 * y1_scr[...] + g2 * y2_scr[...])
    if emit_x:
        outs[0][...] = xo
    outs[-1][...] = _rms(xo, g_ref[...]).astype(outs[-1].dtype)


def _combine(d1, d2, ys, res, sel, g, tt, emit_x, norm_dtype):
    m, d = res.shape
    row = lambda i, *_: (i, 0)
    out_specs = [pl.BlockSpec((tt, d), row)]
    out_shape = [jax.ShapeDtypeStruct((m, d), norm_dtype)]
    if emit_x:
        out_specs.insert(0, pl.BlockSpec((tt, d), row))
        out_shape.insert(0, jax.ShapeDtypeStruct((m, d), F32))
    grid_spec = pltpu.PrefetchScalarGridSpec(
        num_scalar_prefetch=2, grid=(m // tt,),
        in_specs=[pl.BlockSpec(memory_space=pl.ANY), pl.BlockSpec((tt, d), row),
                  pl.BlockSpec((tt, LANES), row), pl.BlockSpec((1, d), lambda i, *_: (0, 0))],
        out_specs=out_specs,
        scratch_shapes=[pltpu.VMEM((tt, d), F32), pltpu.VMEM((tt, d), F32),
                        pltpu.SemaphoreType.DMA(())])
    outs = pl.pallas_call(
        functools.partial(_combine_kernel, emit_x=emit_x), grid_spec=grid_spec,
        out_shape=out_shape, compiler_params=_cp(1), name="moe_combine")(
            d1, d2, ys, res, sel, g.reshape(1, d))
    return (outs[0], outs[1]) if emit_x else (None, outs[0])


def _moe_ffn(x_p, hn_p, x_s, hn_s, ffn_g, router, wg, wu, wd, next_g, emit_x, norm_dtype,
             tm=512, tn=1408, t_route=512, t_move=512, t_comb=256):
    mp, d = x_p.shape
    ms = x_s.shape[0]
    n_tiles = (TOP_K * (mp + ms)) // tm + N_EXPERTS
    r_pad = _pad_lanes(router).astype(BF16)
    ms_pad = -(-ms // LANES) * LANES
    hn_s_pad = jnp.pad(hn_s, ((0, ms_pad - ms), (0, 0)))
    sel_p, cnt_p = _router(hn_p, r_pad, jnp.zeros((1, LANES), F32), t_route, mp)
    sel_s, cnt = _router(hn_s_pad, r_pad, cnt_p, ms_pad, ms)

    count = cnt[0, :N_EXPERTS].astype(jnp.int32)
    padded = (count + tm - 1) // tm * tm
    g_end = jnp.cumsum(padded)
    g_start = g_end - padded
    tile_expert = jnp.minimum(jnp.searchsorted(g_end // tm, jnp.arange(n_tiles), side="right"),
                              N_EXPERTS - 1).astype(jnp.int32)
    n_active = (g_end[-1:] // tm).astype(jnp.int32)

    def dests(sel, rows):
        e1, e2 = sel[:rows, SEL_E1].astype(jnp.int32), sel[:rows, SEL_E2].astype(jnp.int32)
        return (g_start[e1] + sel[:rows, SEL_R1].astype(jnp.int32),
                g_start[e2] + sel[:rows, SEL_R2].astype(jnp.int32))

    d1_p, d2_p = dests(sel_p, mp)
    d1_s, d2_s = dests(sel_s, ms)
    pad_start = g_start + count
    pad_count = padded - count
    xs = _dispatch(d1_p, d2_p, pad_start, pad_count, x_p, None, t_move, n_tiles * tm)
    xs = _dispatch(d1_s, d2_s, pad_start, pad_count, x_s, xs, ms, n_tiles * tm)
    h = _moe_up(tile_expert, n_active, xs, ffn_g, wg, wu, tm, tn)
    ys = _moe_down(tile_expert, n_active, h, wd, tm)
    xo_p, xn_p = _combine(d1_p, d2_p, ys, x_p, sel_p, next_g, t_comb, emit_x, norm_dtype)
    xo_s, xn_s = _combine(d1_s, d2_s, ys, x_s, sel_s, next_g, ms, emit_x, norm_dtype)
    return xo_p, xn_p, xo_s, xn_s


def _forget_kernel(x_ref, w_ref, b_ref, lf_ref, *c_ref, cumsum):
    z = _dot(x_ref[...], w_ref[...]) + b_ref[...]
    lf = jnp.minimum(z, 0.0) - jnp.log1p(jnp.exp(-jnp.abs(z)))
    lf_ref[...] = lf
    if cumsum:
        r = lax.broadcasted_iota(jnp.int32, (LANES, LANES), 0)
        c = lax.broadcasted_iota(jnp.int32, (LANES, LANES), 1)
        tri = (c <= r).astype(F32)
        carry = jnp.zeros((1, LANES), F32)
        for blk in range(lf.shape[0] // LANES):
            rows = slice(blk * LANES, (blk + 1) * LANES)
            cb = jnp.dot(tri, lf[rows, :], preferred_element_type=F32,
                         precision=lax.Precision.HIGHEST) + carry
            carry = cb[LANES - 1:LANES, :]
            c_ref[0][rows, :] = cb


def _forget(xn, wf_pad, bf_pad, rows, cumsum):
    m, d = xn.shape
    n = m // rows
    out_specs = [pl.BlockSpec((rows, LANES), lambda b: (b, 0))]
    out_shape = [jax.ShapeDtypeStruct((m, LANES), F32)]
    if cumsum:
        out_specs.append(pl.BlockSpec((rows, LANES), lambda b: (b, 0)))
        out_shape.append(jax.ShapeDtypeStruct((m, LANES), F32))
    return pl.pallas_call(
        functools.partial(_forget_kernel, cumsum=cumsum), grid=(n,),
        in_specs=[pl.BlockSpec((rows, d), lambda b: (b, 0)),
                  pl.BlockSpec((d, LANES), lambda b: (0, 0)),
                  pl.BlockSpec((1, LANES), lambda b: (0, 0))],
        out_specs=out_specs, out_shape=out_shape,
        compiler_params=_cp(1), name="forget_gate")(xn, wf_pad, bf_pad)


def _group_norm_store(yc, g_ref, o_ref):
    for gi in range(yc.shape[1] // HEAD_DIM):
        sl = slice(gi * HEAD_DIM, (gi + 1) * HEAD_DIM)
        yg = yc[:, sl]
        ms = jnp.mean(yg * yg, axis=-1, keepdims=True)
        o_ref[:, sl] = (yg * lax.rsqrt(ms + RMS_EPS) * g_ref[:, sl]).astype(o_ref.dtype)


def _conv_kernel(x_ref, wb_ref, wc_ref, wh_ref, cw_ref, g_ref, o_ref, tail_ref, halo_scr,
                 *, tiles_per_seq):
    i, j = pl.program_id(0), pl.program_id(1)
    x = x_ref[...]
    gate_b = _dot(x, wb_ref[...])
    u = _dot(x, wc_ref[...]) * _dot(x, wh_ref[...])
    tm = u.shape[0]

    @pl.when(i % tiles_per_seq == 0)
    def _():
        halo_scr[j] = jnp.zeros(halo_scr.shape[1:], F32)

    prev = halo_scr[j]
    row = lax.broadcasted_iota(jnp.int32, u.shape, 0)
    u1 = jnp.where(row == 0, prev[1:2, :], pltpu.roll(u, 1, axis=0))
    u2 = jnp.where(row == 0, prev[0:1, :],
                   jnp.where(row == 1, prev[1:2, :], pltpu.roll(u, 2, axis=0)))
    cw = cw_ref[...]
    y = u2 * cw[0:1, :] + u1 * cw[1:2, :] + u * cw[2:3, :]
    _group_norm_store(gate_b * y, g_ref, o_ref)
    tail = u[tm - (CONV_W - 1):tm, :]
    halo_scr[j] = tail
    tail_ref[0] = tail


def _conv_branch(xn, wb, wc, wh, cw, g, seq, tm, tc):
    m, d = xn.shape
    n_i, n_j = m // tm, CONV_DIM // tc
    col = lambda i, j: (0, j)
    return pl.pallas_call(
        functools.partial(_conv_kernel, tiles_per_seq=seq // tm), grid=(n_i, n_j),
        in_specs=[pl.BlockSpec((tm, d), lambda i, j: (i, 0)),
                  pl.BlockSpec((d, tc), col), pl.BlockSpec((d, tc), col),
                  pl.BlockSpec((d, tc), col), pl.BlockSpec((CONV_W, tc), col),
                  pl.BlockSpec((1, tc), col)],
        out_specs=[pl.BlockSpec((tm, tc), lambda i, j: (i, j)),
                   pl.BlockSpec((1, CONV_W - 1, tc), lambda i, j: (i, 0, j))],
        out_shape=[jax.ShapeDtypeStruct((m, CONV_DIM), BF16),
                   jax.ShapeDtypeStruct((n_i, CONV_W - 1, CONV_DIM), F32)],
        scratch_shapes=[pltpu.VMEM((n_j, CONV_W - 1, tc), F32)],
        compiler_params=_cp(2), name="conv_branch")(xn, wb, wc, wh, cw, g.reshape(1, CONV_DIM))


def _conv_step_kernel(x_ref, wb_ref, wc_ref, wh_ref, cw_ref, g_ref, p0_ref, p1_ref, o_ref, u_ref):
    x = x_ref[...]
    gate_b = _dot(x, wb_ref[...])
    u = _dot(x, wc_ref[...]) * _dot(x, wh_ref[...])
    cw = cw_ref[...]
    y = p0_ref[...] * cw[0:1, :] + p1_ref[...] * cw[1:2, :] + u * cw[2:3, :]
    _group_norm_store(gate_b * y, g_ref, o_ref)
    u_ref[...] = u


def _conv_step(xn, wb, wc, wh, cw, g, p0, p1, tc):
    m, d = xn.shape
    col = lambda j: (0, j)
    return pl.pallas_call(
        _conv_step_kernel, grid=(CONV_DIM // tc,),
        in_specs=[pl.BlockSpec((m, d), lambda j: (0, 0)),
                  pl.BlockSpec((d, tc), col), pl.BlockSpec((d, tc), col),
                  pl.BlockSpec((d, tc), col), pl.BlockSpec((CONV_W, tc), col),
                  pl.BlockSpec((1, tc), col), pl.BlockSpec((m, tc), col),
                  pl.BlockSpec((m, tc), col)],
        out_specs=[pl.BlockSpec((m, tc), col), pl.BlockSpec((m, tc), col)],
        out_shape=[jax.ShapeDtypeStruct((m, CONV_DIM), BF16),
                   jax.ShapeDtypeStruct((m, CONV_DIM), F32)],
        compiler_params=_cp(1), name="conv_step")(xn, wb, wc, wh, cw, g.reshape(1, CONV_DIM), p0, p1)


def _prompt_attn_kernel(q_ref, k_ref, v_ref, c_ref, g_ref, o_ref, m_scr, l_scr, acc_scr):
    qi, ki = pl.program_id(1), pl.program_id(2)
    tq, tk = q_ref.shape[1], k_ref.shape[2]

    @pl.when(ki == 0)
    def _():
        m_scr[...] = jnp.full(m_scr.shape, NEG_BIG, F32)
        l_scr[...] = jnp.zeros_like(l_scr)
        acc_scr[...] = jnp.zeros_like(acc_scr)

    def update(masked):
        if masked:
            key = lax.broadcasted_iota(jnp.int32, (tk, tq), 0)
            qry = lax.broadcasted_iota(jnp.int32, (tk, tq), 1)
            keep = key <= qry
        for h in range(ATT_HEADS):
            sl = slice(h * HEAD_DIM, (h + 1) * HEAD_DIM)
            q = q_ref[0, :, sl]
            k = k_ref[0, 0, :, sl].astype(BF16)
            st = lax.dot_general(k, q, (((1,), (1,)), ((), ())), preferred_element_type=F32)
            st = st * ATT_SCALE - c_ref[0, :, h:h + 1]
            if masked:
                st = jnp.where(keep, st, NEG_BIG)
            m_prev = m_scr[h]
            m_new = jnp.maximum(m_prev, jnp.max(st, axis=0, keepdims=True))
            alpha = jnp.exp(m_prev - m_new)
            p = jnp.exp(st - m_new)
            l_scr[h] = alpha * l_scr[h] + jnp.sum(p, axis=0, keepdims=True)
            vt = v_ref[0, 0, :, sl].T.astype(BF16)
            acc_scr[h] = alpha * acc_scr[h] + _dot(vt, p.astype(BF16))
            m_scr[h] = m_new

    @pl.when(ki < qi)
    def _():
        update(False)

    @pl.when(ki == qi)
    def _():
        update(True)
        for h in range(ATT_HEADS):
            sl = slice(h * HEAD_DIM, (h + 1) * HEAD_DIM)
            ot = acc_scr[h] / l_scr[h]
            ms = jnp.mean(ot * ot, axis=0, keepdims=True)
            o = (ot * lax.rsqrt(ms + RMS_EPS)).T
            o_ref[0, :, sl] = (o * g_ref[:, sl]).astype(o_ref.dtype)


def _prompt_attention(q, k, v, layer, c, g, blk):
    n, s, _ = q.shape
    nb = s // blk
    kv_map = lambda b, qi, ki: (layer, b, jnp.minimum(ki, qi), 0)
    return pl.pallas_call(
        _prompt_attn_kernel, grid=(n, nb, nb),
        in_specs=[pl.BlockSpec((1, blk, ATT_DIM), lambda b, qi, ki: (b, qi, 0)),
                  pl.BlockSpec((1, 1, blk, ATT_DIM), kv_map),
                  pl.BlockSpec((1, 1, blk, ATT_DIM), kv_map),
                  pl.BlockSpec((1, blk, LANES), lambda b, qi, ki: (b, jnp.minimum(ki, qi), 0)),
                  pl.BlockSpec((1, ATT_DIM), lambda b, qi, ki: (0, 0))],
        out_specs=pl.BlockSpec((1, blk, ATT_DIM), lambda b, qi, ki: (b, qi, 0)),
        out_shape=jax.ShapeDtypeStruct((n, s, ATT_DIM), BF16),
        scratch_shapes=[pltpu.VMEM((ATT_HEADS, 1, blk), F32),
                        pltpu.VMEM((ATT_HEADS, 1, blk), F32),
                        pltpu.VMEM((ATT_HEADS, HEAD_DIM, blk), F32)],
        compiler_params=_cp(3), name="prompt_attention")(q, k, v, c, g.reshape(1, ATT_DIM))


def _decode_attn_kernel(pt_ref, q_ref, kn_ref, vn_ref, lfn_ref, g_ref, *rest):
    del pt_ref
    npg = PAGES_PER_STEP
    k_refs, v_refs, lf_refs = rest[:npg], rest[npg:2 * npg], rest[2 * npg:3 * npg]
    o_ref, m_scr, l_scr, acc_scr, carry_scr, lf_scr = rest[3 * npg:]
    c = pl.program_id(1)
    width = ATT_HEADS * HEAD_DIM

    @pl.when(c == 0)
    def _():
        m_scr[...] = jnp.full(m_scr.shape, NEG_BIG, F32)
        l_scr[...] = jnp.zeros_like(l_scr)
        acc_scr[...] = jnp.zeros_like(acc_scr)
        carry_scr[...] = lfn_ref[0]

    lane = lax.broadcasted_iota(jnp.int32, (npg, width), 1)
    for p in range(npg):
        lf_scr[p:p + 1, :] = lf_refs[p][0, 0]
    lf = lf_scr[...]
    y = lf
    sh = ATT_HEADS
    while sh < width:
        y = y + jnp.where(lane < width - sh, pltpu.roll(y, width - sh, axis=1), 0.0)
        sh *= 2
    excl = y - lf
    z = jnp.where(lane < ATT_HEADS, y, 0.0)
    sh = ATT_HEADS
    while sh < width:
        z = z + pltpu.roll(z, sh, axis=1)
        sh *= 2
    carry = carry_scr[...]
    bias = [None] * npg
    for p in reversed(range(npg)):
        bias[p] = carry + excl[p:p + 1, :]
        carry = carry + z[p:p + 1, :]
    carry_scr[...] = carry

    q = q_ref[0]
    hrow = lax.broadcasted_iota(jnp.int32, (ATT_HEADS, width), 0)
    hlane = lax.broadcasted_iota(jnp.int32, (ATT_HEADS, width), 1)
    own = jnp.bitwise_and(hlane, ATT_HEADS - 1) == hrow
    s_all = []
    for p in range(npg):
        kp = k_refs[p][0, 0].astype(BF16)
        s = lax.dot_general(q, kp, (((1,), (1,)), ((), ())), preferred_element_type=F32)
        s_all.append(jnp.where(own, s * ATT_SCALE + bias[p], NEG_BIG))
    m_prev = m_scr[...]
    m_new = m_prev
    for s in s_all:
        m_new = jnp.maximum(m_new, jnp.max(s, axis=1, keepdims=True))
    alpha = jnp.exp(m_prev - m_new)
    l_new = alpha * l_scr[...]
    acc = alpha * acc_scr[...]
    for p in range(npg):
        pr = jnp.exp(s_all[p] - m_new)
        l_new = l_new + jnp.sum(pr, axis=1, keepdims=True)
        acc = acc + _dot(pr.astype(BF16), v_refs[p][0, 0].astype(BF16))
    m_scr[...] = m_new
    l_scr[...] = l_new
    acc_scr[...] = acc

    @pl.when(c == pl.num_programs(1) - 1)
    def _():
        kn = kn_ref[0].astype(BF16).astype(F32)
        vn = vn_ref[0].astype(BF16).astype(F32)
        s_new = jnp.sum(q.astype(F32) * kn, axis=1, keepdims=True) * ATT_SCALE
        m_fin = jnp.maximum(m_new, s_new)
        a = jnp.exp(m_new - m_fin)
        pn = jnp.exp(s_new - m_fin)
        o = (a * acc + pn * vn) / (a * l_new + pn)
        ms = jnp.mean(o * o, axis=-1, keepdims=True)
        o_ref[0] = (o * lax.rsqrt(ms + RMS_EPS) * g_ref[...]).astype(o_ref.dtype)


def _decode_attention(layer, page_table, q, kn, vn, lfn, g, cache_k, cache_v, cache_logf):
    n, n_pages = page_table.shape
    depth, n_pool, page, heads, hd = cache_k.shape
    width = page * heads
    ck = cache_k.reshape(depth, n_pool, width, hd)
    cv = cache_v.reshape(depth, n_pool, width, hd)
    cl = cache_logf.reshape(depth, n_pool, 1, width)
    n_chunks = n_pages // PAGES_PER_STEP

    def page_map(p):
        return lambda b, c, pt: (layer, pt[b, (n_chunks - 1 - c) * PAGES_PER_STEP + p], 0, 0)

    per_seq = lambda b, c, pt: (b, 0, 0)
    in_specs = [pl.BlockSpec((1, heads, hd), per_seq), pl.BlockSpec((1, heads, hd), per_seq),
                pl.BlockSpec((1, heads, hd), per_seq), pl.BlockSpec((1, 1, width), per_seq),
                pl.BlockSpec((heads, hd), lambda b, c, pt: (0, 0))]
    in_specs += [pl.BlockSpec((1, 1, width, hd), page_map(p)) for p in range(PAGES_PER_STEP)]
    in_specs += [pl.BlockSpec((1, 1, width, hd), page_map(p)) for p in range(PAGES_PER_STEP)]
    in_specs += [pl.BlockSpec((1, 1, 1, width), page_map(p)) for p in range(PAGES_PER_STEP)]
    grid_spec = pltpu.PrefetchScalarGridSpec(
        num_scalar_prefetch=1, grid=(n, n_chunks), in_specs=in_specs,
        out_specs=pl.BlockSpec((1, heads, hd), per_seq),
        scratch_shapes=[pltpu.VMEM((heads, 1), F32), pltpu.VMEM((heads, 1), F32),
                        pltpu.VMEM((heads, hd), F32), pltpu.VMEM((1, width), F32),
                        pltpu.VMEM((PAGES_PER_STEP, width), F32)])
    return pl.pallas_call(
        _decode_attn_kernel, grid_spec=grid_spec,
        out_shape=jax.ShapeDtypeStruct((n, heads, hd), BF16),
        compiler_params=_cp(2), name="decode_attention")(
            page_table, q, kn, vn, lfn, g.reshape(heads, hd),
            *([ck] * PAGES_PER_STEP), *([cv] * PAGES_PER_STEP), *([cl] * PAGES_PER_STEP))


def _pad_lanes(w):
    return jnp.pad(w, ((0, 0), (0, LANES - w.shape[1])))


def kernel(x_prompt, x_sample, cache_k, cache_v, cache_logf, state_conv, page_table, norm_mix_g,
           w_in, b_forget, conv_w, g_att_out, g_conv_out, w_out, norm_ffn_g, dense_w_gate,
           dense_w_up, dense_w_down, moe_router, moe_w_gate, moe_w_up, moe_w_down, final_norm_g):
    n, seq, d = x_prompt.shape
    n_dec, dec_seq, _ = x_sample.shape
    assert dec_seq == 1 and d == D_MODEL
    depth = w_in.shape[0]
    mp = n * seq
    tm_p, tm_s = 512, n_dec

    xp = x_prompt.reshape(mp, d)
    xs = x_sample.reshape(n_dec, d)
    xnp = _rmsnorm(xp, norm_mix_g[0], tm_p)
    xns = _rmsnorm(xs, norm_mix_g[0], tm_s)

    outs = {key: [] for key in ("fp", "cp", "ks", "vs", "fs", "cs")}
    yp = ys = k_all = v_all = None
    for l in range(depth):
        wl = w_in[l]
        o = 0
        wq = wl[:, o:o + ATT_DIM].astype(BF16); o += ATT_DIM
        wk = wl[:, o:o + ATT_DIM].astype(BF16); o += ATT_DIM
        wv = wl[:, o:o + ATT_DIM].astype(BF16); o += ATT_DIM
        wf = _pad_lanes(wl[:, o:o + ATT_HEADS]).astype(BF16); o += ATT_HEADS
        wb = wl[:, o:o + CONV_DIM].astype(BF16); o += CONV_DIM
        wc = wl[:, o:o + CONV_DIM].astype(BF16); o += CONV_DIM
        wh = wl[:, o:o + CONV_DIM].astype(BF16)
        bf = _pad_lanes(b_forget[l].reshape(1, ATT_HEADS))
        wo_a = w_out[l][:ATT_DIM].astype(BF16)
        wo_c = w_out[l][ATT_DIM:].astype(BF16)

        q_p = _mm(xnp, wq, BF16, 1024, 1024, "q_proj")
        k_all = _mm_layer(xnp, wk, k_all, l, depth, 1024, 1024, "k_proj")
        v_all = _mm_layer(xnp, wv, v_all, l, depth, 1024, 1024, "v_proj")
        lf_p, c_p = _forget(xnp, wf, bf, seq, True)
        conv_p, tails = _conv_branch(xnp, wb, wc, wh, conv_w[l], g_conv_out[l], seq, 512, 1024)
        att_p = _prompt_attention(q_p.reshape(n, seq, ATT_DIM),
                                  k_all.reshape(depth, n, seq, ATT_DIM),
                                  v_all.reshape(depth, n, seq, ATT_DIM), l,
                                  c_p.reshape(n, seq, LANES), g_att_out[l], 512)
        xp, hnp = _outproj(att_p.reshape(mp, ATT_DIM), conv_p, wo_a, wo_c, xp, norm_ffn_g[l], tm_p)
        outs["fp"].append(lf_p[:, :ATT_HEADS].reshape(n, seq, ATT_HEADS))
        tiles_per_seq = seq // 512
        outs["cp"].append(tails[tiles_per_seq - 1::tiles_per_seq])

        q_s = _mm(xns, wq, BF16, tm_s, 512, "q_proj_s")
        k_s = _mm(xns, wk, F32, tm_s, 512, "k_proj_s")
        v_s = _mm(xns, wv, F32, tm_s, 512, "v_proj_s")
        lf_s = _forget(xns, wf, bf, n_dec, False)[0]
        st = state_conv[l]
        conv_s, u_s = _conv_step(xns, wb, wc, wh, conv_w[l], g_conv_out[l], st[:, 0], st[:, 1], 512)
        lfn = jnp.tile(lf_s[:, :ATT_HEADS], (1, cache_k.shape[2])).reshape(n_dec, 1, -1)
        att_s = _decode_attention(l, page_table, q_s.reshape(n_dec, ATT_HEADS, HEAD_DIM),
                                  k_s.reshape(n_dec, ATT_HEADS, HEAD_DIM),
                                  v_s.reshape(n_dec, ATT_HEADS, HEAD_DIM), lfn, g_att_out[l],
                                  cache_k, cache_v, cache_logf)
        xs, hns = _outproj(att_s.reshape(n_dec, ATT_DIM), conv_s, wo_a, wo_c, xs, norm_ffn_g[l], tm_s)
        outs["ks"].append(k_s.reshape(n_dec, 1, ATT_HEADS, HEAD_DIM))
        outs["vs"].append(v_s.reshape(n_dec, 1, ATT_HEADS, HEAD_DIM))
        outs["fs"].append(lf_s[:, :ATT_HEADS].reshape(n_dec, 1, ATT_HEADS))
        outs["cs"].append(jnp.stack([st[:, 1], u_s], axis=1))

        last = l == depth - 1
        next_g = final_norm_g if last else norm_mix_g[l + 1]
        next_dtype = F32 if last else BF16
        i = l // 2
        if l % 2 == 0:
            wg = dense_w_gate[i].astype(BF16)
            wu = dense_w_up[i].astype(BF16)
            wd = dense_w_down[i].astype(BF16)
            hp = _swiglu(hnp, wg, wu, 1024, 512)
            xp, nxp = _down(hp, wd, xp, next_g, tm_p, 1408, not last, next_dtype)
            hs = _swiglu(hns, wg, wu, tm_s, 512)
            xs, nxs = _down(hs, wd, xs, next_g, tm_s, 1408, not last, next_dtype)
        else:
            xp, nxp, xs, nxs = _moe_ffn(
                xp, hnp, xs, hns, norm_ffn_g[l], moe_router[i], moe_w_gate[i].astype(BF16),
                moe_w_up[i].astype(BF16), moe_w_down[i].astype(BF16), next_g, not last, next_dtype)
        if last:
            yp, ys = nxp, nxs
        else:
            xnp, xns = nxp, nxs

    return (yp.reshape(n, seq, d), ys.reshape(n_dec, 1, d),
            k_all.reshape(depth, n, seq, ATT_HEADS, HEAD_DIM),
            v_all.reshape(depth, n, seq, ATT_HEADS, HEAD_DIM),
            jnp.stack(outs["fp"]), jnp.stack(outs["cp"]),
            jnp.stack(outs["ks"]), jnp.stack(outs["vs"]), jnp.stack(outs["fs"]), jnp.stack(outs["cs"]))
```

```python
import functools

import jax
import jax.numpy as jnp
from jax import lax
from jax.experimental import pallas as pl
from jax.experimental.pallas import tpu as pltpu

D_MODEL = 2048
HEAD_DIM = 128
ATT_HEADS = 8
ATT_DIM = ATT_HEADS * HEAD_DIM
CONV_DIM = D_MODEL - ATT_DIM
CONV_W = 3
N_EXPERTS = 8
TOP_K = 2
ATT_SCALE = HEAD_DIM ** -0.5
RMS_EPS = 1e-6
LANES = 128
NEG_BIG = -1e30
PAGES_PER_STEP = 16
VMEM_LIMIT = 56 * 1024 * 1024

F32 = jnp.float32
BF16 = jnp.bfloat16


def _cp(n_axes):
    return pltpu.CompilerParams(dimension_semantics=("arbitrary",) * n_axes,
                                vmem_limit_bytes=VMEM_LIMIT)


def _rms(x, g):
    return x * lax.rsqrt(jnp.mean(x * x, axis=-1, keepdims=True) + RMS_EPS) * g


def _dot(a, b):
    return jnp.dot(a, b, preferred_element_type=F32)


def _rmsnorm_kernel(x_ref, g_ref, o_ref):
    o_ref[...] = _rms(x_ref[...], g_ref[...]).astype(o_ref.dtype)


def _rmsnorm(x, g, tm, out_dtype=BF16):
    m, d = x.shape
    return pl.pallas_call(
        _rmsnorm_kernel, grid=(m // tm,),
        in_specs=[pl.BlockSpec((tm, d), lambda i: (i, 0)),
                  pl.BlockSpec((1, d), lambda i: (0, 0))],
        out_specs=pl.BlockSpec((tm, d), lambda i: (i, 0)),
        out_shape=jax.ShapeDtypeStruct((m, d), out_dtype),
        compiler_params=_cp(1), name="rmsnorm")(x, g.reshape(1, d))


def _mm_kernel(x_ref, w_ref, *rest):
    o_ref = rest[-1]
    o_ref[...] = _dot(x_ref[...], w_ref[...]).astype(o_ref.dtype).reshape(o_ref.shape)


def _mm_layer(x, w, stacked, layer, depth, tm, tn, name):
    m, k = x.shape
    n = w.shape[1]
    in_specs = [pl.BlockSpec((tm, k), lambda i, j: (i, 0)),
                pl.BlockSpec((k, tn), lambda i, j: (0, j))]
    args, aliases = [x, w], {}
    if stacked is not None:
        in_specs.append(pl.BlockSpec(memory_space=pl.ANY))
        args.append(stacked)
        aliases = {2: 0}
    return pl.pallas_call(
        _mm_kernel, grid=(m // tm, n // tn), in_specs=in_specs,
        out_specs=pl.BlockSpec((1, tm, tn), lambda i, j: (layer, i, j)),
        out_shape=jax.ShapeDtypeStruct((depth, m, n), F32), input_output_aliases=aliases,
        compiler_params=_cp(2), name=name)(*args)


def _mm(x, w, out_dtype, tm, tn, name):
    m, k = x.shape
    n = w.shape[1]
    return pl.pallas_call(
        _mm_kernel, grid=(m // tm, n // tn),
        in_specs=[pl.BlockSpec((tm, k), lambda i, j: (i, 0)),
                  pl.BlockSpec((k, tn), lambda i, j: (0, j))],
        out_specs=pl.BlockSpec((tm, tn), lambda i, j: (i, j)),
        out_shape=jax.ShapeDtypeStruct((m, n), out_dtype),
        compiler_params=_cp(2), name=name)(x, w)


def _swiglu_kernel(x_ref, wg_ref, wu_ref, o_ref):
    x = x_ref[...]
    a = _dot(x, wg_ref[...])
    b = _dot(x, wu_ref[...])
    o_ref[...] = (a * (1.0 / (1.0 + jnp.exp(-a))) * b).astype(o_ref.dtype)


def _swiglu(x, wg, wu, tm, tn):
    m, k = x.shape
    n = wg.shape[1]
    return pl.pallas_call(
        _swiglu_kernel, grid=(m // tm, n // tn),
        in_specs=[pl.BlockSpec((tm, k), lambda i, j: (i, 0)),
                  pl.BlockSpec((k, tn), lambda i, j: (0, j)),
                  pl.BlockSpec((k, tn), lambda i, j: (0, j))],
        out_specs=pl.BlockSpec((tm, tn), lambda i, j: (i, j)),
        out_shape=jax.ShapeDtypeStruct((m, n), BF16),
        compiler_params=_cp(2), name="swiglu_up")(x, wg, wu)


def _down_kernel(h_ref, w_ref, res_ref, g_ref, *rest, emit_x):
    xn_ref, acc_scr = rest[-2:]
    k = pl.program_id(1)

    @pl.when(k == 0)
    def _():
        acc_scr[...] = jnp.zeros_like(acc_scr)

    acc_scr[...] += _dot(h_ref[...], w_ref[...])

    @pl.when(k == pl.num_programs(1) - 1)
    def _():
        xo = res_ref[...] + acc_scr[...]
        if emit_x:
            rest[0][...] = xo
        xn_ref[...] = _rms(xo, g_ref[...]).astype(xn_ref.dtype)


def _down(h, w, res, norm_g, tm, tk, emit_x, norm_dtype):
    m, kdim = h.shape
    d = w.shape[1]
    row = lambda i, k: (i, 0)
    out_specs = [pl.BlockSpec((tm, d), row)]
    out_shape = [jax.ShapeDtypeStruct((m, d), norm_dtype)]
    if emit_x:
        out_specs.insert(0, pl.BlockSpec((tm, d), row))
        out_shape.insert(0, jax.ShapeDtypeStruct((m, d), F32))
    outs = pl.pallas_call(
        functools.partial(_down_kernel, emit_x=emit_x), grid=(m // tm, kdim // tk),
        in_specs=[pl.BlockSpec((tm, tk), lambda i, k: (i, k)),
                  pl.BlockSpec((tk, d), lambda i, k: (k, 0)),
                  pl.BlockSpec((tm, d), row), pl.BlockSpec((1, d), lambda i, k: (0, 0))],
        out_specs=out_specs, out_shape=out_shape, scratch_shapes=[pltpu.VMEM((tm, d), F32)],
        compiler_params=_cp(2), name="ffn_down")(h, w, res, norm_g.reshape(1, d))
    return (outs[0], outs[1]) if emit_x else (None, outs[0])


def _outproj_kernel(a_ref, c_ref, wa_ref, wc_ref, res_ref, g_ref, xo_ref, xn_ref):
    xo = res_ref[...] + (_dot(a_ref[...], wa_ref[...]) + _dot(c_ref[...], wc_ref[...]))
    xo_ref[...] = xo
    xn_ref[...] = _rms(xo, g_ref[...]).astype(xn_ref.dtype)


def _outproj(a, c, wa, wc, res, g, tm):
    m, d = res.shape
    ka, kc = a.shape[1], c.shape[1]
    row = lambda i: (i, 0)
    fixed = lambda i: (0, 0)
    return pl.pallas_call(
        _outproj_kernel, grid=(m // tm,),
        in_specs=[pl.BlockSpec((tm, ka), row), pl.BlockSpec((tm, kc), row),
                  pl.BlockSpec((ka, d), fixed), pl.BlockSpec((kc, d), fixed),
                  pl.BlockSpec((tm, d), row), pl.BlockSpec((1, d), fixed)],
        out_specs=[pl.BlockSpec((tm, d), row), pl.BlockSpec((tm, d), row)],
        out_shape=[jax.ShapeDtypeStruct((m, d), F32), jax.ShapeDtypeStruct((m, d), BF16)],
        compiler_params=_cp(1), name="out_proj")(a, c, wa, wc, res, g.reshape(1, d))


SEL_E1, SEL_E2, SEL_G1, SEL_G2, SEL_R1, SEL_R2 = range(6)


def _router_kernel(x_ref, r_ref, cin_ref, sel_ref, cnt_ref, carry_scr, *, n_valid):
    i = pl.program_id(0)
    tm = x_ref.shape[0]

    @pl.when(i == 0)
    def _():
        carry_scr[...] = cin_ref[...]

    logits = _dot(x_ref[...], r_ref[...])
    lane = lax.broadcasted_iota(jnp.int32, logits.shape, 1).astype(F32)
    lg = jnp.where(lane < N_EXPERTS, logits, -jnp.inf)
    m1 = jnp.max(lg, axis=1, keepdims=True)
    i1 = jnp.min(jnp.where(lg == m1, lane, float(LANES)), axis=1, keepdims=True)
    lg2 = jnp.where(lane == i1, -jnp.inf, lg)
    m2 = jnp.max(lg2, axis=1, keepdims=True)
    i2 = jnp.min(jnp.where(lg2 == m2, lane, float(LANES)), axis=1, keepdims=True)
    e2 = jnp.exp(m2 - m1)
    den = 1.0 + e2
    row = lax.broadcasted_iota(jnp.int32, logits.shape, 0) + i * tm
    hit = jnp.where(((lane == i1) | (lane == i2)) & (row < n_valid), 1.0, 0.0)
    r = lax.broadcasted_iota(jnp.int32, (tm, tm), 0)
    c = lax.broadcasted_iota(jnp.int32, (tm, tm), 1)
    before = (c < r).astype(BF16)
    rank = _dot(before, hit.astype(BF16)) + carry_scr[...]
    carry_scr[...] = carry_scr[...] + jnp.sum(hit, axis=0, keepdims=True)
    cnt_ref[...] = carry_scr[...]
    rank1 = jnp.sum(jnp.where(lane == i1, rank, 0.0), axis=1, keepdims=True)
    rank2 = jnp.sum(jnp.where(lane == i2, rank, 0.0), axis=1, keepdims=True)
    sel = jnp.zeros_like(logits)
    for idx, val in ((SEL_E1, i1), (SEL_E2, i2), (SEL_G1, 1.0 / den), (SEL_G2, e2 / den),
                     (SEL_R1, rank1), (SEL_R2, rank2)):
        sel = jnp.where(lane == idx, val, sel)
    sel_ref[...] = sel


def _router(hn, r_pad, count_in, tm, n_valid):
    m, d = hn.shape
    fixed = lambda i: (0, 0)
    return pl.pallas_call(
        functools.partial(_router_kernel, n_valid=n_valid), grid=(m // tm,),
        in_specs=[pl.BlockSpec((tm, d), lambda i: (i, 0)), pl.BlockSpec((d, LANES), fixed),
                  pl.BlockSpec((1, LANES), fixed)],
        out_specs=[pl.BlockSpec((tm, LANES), lambda i: (i, 0)), pl.BlockSpec((1, LANES), fixed)],
        out_shape=[jax.ShapeDtypeStruct((m, LANES), F32), jax.ShapeDtypeStruct((1, LANES), F32)],
        scratch_shapes=[pltpu.VMEM((1, LANES), F32)],
        compiler_params=_cp(1), name="router")(hn, r_pad, count_in)


def _dispatch_kernel(d1_ref, d2_ref, ps_ref, pc_ref, x_ref, *rest, first):
    xs_ref, zero_scr, sem = rest[-3:]
    tt = x_ref.shape[0]
    base = pl.program_id(0) * tt

    def row_copy(src_row, dst_row):
        return pltpu.make_async_copy(src_row, xs_ref.at[pl.ds(dst_row, 1), :], sem)

    def issue(r, carry):
        src = x_ref.at[pl.ds(r, 1), :]
        row_copy(src, d1_ref[base + r]).start()
        row_copy(src, d2_ref[base + r]).start(priority=1)
        return carry

    def drain(r, carry):
        row_copy(x_ref.at[pl.ds(0, 1), :], 0).wait()
        row_copy(x_ref.at[pl.ds(0, 1), :], 0).wait()
        return carry

    lax.fori_loop(0, tt, issue, 0, unroll=8)
    lax.fori_loop(0, tt, drain, 0, unroll=8)

    if first:
        @pl.when(pl.program_id(0) == 0)
        def _():
            zero_scr[...] = jnp.zeros_like(zero_scr)
            zrow = zero_scr.at[pl.ds(0, 1), :]
            for e in range(N_EXPERTS):
                def fill(r, carry, e=e):
                    row_copy(zrow, ps_ref[e] + r).start()
                    return carry

                def fill_done(r, carry):
                    row_copy(zrow, 0).wait()
                    return carry

                lax.fori_loop(0, pc_ref[e], fill, 0)
                lax.fori_loop(0, pc_ref[e], fill_done, 0)


def _dispatch(d1, d2, pad_start, pad_count, x, xs, tt, n_rows):
    m, d = x.shape
    first = xs is None
    in_specs = [pl.BlockSpec((tt, d), lambda i, *_: (i, 0))]
    args = [d1, d2, pad_start, pad_count, x]
    aliases = {}
    if not first:
        in_specs.append(pl.BlockSpec(memory_space=pl.ANY))
        args.append(xs)
        aliases = {5: 0}
    grid_spec = pltpu.PrefetchScalarGridSpec(
        num_scalar_prefetch=4, grid=(m // tt,), in_specs=in_specs,
        out_specs=pl.BlockSpec(memory_space=pl.ANY),
        scratch_shapes=[pltpu.VMEM((8, d), F32), pltpu.SemaphoreType.DMA(())])
    return pl.pallas_call(
        functools.partial(_dispatch_kernel, first=first), grid_spec=grid_spec,
        out_shape=jax.ShapeDtypeStruct((n_rows, d), F32), input_output_aliases=aliases,
        compiler_params=_cp(1), name="moe_dispatch")(*args)


def _moe_up_kernel(te_ref, na_ref, x_ref, g_ref, wg_ref, wu_ref, o_ref):
    del te_ref

    @pl.when(pl.program_id(1) < na_ref[0])
    def _():
        xn = _rms(x_ref[...], g_ref[...]).astype(BF16)
        a = _dot(xn, wg_ref[0])
        b = _dot(xn, wu_ref[0])
        o_ref[...] = (a * (1.0 / (1.0 + jnp.exp(-a))) * b).astype(o_ref.dtype)


def _moe_up(tile_expert, n_active, xs, g, wg, wu, tm, tn):
    r, d = xs.shape
    f = wg.shape[2]
    tile = lambda t, na: jnp.minimum(t, na[0] - 1)
    w_map = lambda j, t, te, na: (te[tile(t, na)], 0, j)
    grid_spec = pltpu.PrefetchScalarGridSpec(
        num_scalar_prefetch=2, grid=(f // tn, r // tm),
        in_specs=[pl.BlockSpec((tm, d), lambda j, t, te, na: (tile(t, na), 0)),
                  pl.BlockSpec((1, d), lambda j, t, te, na: (0, 0)),
                  pl.BlockSpec((1, d, tn), w_map), pl.BlockSpec((1, d, tn), w_map)],
        out_specs=pl.BlockSpec((tm, tn), lambda j, t, te, na: (tile(t, na), j)))
    return pl.pallas_call(
        _moe_up_kernel, grid_spec=grid_spec, out_shape=jax.ShapeDtypeStruct((r, f), BF16),
        compiler_params=_cp(2), name="moe_up")(tile_expert, n_active, xs, g.reshape(1, d), wg, wu)


def _moe_down_kernel(te_ref, na_ref, h_ref, w_ref, o_ref):
    del te_ref

    @pl.when(pl.program_id(0) < na_ref[0])
    def _():
        o_ref[...] = _dot(h_ref[...], w_ref[0])


def _moe_down(tile_expert, n_active, h, wd, tm):
    r, f = h.shape
    d = wd.shape[2]
    tile = lambda t, na: jnp.minimum(t, na[0] - 1)
    grid_spec = pltpu.PrefetchScalarGridSpec(
        num_scalar_prefetch=2, grid=(r // tm,),
        in_specs=[pl.BlockSpec((tm, f), lambda t, te, na: (tile(t, na), 0)),
                  pl.BlockSpec((1, f, d), lambda t, te, na: (te[tile(t, na)], 0, 0))],
        out_specs=pl.BlockSpec((tm, d), lambda t, te, na: (tile(t, na), 0)))
    return pl.pallas_call(
        _moe_down_kernel, grid_spec=grid_spec, out_shape=jax.ShapeDtypeStruct((r, d), F32),
        compiler_params=_cp(1), name="moe_down")(tile_expert, n_active, h, wd)


def _combine_kernel(d1_ref, d2_ref, ys_ref, res_ref, sel_ref, g_ref, *rest, emit_x):
    y1_scr, y2_scr, sem = rest[-3:]
    outs = rest[:-3]
    tt = res_ref.shape[0]
    base = pl.program_id(0) * tt

    def row_copy(src_row, dst):
        return pltpu.make_async_copy(ys_ref.at[pl.ds(src_row, 1), :], dst, sem)

    def issue(r, carry):
        row_copy(d1_ref[base + r], y1_scr.at[pl.ds(r, 1), :]).start()
        row_copy(d2_ref[base + r], y2_scr.at[pl.ds(r, 1), :]).start(priority=1)
        return carry

    def drain(r, carry):
        row_copy(0, y1_scr.at[pl.ds(0, 1), :]).wait()
        row_copy(0, y2_scr.at[pl.ds(0, 1), :]).wait()
        return carry

    lax.fori_loop(0, tt, issue, 0, unroll=8)
    lax.fori_loop(0, tt, drain, 0, unroll=8)
    sel = sel_ref[...]
    g1 = sel[:, SEL_G1:SEL_G1 + 1]
    g2 = sel[:, SEL_G2:SEL_G2 + 1]
    xo = res_ref[...] + (g1 * y1_scr[...] + g2 * y2_scr[...])
    if emit_x:
        outs[0][...] = xo
    outs[-1][...] = _rms(xo, g_ref[...]).astype(outs[-1].dtype)


def _combine(d1, d2, ys, res, sel, g, tt, emit_x, norm_dtype):
    m, d = res.shape
    row = lambda i, *_: (i, 0)
    out_specs = [pl.BlockSpec((tt, d), row)]
    out_shape = [jax.ShapeDtypeStruct((m, d), norm_dtype)]
    if emit_x:
        out_specs.insert(0, pl.BlockSpec((tt, d), row))
        out_shape.insert(0, jax.ShapeDtypeStruct((m, d), F32))
    grid_spec = pltpu.PrefetchScalarGridSpec(
        num_scalar_prefetch=2, grid=(m // tt,),
        in_specs=[pl.BlockSpec(memory_space=pl.ANY), pl.BlockSpec((tt, d), row),
                  pl.BlockSpec((tt, LANES), row), pl.BlockSpec((1, d), lambda i, *_: (0, 0))],
        out_specs=out_specs,
        scratch_shapes=[pltpu.VMEM((tt, d), F32), pltpu.VMEM((tt, d), F32),
                        pltpu.SemaphoreType.DMA(())])
    outs = pl.pallas_call(
        functools.partial(_combine_kernel, emit_x=emit_x), grid_spec=grid_spec,
        out_shape=out_shape, compiler_params=_cp(1), name="moe_combine")(
            d1, d2, ys, res, sel, g.reshape(1, d))
    return (outs[0], outs[1]) if emit_x else (None, outs[0])


def _moe_ffn(x_p, hn_p, x_s, hn_s, ffn_g, router, wg, wu, wd, next_g, emit_x, norm_dtype,
             tm=512, tn=1408, t_route=512, t_move=512, t_comb=256):
    mp, d = x_p.shape
    ms = x_s.shape[0]
    n_tiles = (TOP_K * (mp + ms)) // tm + N_EXPERTS
    r_pad = _pad_lanes(router).astype(BF16)
    ms_pad = -(-ms // LANES) * LANES
    hn_s_pad = jnp.pad(hn_s, ((0, ms_pad - ms), (0, 0)))
    sel_p, cnt_p = _router(hn_p, r_pad, jnp.zeros((1, LANES), F32), t_route, mp)
    sel_s, cnt = _router(hn_s_pad, r_pad, cnt_p, ms_pad, ms)

    count = cnt[0, :N_EXPERTS].astype(jnp.int32)
    padded = (count + tm - 1) // tm * tm
    g_end = jnp.cumsum(padded)
    g_start = g_end - padded
    tile_expert = jnp.minimum(jnp.searchsorted(g_end // tm, jnp.arange(n_tiles), side="right"),
                              N_EXPERTS - 1).astype(jnp.int32)
    n_active = (g_end[-1:] // tm).astype(jnp.int32)

    def dests(sel, rows):
        e1, e2 = sel[:rows, SEL_E1].astype(jnp.int32), sel[:rows, SEL_E2].astype(jnp.int32)
        return (g_start[e1] + sel[:rows, SEL_R1].astype(jnp.int32),
                g_start[e2] + sel[:rows, SEL_R2].astype(jnp.int32))

    d1_p, d2_p = dests(sel_p, mp)
    d1_s, d2_s = dests(sel_s, ms)
    pad_start = g_start + count
    pad_count = padded - count
    xs = _dispatch(d1_p, d2_p, pad_start, pad_count, x_p, None, t_move, n_tiles * tm)
    xs = _dispatch(d1_s, d2_s, pad_start, pad_count, x_s, xs, ms, n_tiles * tm)
    h = _moe_up(tile_expert, n_active, xs, ffn_g, wg, wu, tm, tn)
    ys = _moe_down(tile_expert, n_active, h, wd, tm)
    xo_p, xn_p = _combine(d1_p, d2_p, ys, x_p, sel_p, next_g, t_comb, emit_x, norm_dtype)
    xo_s, xn_s = _combine(d1_s, d2_s, ys, x_s, sel_s, next_g, ms, emit_x, norm_dtype)
    return xo_p, xn_p, xo_s, xn_s


def _forget_kernel(x_ref, w_ref, b_ref, lf_ref, *c_ref, cumsum):
    z = _dot(x_ref[...], w_ref[...]) + b_ref[...]
    lf = jnp.minimum(z, 0.0) - jnp.log1p(jnp.exp(-jnp.abs(z)))
    lf_ref[...] = lf
    if cumsum:
        r = lax.broadcasted_iota(jnp.int32, (LANES, LANES), 0)
        c = lax.broadcasted_iota(jnp.int32, (LANES, LANES), 1)
        tri = (c <= r).astype(F32)
        carry = jnp.zeros((1, LANES), F32)
        for blk in range(lf.shape[0] // LANES):
            rows = slice(blk * LANES, (blk + 1) * LANES)
            cb = jnp.dot(tri, lf[rows, :], preferred_element_type=F32,
                         precision=lax.Precision.HIGHEST) + carry
            carry = cb[LANES - 1:LANES, :]
            c_ref[0][rows, :] = cb


def _forget(xn, wf_pad, bf_pad, rows, cumsum):
    m, d = xn.shape
    n = m // rows
    out_specs = [pl.BlockSpec((rows, LANES), lambda b: (b, 0))]
    out_shape = [jax.ShapeDtypeStruct((m, LANES), F32)]
    if cumsum:
        out_specs.append(pl.BlockSpec((rows, LANES), lambda b: (b, 0)))
        out_shape.append(jax.ShapeDtypeStruct((m, LANES), F32))
    return pl.pallas_call(
        functools.partial(_forget_kernel, cumsum=cumsum), grid=(n,),
        in_specs=[pl.BlockSpec((rows, d), lambda b: (b, 0)),
                  pl.BlockSpec((d, LANES), lambda b: (0, 0)),
                  pl.BlockSpec((1, LANES), lambda b: (0, 0))],
        out_specs=out_specs, out_shape=out_shape,
        compiler_params=_cp(1), name="forget_gate")(xn, wf_pad, bf_pad)


def _group_norm_store(yc, g_ref, o_ref):
    for gi in range(yc.shape[1] // HEAD_DIM):
        sl = slice(gi * HEAD_DIM, (gi + 1) * HEAD_DIM)
        yg = yc[:, sl]
        ms = jnp.mean(yg * yg, axis=-1, keepdims=True)
        o_ref[:, sl] = (yg * lax.rsqrt(ms + RMS_EPS) * g_ref[:, sl]).astype(o_ref.dtype)


def _conv_kernel(x_ref, wb_ref, wc_ref, wh_ref, cw_ref, g_ref, o_ref, tail_ref, halo_scr,
                 *, tiles_per_seq):
    i, j = pl.program_id(0), pl.program_id(1)
    x = x_ref[...]
    gate_b = _dot(x, wb_ref[...])
    u = _dot(x, wc_ref[...]) * _dot(x, wh_ref[...])
    tm = u.shape[0]

    @pl.when(i % tiles_per_seq == 0)
    def _():
        halo_scr[j] = jnp.zeros(halo_scr.shape[1:], F32)

    prev = halo_scr[j]
    row = lax.broadcasted_iota(jnp.int32, u.shape, 0)
    u1 = jnp.where(row == 0, prev[1:2, :], pltpu.roll(u, 1, axis=0))
    u2 = jnp.where(row == 0, prev[0:1, :],
                   jnp.where(row == 1, prev[1:2, :], pltpu.roll(u, 2, axis=0)))
    cw = cw_ref[...]
    y = u2 * cw[0:1, :] + u1 * cw[1:2, :] + u * cw[2:3, :]
    _group_norm_store(gate_b * y, g_ref, o_ref)
    tail = u[tm - (CONV_W - 1):tm, :]
    halo_scr[j] = tail
    tail_ref[0] = tail


def _conv_branch(xn, wb, wc, wh, cw, g, seq, tm, tc):
    m, d = xn.shape
    n_i, n_j = m // tm, CONV_DIM // tc
    col = lambda i, j: (0, j)
    return pl.pallas_call(
        functools.partial(_conv_kernel, tiles_per_seq=seq // tm), grid=(n_i, n_j),
        in_specs=[pl.BlockSpec((tm, d), lambda i, j: (i, 0)),
                  pl.BlockSpec((d, tc), col), pl.BlockSpec((d, tc), col),
                  pl.BlockSpec((d, tc), col), pl.BlockSpec((CONV_W, tc), col),
                  pl.BlockSpec((1, tc), col)],
        out_specs=[pl.BlockSpec((tm, tc), lambda i, j: (i, j)),
                   pl.BlockSpec((1, CONV_W - 1, tc), lambda i, j: (i, 0, j))],
        out_shape=[jax.ShapeDtypeStruct((m, CONV_DIM), BF16),
                   jax.ShapeDtypeStruct((n_i, CONV_W - 1, CONV_DIM), F32)],
        scratch_shapes=[pltpu.VMEM((n_j, CONV_W - 1, tc), F32)],
        compiler_params=_cp(2), name="conv_branch")(xn, wb, wc, wh, cw, g.reshape(1, CONV_DIM))


def _conv_step_kernel(x_ref, wb_ref, wc_ref, wh_ref, cw_ref, g_ref, p0_ref, p1_ref, o_ref, u_ref):
    x = x_ref[...]
    gate_b = _dot(x, wb_ref[...])
    u = _dot(x, wc_ref[...]) * _dot(x, wh_ref[...])
    cw = cw_ref[...]
    y = p0_ref[...] * cw[0:1, :] + p1_ref[...] * cw[1:2, :] + u * cw[2:3, :]
    _group_norm_store(gate_b * y, g_ref, o_ref)
    u_ref[...] = u


def _conv_step(xn, wb, wc, wh, cw, g, p0, p1, tc):
    m, d = xn.shape
    col = lambda j: (0, j)
    return pl.pallas_call(
        _conv_step_kernel, grid=(CONV_DIM // tc,),
        in_specs=[pl.BlockSpec((m, d), lambda j: (0, 0)),
                  pl.BlockSpec((d, tc), col), pl.BlockSpec((d, tc), col),
                  pl.BlockSpec((d, tc), col), pl.BlockSpec((CONV_W, tc), col),
                  pl.BlockSpec((1, tc), col), pl.BlockSpec((m, tc), col),
                  pl.BlockSpec((m, tc), col)],
        out_specs=[pl.BlockSpec((m, tc), col), pl.BlockSpec((m, tc), col)],
        out_shape=[jax.ShapeDtypeStruct((m, CONV_DIM), BF16),
                   jax.ShapeDtypeStruct((m, CONV_DIM), F32)],
        compiler_params=_cp(1), name="conv_step")(xn, wb, wc, wh, cw, g.reshape(1, CONV_DIM), p0, p1)


def _prompt_step(q_ref, k_ref, v_ref, c_ref, m_scr, l_scr, acc_scr, key_shift):
    tq, tk = q_ref.shape[1], k_ref.shape[2]
    if key_shift is not None:
        key = lax.broadcasted_iota(jnp.int32, (tk, tq), 0) + key_shift
        qry = lax.broadcasted_iota(jnp.int32, (tk, tq), 1)
        keep = key <= qry
    for h in range(ATT_HEADS):
        sl = slice(h * HEAD_DIM, (h + 1) * HEAD_DIM)
        q = q_ref[0, :, sl]
        k = k_ref[0, 0, :, sl].astype(BF16)
        st = lax.dot_general(k, q, (((1,), (1,)), ((), ())), preferred_element_type=F32)
        st = st * ATT_SCALE - c_ref[0, :, h:h + 1]
        if key_shift is not None:
            st = jnp.where(keep, st, NEG_BIG)
        m_prev = m_scr[h]
        m_new = jnp.maximum(m_prev, jnp.max(st, axis=0, keepdims=True))
        alpha = jnp.exp(m_prev - m_new)
        p = jnp.exp(st - m_new)
        l_scr[h] = alpha * l_scr[h] + jnp.sum(p, axis=0, keepdims=True)
        vt = v_ref[0, 0, :, sl].T.astype(BF16)
        acc_scr[h] = alpha * acc_scr[h] + _dot(vt, p.astype(BF16))
        m_scr[h] = m_new


def _prompt_finish(g_ref, o_ref, l_scr, acc_scr):
    for h in range(ATT_HEADS):
        sl = slice(h * HEAD_DIM, (h + 1) * HEAD_DIM)
        ot = acc_scr[h] / l_scr[h]
        ms = jnp.mean(ot * ot, axis=0, keepdims=True)
        o = (ot * lax.rsqrt(ms + RMS_EPS)).T
        o_ref[0, :, sl] = (o * g_ref[:, sl]).astype(o_ref.dtype)


def _decode_step(c, n_chunks, q_ref, kn_ref, vn_ref, lfn_ref, g_ref, k_refs, v_refs, lf_refs,
                 o_ref, m_scr, l_scr, acc_scr, carry_scr, lf_scr):
    npg = PAGES_PER_STEP
    width = lf_scr.shape[1]

    @pl.when(c == 0)
    def _():
        m_scr[...] = jnp.full(m_scr.shape, NEG_BIG, F32)
        l_scr[...] = jnp.zeros_like(l_scr)
        acc_scr[...] = jnp.zeros_like(acc_scr)
        carry_scr[...] = lfn_ref[0]

    lane = lax.broadcasted_iota(jnp.int32, (npg, width), 1)
    for p in range(npg):
        lf_scr[p:p + 1, :] = lf_refs[p][0, 0]
    lf = lf_scr[...]
    y = lf
    sh = ATT_HEADS
    while sh < width:
        y = y + jnp.where(lane < width - sh, pltpu.roll(y, width - sh, axis=1), 0.0)
        sh *= 2
    excl = y - lf
    z = jnp.where(lane < ATT_HEADS, y, 0.0)
    sh = ATT_HEADS
    while sh < width:
        z = z + pltpu.roll(z, sh, axis=1)
        sh *= 2
    carry = carry_scr[...]
    bias = [None] * npg
    for p in reversed(range(npg)):
        bias[p] = carry + excl[p:p + 1, :]
        carry = carry + z[p:p + 1, :]
    carry_scr[...] = carry

    q = q_ref[0]
    hrow = lax.broadcasted_iota(jnp.int32, (ATT_HEADS, width), 0)
    hlane = lax.broadcasted_iota(jnp.int32, (ATT_HEADS, width), 1)
    own = jnp.bitwise_and(hlane, ATT_HEADS - 1) == hrow
    s_all = []
    for p in range(npg):
        kp = k_refs[p][0, 0].astype(BF16)
        s = lax.dot_general(q, kp, (((1,), (1,)), ((), ())), preferred_element_type=F32)
        s_all.append(jnp.where(own, s * ATT_SCALE + bias[p], NEG_BIG))
    m_prev = m_scr[...]
    m_new = m_prev
    for s in s_all:
        m_new = jnp.maximum(m_new, jnp.max(s, axis=1, keepdims=True))
    alpha = jnp.exp(m_prev - m_new)
    l_new = alpha * l_scr[...]
    acc = alpha * acc_scr[...]
    for p in range(npg):
        pr = jnp.exp(s_all[p] - m_new)
        l_new = l_new + jnp.sum(pr, axis=1, keepdims=True)
        acc = acc + _dot(pr.astype(BF16), v_refs[p][0, 0].astype(BF16))
    m_scr[...] = m_new
    l_scr[...] = l_new
    acc_scr[...] = acc

    @pl.when(c == n_chunks - 1)
    def _():
        kn = kn_ref[0].astype(BF16).astype(F32)
        vn = vn_ref[0].astype(BF16).astype(F32)
        s_new = jnp.sum(q.astype(F32) * kn, axis=1, keepdims=True) * ATT_SCALE
        m_fin = jnp.maximum(m_new, s_new)
        a = jnp.exp(m_new - m_fin)
        pn = jnp.exp(s_new - m_fin)
        o = (a * acc + pn * vn) / (a * l_new + pn)
        ms = jnp.mean(o * o, axis=-1, keepdims=True)
        o_ref[0] = (o * lax.rsqrt(ms + RMS_EPS) * g_ref[...]).astype(o_ref.dtype)


def _attention_kernel(pt_ref, q_ref, k_ref, v_ref, c_ref, gp_ref, qs_ref, kn_ref, vn_ref, lfn_ref,
                      gs_ref, *rest):
    del pt_ref
    npg = PAGES_PER_STEP
    k_refs, v_refs, lf_refs = rest[:npg], rest[npg:2 * npg], rest[2 * npg:3 * npg]
    (o_ref, os_ref, m_scr, l_scr, acc_scr,
     dm_scr, dl_scr, dacc_scr, carry_scr, lf_scr) = rest[3 * npg:]
    qi, ki = pl.program_id(1), pl.program_id(2)
    tq, tk = q_ref.shape[1], k_ref.shape[2]
    first_diag = (tq // tk) * qi
    last = first_diag + tq // tk - 1

    def decode():
        _decode_step(ki, pl.num_programs(2), qs_ref, kn_ref, vn_ref, lfn_ref, gs_ref, k_refs,
                     v_refs, lf_refs, os_ref, dm_scr, dl_scr, dacc_scr, carry_scr, lf_scr)

    @pl.when(ki == 0)
    def _():
        m_scr[...] = jnp.full(m_scr.shape, NEG_BIG, F32)
        l_scr[...] = jnp.zeros_like(l_scr)
        acc_scr[...] = jnp.zeros_like(acc_scr)

    @pl.when(ki < first_diag)
    def _():
        decode()
        _prompt_step(q_ref, k_ref, v_ref, c_ref, m_scr, l_scr, acc_scr, None)

    @pl.when((ki >= first_diag) & (ki <= last))
    def _():
        _prompt_step(q_ref, k_ref, v_ref, c_ref, m_scr, l_scr, acc_scr, ki * tk - qi * tq)

        @pl.when(ki == last)
        def _():
            _prompt_finish(gp_ref, o_ref, l_scr, acc_scr)

        decode()

    @pl.when(ki > last)
    def _():
        decode()


def _attention(layer, q, k, v, c, g, page_table, qs, kn, vn, lfn, cache_k, cache_v, cache_logf,
               tq, tk):
    n, s, _ = q.shape
    nq, nk = s // tq, s // tk
    n_dec, n_pages = page_table.shape
    depth, n_pool, page, heads, hd = cache_k.shape
    width = page * heads
    assert n_dec == n * nq and n_pages == nk * PAGES_PER_STEP and tq % tk == 0
    ck = cache_k.reshape(depth, n_pool, width, hd)
    cv = cache_v.reshape(depth, n_pool, width, hd)
    cl = cache_logf.reshape(depth, n_pool, 1, width)

    def kv_block(qi, ki):
        return jnp.minimum(ki, (tq // tk) * (qi + 1) - 1)

    def page_map(p):
        return lambda b, qi, ki, pt: (layer, pt[b * nq + qi, (nk - 1 - ki) * PAGES_PER_STEP + p], 0, 0)

    kv_map = lambda b, qi, ki, pt: (layer, b, kv_block(qi, ki), 0)
    q_map = lambda b, qi, ki, pt: (b, qi, 0)
    seq_map = lambda b, qi, ki, pt: (b * nq + qi, 0, 0)
    fixed = lambda b, qi, ki, pt: (0, 0)
    in_specs = [pl.BlockSpec((1, tq, ATT_DIM), q_map),
                pl.BlockSpec((1, 1, tk, ATT_DIM), kv_map), pl.BlockSpec((1, 1, tk, ATT_DIM), kv_map),
                pl.BlockSpec((1, tk, LANES), lambda b, qi, ki, pt: (b, kv_block(qi, ki), 0)),
                pl.BlockSpec((1, ATT_DIM), fixed),
                pl.BlockSpec((1, heads, hd), seq_map), pl.BlockSpec((1, heads, hd), seq_map),
                pl.BlockSpec((1, heads, hd), seq_map), pl.BlockSpec((1, 1, width), seq_map),
                pl.BlockSpec((heads, hd), fixed)]
    in_specs += [pl.BlockSpec((1, 1, width, hd), page_map(p)) for p in range(PAGES_PER_STEP)]
    in_specs += [pl.BlockSpec((1, 1, width, hd), page_map(p)) for p in range(PAGES_PER_STEP)]
    in_specs += [pl.BlockSpec((1, 1, 1, width), page_map(p)) for p in range(PAGES_PER_STEP)]
    grid_spec = pltpu.PrefetchScalarGridSpec(
        num_scalar_prefetch=1, grid=(n, nq, nk), in_specs=in_specs,
        out_specs=[pl.BlockSpec((1, tq, ATT_DIM), q_map), pl.BlockSpec((1, heads, hd), seq_map)],
        scratch_shapes=[pltpu.VMEM((ATT_HEADS, 1, tq), F32), pltpu.VMEM((ATT_HEADS, 1, tq), F32),
                        pltpu.VMEM((ATT_HEADS, HEAD_DIM, tq), F32),
                        pltpu.VMEM((heads, 1), F32), pltpu.VMEM((heads, 1), F32),
                        pltpu.VMEM((heads, hd), F32), pltpu.VMEM((1, width), F32),
                        pltpu.VMEM((PAGES_PER_STEP, width), F32)])
    return pl.pallas_call(
        _attention_kernel, grid_spec=grid_spec,
        out_shape=[jax.ShapeDtypeStruct((n, s, ATT_DIM), BF16),
                   jax.ShapeDtypeStruct((n_dec, heads, hd), BF16)],
        compiler_params=_cp(3), name="attention")(
            page_table, q, k, v, c, g.reshape(1, ATT_DIM), qs, kn, vn, lfn, g.reshape(heads, hd),
            *([ck] * PAGES_PER_STEP), *([cv] * PAGES_PER_STEP), *([cl] * PAGES_PER_STEP))


def _prompt_attn_kernel(q_ref, k_ref, v_ref, c_ref, g_ref, o_ref, m_scr, l_scr, acc_scr):
    qi, ki = pl.program_id(1), pl.program_id(2)
    tq, tk = q_ref.shape[1], k_ref.shape[2]

    @pl.when(ki == 0)
    def _():
        m_scr[...] = jnp.full(m_scr.shape, NEG_BIG, F32)
        l_scr[...] = jnp.zeros_like(l_scr)
        acc_scr[...] = jnp.zeros_like(acc_scr)

    def update(masked):
        if masked:
            key = lax.broadcasted_iota(jnp.int32, (tk, tq), 0)
            qry = lax.broadcasted_iota(jnp.int32, (tk, tq), 1)
            keep = key <= qry
        for h in range(ATT_HEADS):
            sl = slice(h * HEAD_DIM, (h + 1) * HEAD_DIM)
            q = q_ref[0, :, sl]
            k = k_ref[0, 0, :, sl].astype(BF16)
            st = lax.dot_general(k, q, (((1,), (1,)), ((), ())), preferred_element_type=F32)
            st = st * ATT_SCALE - c_ref[0, :, h:h + 1]
            if masked:
                st = jnp.where(keep, st, NEG_BIG)
            m_prev = m_scr[h]
            m_new = jnp.maximum(m_prev, jnp.max(st, axis=0, keepdims=True))
            alpha = jnp.exp(m_prev - m_new)
            p = jnp.exp(st - m_new)
            l_scr[h] = alpha * l_scr[h] + jnp.sum(p, axis=0, keepdims=True)
            vt = v_ref[0, 0, :, sl].T.astype(BF16)
            acc_scr[h] = alpha * acc_scr[h] + _dot(vt, p.astype(BF16))
            m_scr[h] = m_new

    @pl.when(ki < qi)
    def _():
        update(False)

    @pl.when(ki == qi)
    def _():
        update(True)
        for h in range(ATT_HEADS):
            sl = slice(h * HEAD_DIM, (h + 1) * HEAD_DIM)
            ot = acc_scr[h] / l_scr[h]
            ms = jnp.mean(ot * ot, axis=0, keepdims=True)
            o = (ot * lax.rsqrt(ms + RMS_EPS)).T
            o_ref[0, :, sl] = (o * g_ref[:, sl]).astype(o_ref.dtype)


def _prompt_attention(q, k, v, layer, c, g, blk):
    n, s, _ = q.shape
    nb = s // blk
    kv_map = lambda b, qi, ki: (layer, b, jnp.minimum(ki, qi), 0)
    return pl.pallas_call(
        _prompt_attn_kernel, grid=(n, nb, nb),
        in_specs=[pl.BlockSpec((1, blk, ATT_DIM), lambda b, qi, ki: (b, qi, 0)),
                  pl.BlockSpec((1, 1, blk, ATT_DIM), kv_map),
                  pl.BlockSpec((1, 1, blk, ATT_DIM), kv_map),
                  pl.BlockSpec((1, blk, LANES), lambda b, qi, ki: (b, jnp.minimum(ki, qi), 0)),
                  pl.BlockSpec((1, ATT_DIM), lambda b, qi, ki: (0, 0))],
        out_specs=pl.BlockSpec((1, blk, ATT_DIM), lambda b, qi, ki: (b, qi, 0)),
        out_shape=jax.ShapeDtypeStruct((n, s, ATT_DIM), BF16),
        scratch_shapes=[pltpu.VMEM((ATT_HEADS, 1, blk), F32),
                        pltpu.VMEM((ATT_HEADS, 1, blk), F32),
                        pltpu.VMEM((ATT_HEADS, HEAD_DIM, blk), F32)],
        compiler_params=_cp(3), name="prompt_attention")(q, k, v, c, g.reshape(1, ATT_DIM))


def _decode_attn_kernel(pt_ref, q_ref, kn_ref, vn_ref, lfn_ref, g_ref, *rest):
    del pt_ref
    npg = PAGES_PER_STEP
    k_refs, v_refs, lf_refs = rest[:npg], rest[npg:2 * npg], rest[2 * npg:3 * npg]
    o_ref, m_scr, l_scr, acc_scr, carry_scr, lf_scr = rest[3 * npg:]
    c = pl.program_id(1)
    width = ATT_HEADS * HEAD_DIM

    @pl.when(c == 0)
    def _():
        m_scr[...] = jnp.full(m_scr.shape, NEG_BIG, F32)
        l_scr[...] = jnp.zeros_like(l_scr)
        acc_scr[...] = jnp.zeros_like(acc_scr)
        carry_scr[...] = lfn_ref[0]

    lane = lax.broadcasted_iota(jnp.int32, (npg, width), 1)
    for p in range(npg):
        lf_scr[p:p + 1, :] = lf_refs[p][0, 0]
    lf = lf_scr[...]
    y = lf
    sh = ATT_HEADS
    while sh < width:
        y = y + jnp.where(lane < width - sh, pltpu.roll(y, width - sh, axis=1), 0.0)
        sh *= 2
    excl = y - lf
    z = jnp.where(lane < ATT_HEADS, y, 0.0)
    sh = ATT_HEADS
    while sh < width:
        z = z + pltpu.roll(z, sh, axis=1)
        sh *= 2
    carry = carry_scr[...]
    bias = [None] * npg
    for p in reversed(range(npg)):
        bias[p] = carry + excl[p:p + 1, :]
        carry = carry + z[p:p + 1, :]
    carry_scr[...] = carry

    q = q_ref[0]
    hrow = lax.broadcasted_iota(jnp.int32, (ATT_HEADS, width), 0)
    hlane = lax.broadcasted_iota(jnp.int32, (ATT_HEADS, width), 1)
    own = jnp.bitwise_and(hlane, ATT_HEADS - 1) == hrow
    s_all = []
    for p in range(npg):
        kp = k_refs[p][0, 0].astype(BF16)
        s = lax.dot_general(q, kp, (((1,), (1,)), ((), ())), preferred_element_type=F32)
        s_all.append(jnp.where(own, s * ATT_SCALE + bias[p], NEG_BIG))
    m_prev = m_scr[...]
    m_new = m_prev
    for s in s_all:
        m_new = jnp.maximum(m_new, jnp.max(s, axis=1, keepdims=True))
    alpha = jnp.exp(m_prev - m_new)
    l_new = alpha * l_scr[...]
    acc = alpha * acc_scr[...]
    for p in range(npg):
        pr = jnp.exp(s_all[p] - m_new)
        l_new = l_new + jnp.sum(pr, axis=1, keepdims=True)
        acc = acc + _dot(pr.astype(BF16), v_refs[p][0, 0].astype(BF16))
    m_scr[...] = m_new
    l_scr[...] = l_new
    acc_scr[...] = acc

    @pl.when(c == pl.num_programs(1) - 1)
    def _():
        kn = kn_ref[0].astype(BF16).astype(F32)
        vn = vn_ref[0].astype(BF16).astype(F32)
        s_new = jnp.sum(q.astype(F32) * kn, axis=1, keepdims=True) * ATT_SCALE
        m_fin = jnp.maximum(m_new, s_new)
        a = jnp.exp(m_new - m_fin)
        pn = jnp.exp(s_new - m_fin)
        o = (a * acc + pn * vn) / (a * l_new + pn)
        ms = jnp.mean(o * o, axis=-1, keepdims=True)
        o_ref[0] = (o * lax.rsqrt(ms + RMS_EPS) * g_ref[...]).astype(o_ref.dtype)


def _decode_attention(layer, page_table, q, kn, vn, lfn, g, cache_k, cache_v, cache_logf):
    n, n_pages = page_table.shape
    depth, n_pool, page, heads, hd = cache_k.shape
    width = page * heads
    ck = cache_k.reshape(depth, n_pool, width, hd)
    cv = cache_v.reshape(depth, n_pool, width, hd)
    cl = cache_logf.reshape(depth, n_pool, 1, width)
    n_chunks = n_pages // PAGES_PER_STEP

    def page_map(p):
        return lambda b, c, pt: (layer, pt[b, (n_chunks - 1 - c) * PAGES_PER_STEP + p], 0, 0)

    per_seq = lambda b, c, pt: (b, 0, 0)
    in_specs = [pl.BlockSpec((1, heads, hd), per_seq), pl.BlockSpec((1, heads, hd), per_seq),
                pl.BlockSpec((1, heads, hd), per_seq), pl.BlockSpec((1, 1, width), per_seq),
                pl.BlockSpec((heads, hd), lambda b, c, pt: (0, 0))]
    in_specs += [pl.BlockSpec((1, 1, width, hd), page_map(p)) for p in range(PAGES_PER_STEP)]
    in_specs += [pl.BlockSpec((1, 1, width, hd), page_map(p)) for p in range(PAGES_PER_STEP)]
    in_specs += [pl.BlockSpec((1, 1, 1, width), page_map(p)) for p in range(PAGES_PER_STEP)]
    grid_spec = pltpu.PrefetchScalarGridSpec(
        num_scalar_prefetch=1, grid=(n, n_chunks), in_specs=in_specs,
        out_specs=pl.BlockSpec((1, heads, hd), per_seq),
        scratch_shapes=[pltpu.VMEM((heads, 1), F32), pltpu.VMEM((heads, 1), F32),
                        pltpu.VMEM((heads, hd), F32), pltpu.VMEM((1, width), F32),
                        pltpu.VMEM((PAGES_PER_STEP, width), F32)])
    return pl.pallas_call(
        _decode_attn_kernel, grid_spec=grid_spec,
        out_shape=jax.ShapeDtypeStruct((n, heads, hd), BF16),
        compiler_params=_cp(2), name="decode_attention")(
            page_table, q, kn, vn, lfn, g.reshape(heads, hd),
            *([ck] * PAGES_PER_STEP), *([cv] * PAGES_PER_STEP), *([cl] * PAGES_PER_STEP))


def _pad_lanes(w):
    return jnp.pad(w, ((0, 0), (0, LANES - w.shape[1])))


def kernel(x_prompt, x_sample, cache_k, cache_v, cache_logf, state_conv, page_table, norm_mix_g,
           w_in, b_forget, conv_w, g_att_out, g_conv_out, w_out, norm_ffn_g, dense_w_gate,
           dense_w_up, dense_w_down, moe_router, moe_w_gate, moe_w_up, moe_w_down, final_norm_g):
    n, seq, d = x_prompt.shape
    n_dec, dec_seq, _ = x_sample.shape
    assert dec_seq == 1 and d == D_MODEL
    depth = w_in.shape[0]
    mp = n * seq
    tm_p, tm_s = 512, n_dec

    xp = x_prompt.reshape(mp, d)
    xs = x_sample.reshape(n_dec, d)
    xnp = _rmsnorm(xp, norm_mix_g[0], tm_p)
    xns = _rmsnorm(xs, norm_mix_g[0], tm_s)

    outs = {key: [] for key in ("fp", "cp", "ks", "vs", "fs", "cs")}
    yp = ys = k_all = v_all = None
    for l in range(depth):
        wl = w_in[l]
        o = 0
        wq = wl[:, o:o + ATT_DIM].astype(BF16); o += ATT_DIM
        wk = wl[:, o:o + ATT_DIM].astype(BF16); o += ATT_DIM
        wv = wl[:, o:o + ATT_DIM].astype(BF16); o += ATT_DIM
        wf = _pad_lanes(wl[:, o:o + ATT_HEADS]).astype(BF16); o += ATT_HEADS
        wb = wl[:, o:o + CONV_DIM].astype(BF16); o += CONV_DIM
        wc = wl[:, o:o + CONV_DIM].astype(BF16); o += CONV_DIM
        wh = wl[:, o:o + CONV_DIM].astype(BF16)
        bf = _pad_lanes(b_forget[l].reshape(1, ATT_HEADS))
        wo_a = w_out[l][:ATT_DIM].astype(BF16)
        wo_c = w_out[l][ATT_DIM:].astype(BF16)

        q_p = _mm(xnp, wq, BF16, 1024, 1024, "q_proj")
        k_all = _mm_layer(xnp, wk, k_all, l, depth, 1024, 1024, "k_proj")
        v_all = _mm_layer(xnp, wv, v_all, l, depth, 1024, 1024, "v_proj")
        lf_p, c_p = _forget(xnp, wf, bf, seq, True)
        conv_p, tails = _conv_branch(xnp, wb, wc, wh, conv_w[l], g_conv_out[l], seq, 512, 1024)
        outs["fp"].append(lf_p[:, :ATT_HEADS].reshape(n, seq, ATT_HEADS))
        tiles_per_seq = seq // 512
        outs["cp"].append(tails[tiles_per_seq - 1::tiles_per_seq])

        q_s = _mm(xns, wq, BF16, tm_s, 512, "q_proj_s")
        k_s = _mm(xns, wk, F32, tm_s, 512, "k_proj_s")
        v_s = _mm(xns, wv, F32, tm_s, 512, "v_proj_s")
        lf_s = _forget(xns, wf, bf, n_dec, False)[0]
        st = state_conv[l]
        conv_s, u_s = _conv_step(xns, wb, wc, wh, conv_w[l], g_conv_out[l], st[:, 0], st[:, 1], 512)
        lfn = jnp.tile(lf_s[:, :ATT_HEADS], (1, cache_k.shape[2])).reshape(n_dec, 1, -1)

        att_p, att_s = _attention(
            l, q_p.reshape(n, seq, ATT_DIM), k_all.reshape(depth, n, seq, ATT_DIM),
            v_all.reshape(depth, n, seq, ATT_DIM), c_p.reshape(n, seq, LANES), g_att_out[l],
            page_table, q_s.reshape(n_dec, ATT_HEADS, HEAD_DIM),
            k_s.reshape(n_dec, ATT_HEADS, HEAD_DIM), v_s.reshape(n_dec, ATT_HEADS, HEAD_DIM), lfn,
            cache_k, cache_v, cache_logf, 512, 512)
        xp, hnp = _outproj(att_p.reshape(mp, ATT_DIM), conv_p, wo_a, wo_c, xp, norm_ffn_g[l], tm_p)
        xs, hns = _outproj(att_s.reshape(n_dec, ATT_DIM), conv_s, wo_a, wo_c, xs, norm_ffn_g[l], tm_s)
        outs["ks"].append(k_s.reshape(n_dec, 1, ATT_HEADS, HEAD_DIM))
        outs["vs"].append(v_s.reshape(n_dec, 1, ATT_HEADS, HEAD_DIM))
        outs["fs"].append(lf_s[:, :ATT_HEADS].reshape(n_dec, 1, ATT_HEADS))
        outs["cs"].append(jnp.stack([st[:, 1], u_s], axis=1))

        last = l == depth - 1
        next_g = final_norm_g if last else norm_mix_g[l + 1]
        next_dtype = F32 if last else BF16
        i = l // 2
        if l % 2 == 0:
            wg = dense_w_gate[i].astype(BF16)
            wu = dense_w_up[i].astype(BF16)
            wd = dense_w_down[i].astype(BF16)
            hp = _swiglu(hnp, wg, wu, 1024, 512)
            xp, nxp = _down(hp, wd, xp, next_g, tm_p, 1408, not last, next_dtype)
            hs = _swiglu(hns, wg, wu, tm_s, 512)
            xs, nxs = _down(hs, wd, xs, next_g, tm_s, 1408, not last, next_dtype)
        else:
            xp, nxp, xs, nxs = _moe_ffn(
                xp, hnp, xs, hns, norm_ffn_g[l], moe_router[i], moe_w_gate[i].astype(BF16),
                moe_w_up[i].astype(BF16), moe_w_down[i].astype(BF16), next_g, not last, next_dtype)
        if last:
            yp, ys = nxp, nxs
        else:
            xnp, xns = nxp, nxs

    return (yp.reshape(n, seq, d), ys.reshape(n_dec, 1, d),
            k_all.reshape(depth, n, seq, ATT_HEADS, HEAD_DIM),
            v_all.reshape(depth, n, seq, ATT_HEADS, HEAD_DIM),
            jnp.stack(outs["fp"]), jnp.stack(outs["cp"]),
            jnp.stack(outs["ks"]), jnp.stack(outs["vs"]), jnp.stack(outs["fs"]), jnp.stack(outs["cs"]))
```

```python
import functools

import jax
import jax.numpy as jnp
from jax import lax
from jax.experimental import pallas as pl
from jax.experimental.pallas import tpu as pltpu

D_MODEL = 2048
HEAD_DIM = 128
ATT_HEADS = 8
ATT_DIM = ATT_HEADS * HEAD_DIM
CONV_DIM = D_MODEL - ATT_DIM
CONV_W = 3
N_EXPERTS = 8
TOP_K = 2
ATT_SCALE = HEAD_DIM ** -0.5
RMS_EPS = 1e-6
LANES = 128
NEG_BIG = -1e30
PAGES_PER_STEP = 16
Q_STRIP = 128
VMEM_LIMIT = 56 * 1024 * 1024

F32 = jnp.float32
BF16 = jnp.bfloat16


def _cp(n_axes):
    return pltpu.CompilerParams(dimension_semantics=("arbitrary",) * n_axes,
                                vmem_limit_bytes=VMEM_LIMIT)


def _rms(x, g):
    return x * lax.rsqrt(jnp.mean(x * x, axis=-1, keepdims=True) + RMS_EPS) * g


def _dot(a, b):
    return jnp.dot(a, b, preferred_element_type=F32)


def _rmsnorm_kernel(x_ref, g_ref, o_ref):
    o_ref[...] = _rms(x_ref[...], g_ref[...]).astype(o_ref.dtype)


def _rmsnorm(x, g, tm, out_dtype=BF16):
    m, d = x.shape
    return pl.pallas_call(
        _rmsnorm_kernel, grid=(m // tm,),
        in_specs=[pl.BlockSpec((tm, d), lambda i: (i, 0)),
                  pl.BlockSpec((1, d), lambda i: (0, 0))],
        out_specs=pl.BlockSpec((tm, d), lambda i: (i, 0)),
        out_shape=jax.ShapeDtypeStruct((m, d), out_dtype),
        compiler_params=_cp(1), name="rmsnorm")(x, g.reshape(1, d))


def _mm_kernel(x_ref, w_ref, *rest):
    o_ref = rest[-1]
    o_ref[...] = _dot(x_ref[...], w_ref[...]).astype(o_ref.dtype).reshape(o_ref.shape)


def _mm_layer(x, w, stacked, layer, depth, tm, tn, name):
    m, k = x.shape
    n = w.shape[1]
    in_specs = [pl.BlockSpec((tm, k), lambda i, j: (i, 0)),
                pl.BlockSpec((k, tn), lambda i, j: (0, j))]
    args, aliases = [x, w], {}
    if stacked is not None:
        in_specs.append(pl.BlockSpec(memory_space=pl.ANY))
        args.append(stacked)
        aliases = {2: 0}
    return pl.pallas_call(
        _mm_kernel, grid=(m // tm, n // tn), in_specs=in_specs,
        out_specs=pl.BlockSpec((1, tm, tn), lambda i, j: (layer, i, j)),
        out_shape=jax.ShapeDtypeStruct((depth, m, n), F32), input_output_aliases=aliases,
        compiler_params=_cp(2), name=name)(*args)


def _mm(x, w, out_dtype, tm, tn, name):
    m, k = x.shape
    n = w.shape[1]
    return pl.pallas_call(
        _mm_kernel, grid=(m // tm, n // tn),
        in_specs=[pl.BlockSpec((tm, k), lambda i, j: (i, 0)),
                  pl.BlockSpec((k, tn), lambda i, j: (0, j))],
        out_specs=pl.BlockSpec((tm, tn), lambda i, j: (i, j)),
        out_shape=jax.ShapeDtypeStruct((m, n), out_dtype),
        compiler_params=_cp(2), name=name)(x, w)


def _swiglu_kernel(x_ref, wg_ref, wu_ref, o_ref):
    x = x_ref[...]
    a = _dot(x, wg_ref[...])
    b = _dot(x, wu_ref[...])
    o_ref[...] = (a * (1.0 / (1.0 + jnp.exp(-a))) * b).astype(o_ref.dtype)


def _swiglu(x, wg, wu, tm, tn):
    m, k = x.shape
    n = wg.shape[1]
    return pl.pallas_call(
        _swiglu_kernel, grid=(m // tm, n // tn),
        in_specs=[pl.BlockSpec((tm, k), lambda i, j: (i, 0)),
                  pl.BlockSpec((k, tn), lambda i, j: (0, j)),
                  pl.BlockSpec((k, tn), lambda i, j: (0, j))],
        out_specs=pl.BlockSpec((tm, tn), lambda i, j: (i, j)),
        out_shape=jax.ShapeDtypeStruct((m, n), BF16),
        compiler_params=_cp(2), name="swiglu_up")(x, wg, wu)


def _down_kernel(h_ref, w_ref, res_ref, g_ref, *outs, emit_x):
    xo = res_ref[...] + _dot(h_ref[...], w_ref[...])
    if emit_x:
        outs[0][...] = xo
    outs[-1][...] = _rms(xo, g_ref[...]).astype(outs[-1].dtype)


def _down(h, w, res, norm_g, tm, emit_x, norm_dtype):
    m, kdim = h.shape
    d = w.shape[1]
    row = lambda i: (i, 0)
    fixed = lambda i: (0, 0)
    out_specs = [pl.BlockSpec((tm, d), row)]
    out_shape = [jax.ShapeDtypeStruct((m, d), norm_dtype)]
    if emit_x:
        out_specs.insert(0, pl.BlockSpec((tm, d), row))
        out_shape.insert(0, jax.ShapeDtypeStruct((m, d), F32))
    outs = pl.pallas_call(
        functools.partial(_down_kernel, emit_x=emit_x), grid=(m // tm,),
        in_specs=[pl.BlockSpec((tm, kdim), row),
                  pl.BlockSpec((kdim, d), fixed, pipeline_mode=pl.Buffered(1)),
                  pl.BlockSpec((tm, d), row), pl.BlockSpec((1, d), fixed)],
        out_specs=out_specs, out_shape=out_shape,
        compiler_params=_cp(1), name="ffn_down")(h, w, res, norm_g.reshape(1, d))
    return (outs[0], outs[1]) if emit_x else (None, outs[0])


def _outproj_kernel(a_ref, c_ref, wa_ref, wc_ref, res_ref, g_ref, xo_ref, xn_ref):
    xo = res_ref[...] + (_dot(a_ref[...], wa_ref[...]) + _dot(c_ref[...], wc_ref[...]))
    xo_ref[...] = xo
    xn_ref[...] = _rms(xo, g_ref[...]).astype(xn_ref.dtype)


def _outproj(a, c, wa, wc, res, g, tm):
    m, d = res.shape
    ka, kc = a.shape[1], c.shape[1]
    row = lambda i: (i, 0)
    fixed = lambda i: (0, 0)
    return pl.pallas_call(
        _outproj_kernel, grid=(m // tm,),
        in_specs=[pl.BlockSpec((tm, ka), row), pl.BlockSpec((tm, kc), row),
                  pl.BlockSpec((ka, d), fixed), pl.BlockSpec((kc, d), fixed),
                  pl.BlockSpec((tm, d), row), pl.BlockSpec((1, d), fixed)],
        out_specs=[pl.BlockSpec((tm, d), row), pl.BlockSpec((tm, d), row)],
        out_shape=[jax.ShapeDtypeStruct((m, d), F32), jax.ShapeDtypeStruct((m, d), BF16)],
        compiler_params=_cp(1), name="out_proj")(a, c, wa, wc, res, g.reshape(1, d))


SEL_E1, SEL_E2, SEL_G1, SEL_G2, SEL_R1, SEL_R2 = range(6)


def _router_kernel(x_ref, r_ref, cin_ref, sel_ref, cnt_ref, carry_scr, *, n_valid):
    i = pl.program_id(0)
    tm = x_ref.shape[0]

    @pl.when(i == 0)
    def _():
        carry_scr[...] = cin_ref[...]

    logits = _dot(x_ref[...], r_ref[...])
    lane = lax.broadcasted_iota(jnp.int32, logits.shape, 1).astype(F32)
    lg = jnp.where(lane < N_EXPERTS, logits, -jnp.inf)
    m1 = jnp.max(lg, axis=1, keepdims=True)
    i1 = jnp.min(jnp.where(lg == m1, lane, float(LANES)), axis=1, keepdims=True)
    lg2 = jnp.where(lane == i1, -jnp.inf, lg)
    m2 = jnp.max(lg2, axis=1, keepdims=True)
    i2 = jnp.min(jnp.where(lg2 == m2, lane, float(LANES)), axis=1, keepdims=True)
    e2 = jnp.exp(m2 - m1)
    den = 1.0 + e2
    row = lax.broadcasted_iota(jnp.int32, logits.shape, 0) + i * tm
    hit = jnp.where(((lane == i1) | (lane == i2)) & (row < n_valid), 1.0, 0.0)
    r = lax.broadcasted_iota(jnp.int32, (tm, tm), 0)
    c = lax.broadcasted_iota(jnp.int32, (tm, tm), 1)
    before = (c < r).astype(BF16)
    rank = _dot(before, hit.astype(BF16)) + carry_scr[...]
    carry_scr[...] = carry_scr[...] + jnp.sum(hit, axis=0, keepdims=True)
    cnt_ref[...] = carry_scr[...]
    rank1 = jnp.sum(jnp.where(lane == i1, rank, 0.0), axis=1, keepdims=True)
    rank2 = jnp.sum(jnp.where(lane == i2, rank, 0.0), axis=1, keepdims=True)
    sel = jnp.zeros_like(logits)
    for idx, val in ((SEL_E1, i1), (SEL_E2, i2), (SEL_G1, 1.0 / den), (SEL_G2, e2 / den),
                     (SEL_R1, rank1), (SEL_R2, rank2)):
        sel = jnp.where(lane == idx, val, sel)
    sel_ref[...] = sel


def _router(hn, r_pad, count_in, tm, n_valid):
    m, d = hn.shape
    fixed = lambda i: (0, 0)
    return pl.pallas_call(
        functools.partial(_router_kernel, n_valid=n_valid), grid=(m // tm,),
        in_specs=[pl.BlockSpec((tm, d), lambda i: (i, 0)), pl.BlockSpec((d, LANES), fixed),
                  pl.BlockSpec((1, LANES), fixed)],
        out_specs=[pl.BlockSpec((tm, LANES), lambda i: (i, 0)), pl.BlockSpec((1, LANES), fixed)],
        out_shape=[jax.ShapeDtypeStruct((m, LANES), F32), jax.ShapeDtypeStruct((1, LANES), F32)],
        scratch_shapes=[pltpu.VMEM((1, LANES), F32)],
        compiler_params=_cp(1), name="router")(hn, r_pad, count_in)


def _dispatch_kernel(d1_ref, d2_ref, ps_ref, pc_ref, x_ref, *rest, first):
    xs_ref, zero_scr, sem = rest[-3:]
    tt = x_ref.shape[0]
    base = pl.program_id(0) * tt

    def row_copy(src_row, dst_row):
        return pltpu.make_async_copy(src_row, xs_ref.at[pl.ds(dst_row, 1), :], sem)

    def issue(r, carry):
        src = x_ref.at[pl.ds(r, 1), :]
        row_copy(src, d1_ref[base + r]).start()
        row_copy(src, d2_ref[base + r]).start(priority=1)
        return carry

    def drain(r, carry):
        row_copy(x_ref.at[pl.ds(0, 1), :], 0).wait()
        row_copy(x_ref.at[pl.ds(0, 1), :], 0).wait()
        return carry

    lax.fori_loop(0, tt, issue, 0, unroll=8)
    lax.fori_loop(0, tt, drain, 0, unroll=8)

    if first:
        @pl.when(pl.program_id(0) == 0)
        def _():
            zero_scr[...] = jnp.zeros_like(zero_scr)
            zrow = zero_scr.at[pl.ds(0, 1), :]
            for e in range(N_EXPERTS):
                def fill(r, carry, e=e):
                    row_copy(zrow, ps_ref[e] + r).start()
                    return carry

                def fill_done(r, carry):
                    row_copy(zrow, 0).wait()
                    return carry

                lax.fori_loop(0, pc_ref[e], fill, 0)
                lax.fori_loop(0, pc_ref[e], fill_done, 0)


def _dispatch(d1, d2, pad_start, pad_count, x, xs, tt, n_rows):
    m, d = x.shape
    first = xs is None
    in_specs = [pl.BlockSpec((tt, d), lambda i, *_: (i, 0))]
    args = [d1, d2, pad_start, pad_count, x]
    aliases = {}
    if not first:
        in_specs.append(pl.BlockSpec(memory_space=pl.ANY))
        args.append(xs)
        aliases = {5: 0}
    grid_spec = pltpu.PrefetchScalarGridSpec(
        num_scalar_prefetch=4, grid=(m // tt,), in_specs=in_specs,
        out_specs=pl.BlockSpec(memory_space=pl.ANY),
        scratch_shapes=[pltpu.VMEM((8, d), F32), pltpu.SemaphoreType.DMA(())])
    return pl.pallas_call(
        functools.partial(_dispatch_kernel, first=first), grid_spec=grid_spec,
        out_shape=jax.ShapeDtypeStruct((n_rows, d), F32), input_output_aliases=aliases,
        compiler_params=_cp(1), name="moe_dispatch")(*args)


def _moe_up_kernel(te_ref, na_ref, x_ref, g_ref, wg_ref, wu_ref, o_ref):
    del te_ref

    @pl.when(pl.program_id(1) < na_ref[0])
    def _():
        xn = _rms(x_ref[...], g_ref[...]).astype(BF16)
        a = _dot(xn, wg_ref[0])
        b = _dot(xn, wu_ref[0])
        o_ref[...] = (a * (1.0 / (1.0 + jnp.exp(-a))) * b).astype(o_ref.dtype)


def _moe_up(tile_expert, n_active, xs, g, wg, wu, tm, tn):
    r, d = xs.shape
    f = wg.shape[2]
    tile = lambda t, na: jnp.minimum(t, na[0] - 1)
    w_map = lambda j, t, te, na: (te[tile(t, na)], 0, j)
    grid_spec = pltpu.PrefetchScalarGridSpec(
        num_scalar_prefetch=2, grid=(f // tn, r // tm),
        in_specs=[pl.BlockSpec((tm, d), lambda j, t, te, na: (tile(t, na), 0)),
                  pl.BlockSpec((1, d), lambda j, t, te, na: (0, 0)),
                  pl.BlockSpec((1, d, tn), w_map), pl.BlockSpec((1, d, tn), w_map)],
        out_specs=pl.BlockSpec((tm, tn), lambda j, t, te, na: (tile(t, na), j)))
    return pl.pallas_call(
        _moe_up_kernel, grid_spec=grid_spec, out_shape=jax.ShapeDtypeStruct((r, f), BF16),
        compiler_params=_cp(2), name="moe_up")(tile_expert, n_active, xs, g.reshape(1, d), wg, wu)


def _moe_down_kernel(te_ref, na_ref, h_ref, w_ref, o_ref):
    del te_ref

    @pl.when(pl.program_id(0) < na_ref[0])
    def _():
        o_ref[...] = _dot(h_ref[...], w_ref[0])


def _moe_down(tile_expert, n_active, h, wd, tm):
    r, f = h.shape
    d = wd.shape[2]
    tile = lambda t, na: jnp.minimum(t, na[0] - 1)
    grid_spec = pltpu.PrefetchScalarGridSpec(
        num_scalar_prefetch=2, grid=(r // tm,),
        in_specs=[pl.BlockSpec((tm, f), lambda t, te, na: (tile(t, na), 0)),
                  pl.BlockSpec((1, f, d), lambda t, te, na: (te[tile(t, na)], 0, 0))],
        out_specs=pl.BlockSpec((tm, d), lambda t, te, na: (tile(t, na), 0)))
    return pl.pallas_call(
        _moe_down_kernel, grid_spec=grid_spec, out_shape=jax.ShapeDtypeStruct((r, d), F32),
        compiler_params=_cp(1), name="moe_down")(tile_expert, n_active, h, wd)


def _combine_kernel(d1_ref, d2_ref, ys_ref, res_ref, sel_ref, g_ref, *rest, emit_x):
    y1_scr, y2_scr, sem = rest[-3:]
    outs = rest[:-3]
    tt = res_ref.shape[0]
    base = pl.program_id(0) * tt

    def row_copy(src_row, dst):
        return pltpu.make_async_copy(ys_ref.at[pl.ds(src_row, 1), :], dst, sem)

    def issue(r, carry):
        row_copy(d1_ref[base + r], y1_scr.at[pl.ds(r, 1), :]).start()
        row_copy(d2_ref[base + r], y2_scr.at[pl.ds(r, 1), :]).start(priority=1)
        return carry

    def drain(r, carry):
        row_copy(0, y1_scr.at[pl.ds(0, 1), :]).wait()
        row_copy(0, y2_scr.at[pl.ds(0, 1), :]).wait()
        return carry

    lax.fori_loop(0, tt, issue, 0, unroll=8)
    lax.fori_loop(0, tt, drain, 0, unroll=8)
    sel = sel_ref[...]
    g1 = sel[:, SEL_G1:SEL_G1 + 1]
    g2 = sel[:, SEL_G2:SEL_G2 + 1]
    xo = res_ref[...] + (g1 * y1_scr[...] + g2 * y2_scr[...])
    if emit_x:
        outs[0][...] = xo
    outs[-1][...] = _rms(xo, g_ref[...]).astype(outs[-1].dtype)


def _combine(d1, d2, ys, res, sel, g, tt, emit_x, norm_dtype):
    m, d = res.shape
    row = lambda i, *_: (i, 0)
    out_specs = [pl.BlockSpec((tt, d), row)]
    out_shape = [jax.ShapeDtypeStruct((m, d), norm_dtype)]
    if emit_x:
        out_specs.insert(0, pl.BlockSpec((tt, d), row))
        out_shape.insert(0, jax.ShapeDtypeStruct((m, d), F32))
    grid_spec = pltpu.PrefetchScalarGridSpec(
        num_scalar_prefetch=2, grid=(m // tt,),
        in_specs=[pl.BlockSpec(memory_space=pl.ANY), pl.BlockSpec((tt, d), row),
                  pl.BlockSpec((tt, LANES), row), pl.BlockSpec((1, d), lambda i, *_: (0, 0))],
        out_specs=out_specs,
        scratch_shapes=[pltpu.VMEM((tt, d), F32), pltpu.VMEM((tt, d), F32),
                        pltpu.SemaphoreType.DMA(())])
    outs = pl.pallas_call(
        functools.partial(_combine_kernel, emit_x=emit_x), grid_spec=grid_spec,
        out_shape=out_shape, compiler_params=_cp(1), name="moe_combine")(
            d1, d2, ys, res, sel, g.reshape(1, d))
    return (outs[0], outs[1]) if emit_x else (None, outs[0])


def _moe_ffn(x_p, hn_p, x_s, hn_s, ffn_g, router, wg, wu, wd, next_g, emit_x, norm_dtype,
             tm=512, tn=1408, t_route=512, t_move=512, t_comb=256):
    mp, d = x_p.shape
    ms = x_s.shape[0]
    n_tiles = (TOP_K * (mp + ms)) // tm + N_EXPERTS
    r_pad = _pad_lanes(router).astype(BF16)
    ms_pad = -(-ms // LANES) * LANES
    hn_s_pad = jnp.pad(hn_s, ((0, ms_pad - ms), (0, 0)))
    sel_p, cnt_p = _router(hn_p, r_pad, jnp.zeros((1, LANES), F32), t_route, mp)
    sel_s, cnt = _router(hn_s_pad, r_pad, cnt_p, ms_pad, ms)

    count = cnt[0, :N_EXPERTS].astype(jnp.int32)
    padded = (count + tm - 1) // tm * tm
    g_end = jnp.cumsum(padded)
    g_start = g_end - padded
    tile_expert = jnp.minimum(jnp.searchsorted(g_end // tm, jnp.arange(n_tiles), side="right"),
                              N_EXPERTS - 1).astype(jnp.int32)
    n_active = (g_end[-1:] // tm).astype(jnp.int32)

    def dests(sel, rows):
        e1, e2 = sel[:rows, SEL_E1].astype(jnp.int32), sel[:rows, SEL_E2].astype(jnp.int32)
        return (g_start[e1] + sel[:rows, SEL_R1].astype(jnp.int32),
                g_start[e2] + sel[:rows, SEL_R2].astype(jnp.int32))

    d1_p, d2_p = dests(sel_p, mp)
    d1_s, d2_s = dests(sel_s, ms)
    pad_start = g_start + count
    pad_count = padded - count
    xs = _dispatch(d1_p, d2_p, pad_start, pad_count, x_p, None, t_move, n_tiles * tm)
    xs = _dispatch(d1_s, d2_s, pad_start, pad_count, x_s, xs, ms, n_tiles * tm)
    h = _moe_up(tile_expert, n_active, xs, ffn_g, wg, wu, tm, tn)
    ys = _moe_down(tile_expert, n_active, h, wd, tm)
    xo_p, xn_p = _combine(d1_p, d2_p, ys, x_p, sel_p, next_g, t_comb, emit_x, norm_dtype)
    xo_s, xn_s = _combine(d1_s, d2_s, ys, x_s, sel_s, next_g, ms, emit_x, norm_dtype)
    return xo_p, xn_p, xo_s, xn_s


def _forget_kernel(x_ref, w_ref, b_ref, lf_ref, *c_ref, cumsum):
    z = _dot(x_ref[...], w_ref[...]) + b_ref[...]
    lf = jnp.minimum(z, 0.0) - jnp.log1p(jnp.exp(-jnp.abs(z)))
    lf_ref[...] = lf
    if cumsum:
        r = lax.broadcasted_iota(jnp.int32, (LANES, LANES), 0)
        c = lax.broadcasted_iota(jnp.int32, (LANES, LANES), 1)
        tri = (c <= r).astype(F32)
        carry = jnp.zeros((1, LANES), F32)
        for blk in range(lf.shape[0] // LANES):
            rows = slice(blk * LANES, (blk + 1) * LANES)
            cb = jnp.dot(tri, lf[rows, :], preferred_element_type=F32,
                         precision=lax.Precision.HIGHEST) + carry
            carry = cb[LANES - 1:LANES, :]
            c_ref[0][rows, :] = cb


def _forget(xn, wf_pad, bf_pad, rows, cumsum):
    m, d = xn.shape
    n = m // rows
    out_specs = [pl.BlockSpec((rows, LANES), lambda b: (b, 0))]
    out_shape = [jax.ShapeDtypeStruct((m, LANES), F32)]
    if cumsum:
        out_specs.append(pl.BlockSpec((rows, LANES), lambda b: (b, 0)))
        out_shape.append(jax.ShapeDtypeStruct((m, LANES), F32))
    return pl.pallas_call(
        functools.partial(_forget_kernel, cumsum=cumsum), grid=(n,),
        in_specs=[pl.BlockSpec((rows, d), lambda b: (b, 0)),
                  pl.BlockSpec((d, LANES), lambda b: (0, 0)),
                  pl.BlockSpec((1, LANES), lambda b: (0, 0))],
        out_specs=out_specs, out_shape=out_shape,
        compiler_params=_cp(1), name="forget_gate")(xn, wf_pad, bf_pad)


def _group_norm_store(yc, g_ref, o_ref):
    for gi in range(yc.shape[1] // HEAD_DIM):
        sl = slice(gi * HEAD_DIM, (gi + 1) * HEAD_DIM)
        yg = yc[:, sl]
        ms = jnp.mean(yg * yg, axis=-1, keepdims=True)
        o_ref[:, sl] = (yg * lax.rsqrt(ms + RMS_EPS) * g_ref[:, sl]).astype(o_ref.dtype)


def _conv_kernel(x_ref, wb_ref, wc_ref, wh_ref, cw_ref, g_ref, o_ref, tail_ref, halo_scr,
                 *, tiles_per_seq):
    i, j = pl.program_id(0), pl.program_id(1)
    x = x_ref[...]
    gate_b = _dot(x, wb_ref[...])
    u = _dot(x, wc_ref[...]) * _dot(x, wh_ref[...])
    tm = u.shape[0]

    @pl.when(i % tiles_per_seq == 0)
    def _():
        halo_scr[j] = jnp.zeros(halo_scr.shape[1:], F32)

    prev = halo_scr[j]
    row = lax.broadcasted_iota(jnp.int32, u.shape, 0)
    u1 = jnp.where(row == 0, prev[1:2, :], pltpu.roll(u, 1, axis=0))
    u2 = jnp.where(row == 0, prev[0:1, :],
                   jnp.where(row == 1, prev[1:2, :], pltpu.roll(u, 2, axis=0)))
    cw = cw_ref[...]
    y = u2 * cw[0:1, :] + u1 * cw[1:2, :] + u * cw[2:3, :]
    _group_norm_store(gate_b * y, g_ref, o_ref)
    tail = u[tm - (CONV_W - 1):tm, :]
    halo_scr[j] = tail
    tail_ref[0] = tail


def _conv_branch(xn, wb, wc, wh, cw, g, seq, tm, tc):
    m, d = xn.shape
    n_i, n_j = m // tm, CONV_DIM // tc
    col = lambda i, j: (0, j)
    return pl.pallas_call(
        functools.partial(_conv_kernel, tiles_per_seq=seq // tm), grid=(n_i, n_j),
        in_specs=[pl.BlockSpec((tm, d), lambda i, j: (i, 0)),
                  pl.BlockSpec((d, tc), col), pl.BlockSpec((d, tc), col),
                  pl.BlockSpec((d, tc), col), pl.BlockSpec((CONV_W, tc), col),
                  pl.BlockSpec((1, tc), col)],
        out_specs=[pl.BlockSpec((tm, tc), lambda i, j: (i, j)),
                   pl.BlockSpec((1, CONV_W - 1, tc), lambda i, j: (i, 0, j))],
        out_shape=[jax.ShapeDtypeStruct((m, CONV_DIM), BF16),
                   jax.ShapeDtypeStruct((n_i, CONV_W - 1, CONV_DIM), F32)],
        scratch_shapes=[pltpu.VMEM((n_j, CONV_W - 1, tc), F32)],
        compiler_params=_cp(2), name="conv_branch")(xn, wb, wc, wh, cw, g.reshape(1, CONV_DIM))


def _conv_step_kernel(x_ref, wb_ref, wc_ref, wh_ref, cw_ref, g_ref, p0_ref, p1_ref, o_ref, u_ref):
    x = x_ref[...]
    gate_b = _dot(x, wb_ref[...])
    u = _dot(x, wc_ref[...]) * _dot(x, wh_ref[...])
    cw = cw_ref[...]
    y = p0_ref[...] * cw[0:1, :] + p1_ref[...] * cw[1:2, :] + u * cw[2:3, :]
    _group_norm_store(gate_b * y, g_ref, o_ref)
    u_ref[...] = u


def _conv_step(xn, wb, wc, wh, cw, g, p0, p1, tc):
    m, d = xn.shape
    col = lambda j: (0, j)
    return pl.pallas_call(
        _conv_step_kernel, grid=(CONV_DIM // tc,),
        in_specs=[pl.BlockSpec((m, d), lambda j: (0, 0)),
                  pl.BlockSpec((d, tc), col), pl.BlockSpec((d, tc), col),
                  pl.BlockSpec((d, tc), col), pl.BlockSpec((CONV_W, tc), col),
                  pl.BlockSpec((1, tc), col), pl.BlockSpec((m, tc), col),
                  pl.BlockSpec((m, tc), col)],
        out_specs=[pl.BlockSpec((m, tc), col), pl.BlockSpec((m, tc), col)],
        out_shape=[jax.ShapeDtypeStruct((m, CONV_DIM), BF16),
                   jax.ShapeDtypeStruct((m, CONV_DIM), F32)],
        compiler_params=_cp(1), name="conv_step")(xn, wb, wc, wh, cw, g.reshape(1, CONV_DIM), p0, p1)


def _prompt_step(q_ref, k_ref, v_ref, c_ref, m_scr, l_scr, acc_scr, diagonal):
    tq, tk = q_ref.shape[1], k_ref.shape[2]
    for h in range(ATT_HEADS):
        sl = slice(h * HEAD_DIM, (h + 1) * HEAD_DIM)
        k = k_ref[0, 0, :, sl].astype(BF16)
        vt = v_ref[0, 0, :, sl].T.astype(BF16)
        c_keys = c_ref[0, :, h:h + 1]
        for s0 in range(0, tq, Q_STRIP):
            qs = slice(s0, s0 + Q_STRIP)
            nkeys = min(tk, s0 + Q_STRIP) if diagonal else tk
            st = lax.dot_general(k[:nkeys], q_ref[0, qs, sl], (((1,), (1,)), ((), ())),
                                 preferred_element_type=F32)
            st = st * ATT_SCALE - c_keys[:nkeys]
            if diagonal:
                key = lax.broadcasted_iota(jnp.int32, (nkeys, Q_STRIP), 0)
                qry = lax.broadcasted_iota(jnp.int32, (nkeys, Q_STRIP), 1) + s0
                st = jnp.where(key <= qry, st, NEG_BIG)
            m_prev = m_scr[h, :, qs]
            m_new = jnp.maximum(m_prev, jnp.max(st, axis=0, keepdims=True))
            alpha = jnp.exp(m_prev - m_new)
            p = jnp.exp(st - m_new)
            l_scr[h, :, qs] = alpha * l_scr[h, :, qs] + jnp.sum(p, axis=0, keepdims=True)
            acc_scr[h, :, qs] = alpha * acc_scr[h, :, qs] + _dot(vt[:, :nkeys], p.astype(BF16))
            m_scr[h, :, qs] = m_new


def _prompt_finish(g_ref, o_ref, l_scr, acc_scr):
    for h in range(ATT_HEADS):
        sl = slice(h * HEAD_DIM, (h + 1) * HEAD_DIM)
        ot = acc_scr[h] / l_scr[h]
        ms = jnp.mean(ot * ot, axis=0, keepdims=True)
        o = (ot * lax.rsqrt(ms + RMS_EPS)).T
        o_ref[0, :, sl] = (o * g_ref[:, sl]).astype(o_ref.dtype)


def _decode_step(c, n_chunks, q_ref, kn_ref, vn_ref, lfn_ref, g_ref, k_refs, v_refs, lf_refs,
                 o_ref, m_scr, l_scr, acc_scr, carry_scr, lf_scr):
    npg = PAGES_PER_STEP
    width = lf_scr.shape[1]

    @pl.when(c == 0)
    def _():
        m_scr[...] = jnp.full(m_scr.shape, NEG_BIG, F32)
        l_scr[...] = jnp.zeros_like(l_scr)
        acc_scr[...] = jnp.zeros_like(acc_scr)
        carry_scr[...] = lfn_ref[0]

    lane = lax.broadcasted_iota(jnp.int32, (npg, width), 1)
    for p in range(npg):
        lf_scr[p:p + 1, :] = lf_refs[p][0, 0]
    lf = lf_scr[...]
    y = lf
    sh = ATT_HEADS
    while sh < width:
        y = y + jnp.where(lane < width - sh, pltpu.roll(y, width - sh, axis=1), 0.0)
        sh *= 2
    excl = y - lf
    z = jnp.where(lane < ATT_HEADS, y, 0.0)
    sh = ATT_HEADS
    while sh < width:
        z = z + pltpu.roll(z, sh, axis=1)
        sh *= 2
    carry = carry_scr[...]
    bias = [None] * npg
    for p in reversed(range(npg)):
        bias[p] = carry + excl[p:p + 1, :]
        carry = carry + z[p:p + 1, :]
    carry_scr[...] = carry

    q = q_ref[0]
    hrow = lax.broadcasted_iota(jnp.int32, (ATT_HEADS, width), 0)
    hlane = lax.broadcasted_iota(jnp.int32, (ATT_HEADS, width), 1)
    own = jnp.bitwise_and(hlane, ATT_HEADS - 1) == hrow
    s_all = []
    for p in range(npg):
        kp = k_refs[p][0, 0].astype(BF16)
        s = lax.dot_general(q, kp, (((1,), (1,)), ((), ())), preferred_element_type=F32)
        s_all.append(jnp.where(own, s * ATT_SCALE + bias[p], NEG_BIG))
    m_prev = m_scr[...]
    m_new = m_prev
    for s in s_all:
        m_new = jnp.maximum(m_new, jnp.max(s, axis=1, keepdims=True))
    alpha = jnp.exp(m_prev - m_new)
    l_new = alpha * l_scr[...]
    acc = alpha * acc_scr[...]
    for p in range(npg):
        pr = jnp.exp(s_all[p] - m_new)
        l_new = l_new + jnp.sum(pr, axis=1, keepdims=True)
        acc = acc + _dot(pr.astype(BF16), v_refs[p][0, 0].astype(BF16))
    m_scr[...] = m_new
    l_scr[...] = l_new
    acc_scr[...] = acc

    @pl.when(c == n_chunks - 1)
    def _():
        kn = kn_ref[0].astype(BF16).astype(F32)
        vn = vn_ref[0].astype(BF16).astype(F32)
        s_new = jnp.sum(q.astype(F32) * kn, axis=1, keepdims=True) * ATT_SCALE
        m_fin = jnp.maximum(m_new, s_new)
        a = jnp.exp(m_new - m_fin)
        pn = jnp.exp(s_new - m_fin)
        o = (a * acc + pn * vn) / (a * l_new + pn)
        ms = jnp.mean(o * o, axis=-1, keepdims=True)
        o_ref[0] = (o * lax.rsqrt(ms + RMS_EPS) * g_ref[...]).astype(o_ref.dtype)


def _attention_kernel(pt_ref, q_ref, k_ref, v_ref, c_ref, gp_ref, qs_ref, kn_ref, vn_ref, lfn_ref,
                      gs_ref, *rest):
    del pt_ref
    npg = PAGES_PER_STEP
    k_refs, v_refs, lf_refs = rest[:npg], rest[npg:2 * npg], rest[2 * npg:3 * npg]
    (o_ref, os_ref, m_scr, l_scr, acc_scr,
     dm_scr, dl_scr, dacc_scr, carry_scr, lf_scr) = rest[3 * npg:]
    qi, ki = pl.program_id(1), pl.program_id(2)

    def decode():
        _decode_step(ki, pl.num_programs(2), qs_ref, kn_ref, vn_ref, lfn_ref, gs_ref, k_refs,
                     v_refs, lf_refs, os_ref, dm_scr, dl_scr, dacc_scr, carry_scr, lf_scr)

    @pl.when(ki == 0)
    def _():
        m_scr[...] = jnp.full(m_scr.shape, NEG_BIG, F32)
        l_scr[...] = jnp.zeros_like(l_scr)
        acc_scr[...] = jnp.zeros_like(acc_scr)

    @pl.when(ki < qi)
    def _():
        _prompt_step(q_ref, k_ref, v_ref, c_ref, m_scr, l_scr, acc_scr, False)
        decode()

    @pl.when(ki == qi)
    def _():
        _prompt_step(q_ref, k_ref, v_ref, c_ref, m_scr, l_scr, acc_scr, True)
        _prompt_finish(gp_ref, o_ref, l_scr, acc_scr)
        decode()

    @pl.when(ki > qi)
    def _():
        decode()


def _attention(layer, q, k, v, c, g, page_table, qs, kn, vn, lfn, cache_k, cache_v, cache_logf,
               blk):
    n, s, _ = q.shape
    tq = tk = blk
    nq = nk = s // blk
    n_dec, n_pages = page_table.shape
    depth, n_pool, page, heads, hd = cache_k.shape
    width = page * heads
    assert n_dec == n * nq and n_pages == nk * PAGES_PER_STEP and blk % Q_STRIP == 0
    ck = cache_k.reshape(depth, n_pool, width, hd)
    cv = cache_v.reshape(depth, n_pool, width, hd)
    cl = cache_logf.reshape(depth, n_pool, 1, width)

    def kv_block(qi, ki):
        return jnp.minimum(ki, (tq // tk) * (qi + 1) - 1)

    def page_map(p):
        return lambda b, qi, ki, pt: (layer, pt[b * nq + qi, (nk - 1 - ki) * PAGES_PER_STEP + p], 0, 0)

    kv_map = lambda b, qi, ki, pt: (layer, b, kv_block(qi, ki), 0)
    q_map = lambda b, qi, ki, pt: (b, qi, 0)
    seq_map = lambda b, qi, ki, pt: (b * nq + qi, 0, 0)
    fixed = lambda b, qi, ki, pt: (0, 0)
    in_specs = [pl.BlockSpec((1, tq, ATT_DIM), q_map),
                pl.BlockSpec((1, 1, tk, ATT_DIM), kv_map), pl.BlockSpec((1, 1, tk, ATT_DIM), kv_map),
                pl.BlockSpec((1, tk, LANES), lambda b, qi, ki, pt: (b, kv_block(qi, ki), 0)),
                pl.BlockSpec((1, ATT_DIM), fixed),
                pl.BlockSpec((1, heads, hd), seq_map), pl.BlockSpec((1, heads, hd), seq_map),
                pl.BlockSpec((1, heads, hd), seq_map), pl.BlockSpec((1, 1, width), seq_map),
                pl.BlockSpec((heads, hd), fixed)]
    in_specs += [pl.BlockSpec((1, 1, width, hd), page_map(p)) for p in range(PAGES_PER_STEP)]
    in_specs += [pl.BlockSpec((1, 1, width, hd), page_map(p)) for p in range(PAGES_PER_STEP)]
    in_specs += [pl.BlockSpec((1, 1, 1, width), page_map(p)) for p in range(PAGES_PER_STEP)]
    grid_spec = pltpu.PrefetchScalarGridSpec(
        num_scalar_prefetch=1, grid=(n, nq, nk), in_specs=in_specs,
        out_specs=[pl.BlockSpec((1, tq, ATT_DIM), q_map), pl.BlockSpec((1, heads, hd), seq_map)],
        scratch_shapes=[pltpu.VMEM((ATT_HEADS, 1, tq), F32), pltpu.VMEM((ATT_HEADS, 1, tq), F32),
                        pltpu.VMEM((ATT_HEADS, HEAD_DIM, tq), F32),
                        pltpu.VMEM((heads, 1), F32), pltpu.VMEM((heads, 1), F32),
                        pltpu.VMEM((heads, hd), F32), pltpu.VMEM((1, width), F32),
                        pltpu.VMEM((PAGES_PER_STEP, width), F32)])
    return pl.pallas_call(
        _attention_kernel, grid_spec=grid_spec,
        out_shape=[jax.ShapeDtypeStruct((n, s, ATT_DIM), BF16),
                   jax.ShapeDtypeStruct((n_dec, heads, hd), BF16)],
        compiler_params=_cp(3), name="attention")(
            page_table, q, k, v, c, g.reshape(1, ATT_DIM), qs, kn, vn, lfn, g.reshape(heads, hd),
            *([ck] * PAGES_PER_STEP), *([cv] * PAGES_PER_STEP), *([cl] * PAGES_PER_STEP))


def _prompt_attn_kernel(q_ref, k_ref, v_ref, c_ref, g_ref, o_ref, m_scr, l_scr, acc_scr):
    qi, ki = pl.program_id(1), pl.program_id(2)
    tq, tk = q_ref.shape[1], k_ref.shape[2]

    @pl.when(ki == 0)
    def _():
        m_scr[...] = jnp.full(m_scr.shape, NEG_BIG, F32)
        l_scr[...] = jnp.zeros_like(l_scr)
        acc_scr[...] = jnp.zeros_like(acc_scr)

    def update(masked):
        if masked:
            key = lax.broadcasted_iota(jnp.int32, (tk, tq), 0)
            qry = lax.broadcasted_iota(jnp.int32, (tk, tq), 1)
            keep = key <= qry
        for h in range(ATT_HEADS):
            sl = slice(h * HEAD_DIM, (h + 1) * HEAD_DIM)
            q = q_ref[0, :, sl]
            k = k_ref[0, 0, :, sl].astype(BF16)
            st = lax.dot_general(k, q, (((1,), (1,)), ((), ())), preferred_element_type=F32)
            st = st * ATT_SCALE - c_ref[0, :, h:h + 1]
            if masked:
                st = jnp.where(keep, st, NEG_BIG)
            m_prev = m_scr[h]
            m_new = jnp.maximum(m_prev, jnp.max(st, axis=0, keepdims=True))
            alpha = jnp.exp(m_prev - m_new)
            p = jnp.exp(st - m_new)
            l_scr[h] = alpha * l_scr[h] + jnp.sum(p, axis=0, keepdims=True)
            vt = v_ref[0, 0, :, sl].T.astype(BF16)
            acc_scr[h] = alpha * acc_scr[h] + _dot(vt, p.astype(BF16))
            m_scr[h] = m_new

    @pl.when(ki < qi)
    def _():
        update(False)

    @pl.when(ki == qi)
    def _():
        update(True)
        for h in range(ATT_HEADS):
            sl = slice(h * HEAD_DIM, (h + 1) * HEAD_DIM)
            ot = acc_scr[h] / l_scr[h]
            ms = jnp.mean(ot * ot, axis=0, keepdims=True)
            o = (ot * lax.rsqrt(ms + RMS_EPS)).T
            o_ref[0, :, sl] = (o * g_ref[:, sl]).astype(o_ref.dtype)


def _prompt_attention(q, k, v, layer, c, g, blk):
    n, s, _ = q.shape
    nb = s // blk
    kv_map = lambda b, qi, ki: (layer, b, jnp.minimum(ki, qi), 0)
    return pl.pallas_call(
        _prompt_attn_kernel, grid=(n, nb, nb),
        in_specs=[pl.BlockSpec((1, blk, ATT_DIM), lambda b, qi, ki: (b, qi, 0)),
                  pl.BlockSpec((1, 1, blk, ATT_DIM), kv_map),
                  pl.BlockSpec((1, 1, blk, ATT_DIM), kv_map),
                  pl.BlockSpec((1, blk, LANES), lambda b, qi, ki: (b, jnp.minimum(ki, qi), 0)),
                  pl.BlockSpec((1, ATT_DIM), lambda b, qi, ki: (0, 0))],
        out_specs=pl.BlockSpec((1, blk, ATT_DIM), lambda b, qi, ki: (b, qi, 0)),
        out_shape=jax.ShapeDtypeStruct((n, s, ATT_DIM), BF16),
        scratch_shapes=[pltpu.VMEM((ATT_HEADS, 1, blk), F32),
                        pltpu.VMEM((ATT_HEADS, 1, blk), F32),
                        pltpu.VMEM((ATT_HEADS, HEAD_DIM, blk), F32)],
        compiler_params=_cp(3), name="prompt_attention")(q, k, v, c, g.reshape(1, ATT_DIM))


def _decode_attn_kernel(pt_ref, q_ref, kn_ref, vn_ref, lfn_ref, g_ref, *rest):
    del pt_ref
    npg = PAGES_PER_STEP
    k_refs, v_refs, lf_refs = rest[:npg], rest[npg:2 * npg], rest[2 * npg:3 * npg]
    o_ref, m_scr, l_scr, acc_scr, carry_scr, lf_scr = rest[3 * npg:]
    c = pl.program_id(1)
    width = ATT_HEADS * HEAD_DIM

    @pl.when(c == 0)
    def _():
        m_scr[...] = jnp.full(m_scr.shape, NEG_BIG, F32)
        l_scr[...] = jnp.zeros_like(l_scr)
        acc_scr[...] = jnp.zeros_like(acc_scr)
        carry_scr[...] = lfn_ref[0]

    lane = lax.broadcasted_iota(jnp.int32, (npg, width), 1)
    for p in range(npg):
        lf_scr[p:p + 1, :] = lf_refs[p][0, 0]
    lf = lf_scr[...]
    y = lf
    sh = ATT_HEADS
    while sh < width:
        y = y + jnp.where(lane < width - sh, pltpu.roll(y, width - sh, axis=1), 0.0)
        sh *= 2
    excl = y - lf
    z = jnp.where(lane < ATT_HEADS, y, 0.0)
    sh = ATT_HEADS
    while sh < width:
        z = z + pltpu.roll(z, sh, axis=1)
        sh *= 2
    carry = carry_scr[...]
    bias = [None] * npg
    for p in reversed(range(npg)):
        bias[p] = carry + excl[p:p + 1, :]
        carry = carry + z[p:p + 1, :]
    carry_scr[...] = carry

    q = q_ref[0]
    hrow = lax.broadcasted_iota(jnp.int32, (ATT_HEADS, width), 0)
    hlane = lax.broadcasted_iota(jnp.int32, (ATT_HEADS, width), 1)
    own = jnp.bitwise_and(hlane, ATT_HEADS - 1) == hrow
    s_all = []
    for p in range(npg):
        kp = k_refs[p][0, 0].astype(BF16)
        s = lax.dot_general(q, kp, (((1,), (1,)), ((), ())), preferred_element_type=F32)
        s_all.append(jnp.where(own, s * ATT_SCALE + bias[p], NEG_BIG))
    m_prev = m_scr[...]
    m_new = m_prev
    for s in s_all:
        m_new = jnp.maximum(m_new, jnp.max(s, axis=1, keepdims=True))
    alpha = jnp.exp(m_prev - m_new)
    l_new = alpha * l_scr[...]
    acc = alpha * acc_scr[...]
    for p in range(npg):
        pr = jnp.exp(s_all[p] - m_new)
        l_new = l_new + jnp.sum(pr, axis=1, keepdims=True)
        acc = acc + _dot(pr.astype(BF16), v_refs[p][0, 0].astype(BF16))
    m_scr[...] = m_new
    l_scr[...] = l_new
    acc_scr[...] = acc

    @pl.when(c == pl.num_programs(1) - 1)
    def _():
        kn = kn_ref[0].astype(BF16).astype(F32)
        vn = vn_ref[0].astype(BF16).astype(F32)
        s_new = jnp.sum(q.astype(F32) * kn, axis=1, keepdims=True) * ATT_SCALE
        m_fin = jnp.maximum(m_new, s_new)
        a = jnp.exp(m_new - m_fin)
        pn = jnp.exp(s_new - m_fin)
        o = (a * acc + pn * vn) / (a * l_new + pn)
        ms = jnp.mean(o * o, axis=-1, keepdims=True)
        o_ref[0] = (o * lax.rsqrt(ms + RMS_EPS) * g_ref[...]).astype(o_ref.dtype)


def _decode_attention(layer, page_table, q, kn, vn, lfn, g, cache_k, cache_v, cache_logf):
    n, n_pages = page_table.shape
    depth, n_pool, page, heads, hd = cache_k.shape
    width = page * heads
    ck = cache_k.reshape(depth, n_pool, width, hd)
    cv = cache_v.reshape(depth, n_pool, width, hd)
    cl = cache_logf.reshape(depth, n_pool, 1, width)
    n_chunks = n_pages // PAGES_PER_STEP

    def page_map(p):
        return lambda b, c, pt: (layer, pt[b, (n_chunks - 1 - c) * PAGES_PER_STEP + p], 0, 0)

    per_seq = lambda b, c, pt: (b, 0, 0)
    in_specs = [pl.BlockSpec((1, heads, hd), per_seq), pl.BlockSpec((1, heads, hd), per_seq),
                pl.BlockSpec((1, heads, hd), per_seq), pl.BlockSpec((1, 1, width), per_seq),
                pl.BlockSpec((heads, hd), lambda b, c, pt: (0, 0))]
    in_specs += [pl.BlockSpec((1, 1, width, hd), page_map(p)) for p in range(PAGES_PER_STEP)]
    in_specs += [pl.BlockSpec((1, 1, width, hd), page_map(p)) for p in range(PAGES_PER_STEP)]
    in_specs += [pl.BlockSpec((1, 1, 1, width), page_map(p)) for p in range(PAGES_PER_STEP)]
    grid_spec = pltpu.PrefetchScalarGridSpec(
        num_scalar_prefetch=1, grid=(n, n_chunks), in_specs=in_specs,
        out_specs=pl.BlockSpec((1, heads, hd), per_seq),
        scratch_shapes=[pltpu.VMEM((heads, 1), F32), pltpu.VMEM((heads, 1), F32),
                        pltpu.VMEM((heads, hd), F32), pltpu.VMEM((1, width), F32),
                        pltpu.VMEM((PAGES_PER_STEP, width), F32)])
    return pl.pallas_call(
        _decode_attn_kernel, grid_spec=grid_spec,
        out_shape=jax.ShapeDtypeStruct((n, heads, hd), BF16),
        compiler_params=_cp(2), name="decode_attention")(
            page_table, q, kn, vn, lfn, g.reshape(heads, hd),
            *([ck] * PAGES_PER_STEP), *([cv] * PAGES_PER_STEP), *([cl] * PAGES_PER_STEP))


def _pad_lanes(w):
    return jnp.pad(w, ((0, 0), (0, LANES - w.shape[1])))


def kernel(x_prompt, x_sample, cache_k, cache_v, cache_logf, state_conv, page_table, norm_mix_g,
           w_in, b_forget, conv_w, g_att_out, g_conv_out, w_out, norm_ffn_g, dense_w_gate,
           dense_w_up, dense_w_down, moe_router, moe_w_gate, moe_w_up, moe_w_down, final_norm_g):
    n, seq, d = x_prompt.shape
    n_dec, dec_seq, _ = x_sample.shape
    assert dec_seq == 1 and d == D_MODEL
    depth = w_in.shape[0]
    mp = n * seq
    tm_p, tm_s = 512, n_dec

    xp = x_prompt.reshape(mp, d)
    xs = x_sample.reshape(n_dec, d)
    xnp = _rmsnorm(xp, norm_mix_g[0], tm_p)
    xns = _rmsnorm(xs, norm_mix_g[0], tm_s)

    outs = {key: [] for key in ("fp", "cp", "ks", "vs", "fs", "cs")}
    yp = ys = k_all = v_all = None
    for l in range(depth):
        wl = w_in[l]
        o = 0
        wq = wl[:, o:o + ATT_DIM].astype(BF16); o += ATT_DIM
        wk = wl[:, o:o + ATT_DIM].astype(BF16); o += ATT_DIM
        wv = wl[:, o:o + ATT_DIM].astype(BF16); o += ATT_DIM
        wf = _pad_lanes(wl[:, o:o + ATT_HEADS]).astype(BF16); o += ATT_HEADS
        wb = wl[:, o:o + CONV_DIM].astype(BF16); o += CONV_DIM
        wc = wl[:, o:o + CONV_DIM].astype(BF16); o += CONV_DIM
        wh = wl[:, o:o + CONV_DIM].astype(BF16)
        bf = _pad_lanes(b_forget[l].reshape(1, ATT_HEADS))
        wo_a = w_out[l][:ATT_DIM].astype(BF16)
        wo_c = w_out[l][ATT_DIM:].astype(BF16)

        q_p = _mm(xnp, wq, BF16, 1024, 1024, "q_proj")
        k_all = _mm_layer(xnp, wk, k_all, l, depth, 1024, 1024, "k_proj")
        v_all = _mm_layer(xnp, wv, v_all, l, depth, 1024, 1024, "v_proj")
        lf_p, c_p = _forget(xnp, wf, bf, seq, True)
        conv_p, tails = _conv_branch(xnp, wb, wc, wh, conv_w[l], g_conv_out[l], seq, 512, 1024)
        outs["fp"].append(lf_p[:, :ATT_HEADS].reshape(n, seq, ATT_HEADS))
        tiles_per_seq = seq // 512
        outs["cp"].append(tails[tiles_per_seq - 1::tiles_per_seq])

        q_s = _mm(xns, wq, BF16, tm_s, 512, "q_proj_s")
        k_s = _mm(xns, wk, F32, tm_s, 512, "k_proj_s")
        v_s = _mm(xns, wv, F32, tm_s, 512, "v_proj_s")
        lf_s = _forget(xns, wf, bf, n_dec, False)[0]
        st = state_conv[l]
        conv_s, u_s = _conv_step(xns, wb, wc, wh, conv_w[l], g_conv_out[l], st[:, 0], st[:, 1], 512)
        lfn = jnp.tile(lf_s[:, :ATT_HEADS], (1, cache_k.shape[2])).reshape(n_dec, 1, -1)

        att_p, att_s = _attention(
            l, q_p.reshape(n, seq, ATT_DIM), k_all.reshape(depth, n, seq, ATT_DIM),
            v_all.reshape(depth, n, seq, ATT_DIM), c_p.reshape(n, seq, LANES), g_att_out[l],
            page_table, q_s.reshape(n_dec, ATT_HEADS, HEAD_DIM),
            k_s.reshape(n_dec, ATT_HEADS, HEAD_DIM), v_s.reshape(n_dec, ATT_HEADS, HEAD_DIM), lfn,
            cache_k, cache_v, cache_logf, 512)
        xp, hnp = _outproj(att_p.reshape(mp, ATT_DIM), conv_p, wo_a, wo_c, xp, norm_ffn_g[l], tm_p)
        xs, hns = _outproj(att_s.reshape(n_dec, ATT_DIM), conv_s, wo_a, wo_c, xs, norm_ffn_g[l], tm_s)
        outs["ks"].append(k_s.reshape(n_dec, 1, ATT_HEADS, HEAD_DIM))
        outs["vs"].append(v_s.reshape(n_dec, 1, ATT_HEADS, HEAD_DIM))
        outs["fs"].append(lf_s[:, :ATT_HEADS].reshape(n_dec, 1, ATT_HEADS))
        outs["cs"].append(jnp.stack([st[:, 1], u_s], axis=1))

        last = l == depth - 1
        next_g = final_norm_g if last else norm_mix_g[l + 1]
        next_dtype = F32 if last else BF16
        i = l // 2
        if l % 2 == 0:
            wg = dense_w_gate[i].astype(BF16)
            wu = dense_w_up[i].astype(BF16)
            wd = dense_w_down[i].astype(BF16)
            hp = _swiglu(hnp, wg, wu, 1024, 512)
            xp, nxp = _down(hp, wd, xp, next_g, 256, not last, next_dtype)
            hs = _swiglu(hns, wg, wu, tm_s, 512)
            xs, nxs = _down(hs, wd, xs, next_g, tm_s, not last, next_dtype)
        else:
            xp, nxp, xs, nxs = _moe_ffn(
                xp, hnp, xs, hns, norm_ffn_g[l], moe_router[i], moe_w_gate[i].astype(BF16),
                moe_w_up[i].astype(BF16), moe_w_down[i].astype(BF16), next_g, not last, next_dtype)
        if last:
            yp, ys = nxp, nxs
        else:
            xnp, xns = nxp, nxs

    return (yp.reshape(n, seq, d), ys.reshape(n_dec, 1, d),
            k_all.reshape(depth, n, seq, ATT_HEADS, HEAD_DIM),
            v_all.reshape(depth, n, seq, ATT_HEADS, HEAD_DIM),
            jnp.stack(outs["fp"]), jnp.stack(outs["cp"]),
            jnp.stack(outs["ks"]), jnp.stack(outs["vs"]), jnp.stack(outs["fs"]), jnp.stack(outs["cs"]))
```

```python
import functools

import jax
import jax.numpy as jnp
from jax import lax
from jax.experimental import pallas as pl
from jax.experimental.pallas import tpu as pltpu

D_MODEL = 2048
HEAD_DIM = 128
ATT_HEADS = 8
ATT_DIM = ATT_HEADS * HEAD_DIM
CONV_DIM = D_MODEL - ATT_DIM
CONV_W = 3
N_EXPERTS = 8
TOP_K = 2
ATT_SCALE = HEAD_DIM ** -0.5
RMS_EPS = 1e-6
LANES = 128
NEG_BIG = -1e30
PAGES_PER_STEP = 16
Q_STRIP = 128
VMEM_LIMIT = 56 * 1024 * 1024

F32 = jnp.float32
BF16 = jnp.bfloat16


def _cp(n_axes):
    return pltpu.CompilerParams(dimension_semantics=("arbitrary",) * n_axes,
                                vmem_limit_bytes=VMEM_LIMIT)


def _rms(x, g):
    return x * lax.rsqrt(jnp.mean(x * x, axis=-1, keepdims=True) + RMS_EPS) * g


def _dot(a, b):
    return jnp.dot(a, b, preferred_element_type=F32)


def _rmsnorm_kernel(x_ref, g_ref, o_ref):
    o_ref[...] = _rms(x_ref[...], g_ref[...]).astype(o_ref.dtype)


def _rmsnorm(x, g, tm, out_dtype=BF16):
    m, d = x.shape
    return pl.pallas_call(
        _rmsnorm_kernel, grid=(m // tm,),
        in_specs=[pl.BlockSpec((tm, d), lambda i: (i, 0)),
                  pl.BlockSpec((1, d), lambda i: (0, 0))],
        out_specs=pl.BlockSpec((tm, d), lambda i: (i, 0)),
        out_shape=jax.ShapeDtypeStruct((m, d), out_dtype),
        compiler_params=_cp(1), name="rmsnorm")(x, g.reshape(1, d))


def _mm_kernel(x_ref, w_ref, *rest):
    o_ref = rest[-1]
    o_ref[...] = _dot(x_ref[...], w_ref[...]).astype(o_ref.dtype).reshape(o_ref.shape)


def _mm_layer(x, w, stacked, layer, depth, tm, tn, name):
    m, k = x.shape
    n = w.shape[1]
    in_specs = [pl.BlockSpec((tm, k), lambda i, j: (i, 0)),
                pl.BlockSpec((k, tn), lambda i, j: (0, j))]
    args, aliases = [x, w], {}
    if stacked is not None:
        in_specs.append(pl.BlockSpec(memory_space=pl.ANY))
        args.append(stacked)
        aliases = {2: 0}
    return pl.pallas_call(
        _mm_kernel, grid=(m // tm, n // tn), in_specs=in_specs,
        out_specs=pl.BlockSpec((1, tm, tn), lambda i, j: (layer, i, j)),
        out_shape=jax.ShapeDtypeStruct((depth, m, n), F32), input_output_aliases=aliases,
        compiler_params=_cp(2), name=name)(*args)


def _mm(x, w, out_dtype, tm, tn, name):
    m, k = x.shape
    n = w.shape[1]
    return pl.pallas_call(
        _mm_kernel, grid=(m // tm, n // tn),
        in_specs=[pl.BlockSpec((tm, k), lambda i, j: (i, 0)),
                  pl.BlockSpec((k, tn), lambda i, j: (0, j))],
        out_specs=pl.BlockSpec((tm, tn), lambda i, j: (i, j)),
        out_shape=jax.ShapeDtypeStruct((m, n), out_dtype),
        compiler_params=_cp(2), name=name)(x, w)


def _swiglu_kernel(x_ref, wg_ref, wu_ref, o_ref):
    x = x_ref[...]
    a = _dot(x, wg_ref[...])
    b = _dot(x, wu_ref[...])
    o_ref[...] = (a * (1.0 / (1.0 + jnp.exp(-a))) * b).astype(o_ref.dtype)


def _swiglu(x, wg, wu, tm, tn):
    m, k = x.shape
    n = wg.shape[1]
    return pl.pallas_call(
        _swiglu_kernel, grid=(m // tm, n // tn),
        in_specs=[pl.BlockSpec((tm, k), lambda i, j: (i, 0)),
                  pl.BlockSpec((k, tn), lambda i, j: (0, j)),
                  pl.BlockSpec((k, tn), lambda i, j: (0, j))],
        out_specs=pl.BlockSpec((tm, tn), lambda i, j: (i, j)),
        out_shape=jax.ShapeDtypeStruct((m, n), BF16),
        compiler_params=_cp(2), name="swiglu_up")(x, wg, wu)


def _down_kernel(h_ref, w_ref, res_ref, g_ref, *outs, emit_x):
    xo = res_ref[...] + _dot(h_ref[...], w_ref[...])
    if emit_x:
        outs[0][...] = xo
    outs[-1][...] = _rms(xo, g_ref[...]).astype(outs[-1].dtype)


def _down(h, w, res, norm_g, tm, emit_x, norm_dtype):
    m, kdim = h.shape
    d = w.shape[1]
    row = lambda i: (i, 0)
    fixed = lambda i: (0, 0)
    out_specs = [pl.BlockSpec((tm, d), row)]
    out_shape = [jax.ShapeDtypeStruct((m, d), norm_dtype)]
    if emit_x:
        out_specs.insert(0, pl.BlockSpec((tm, d), row))
        out_shape.insert(0, jax.ShapeDtypeStruct((m, d), F32))
    outs = pl.pallas_call(
        functools.partial(_down_kernel, emit_x=emit_x), grid=(m // tm,),
        in_specs=[pl.BlockSpec((tm, kdim), row),
                  pl.BlockSpec((kdim, d), fixed, pipeline_mode=pl.Buffered(1)),
                  pl.BlockSpec((tm, d), row), pl.BlockSpec((1, d), fixed)],
        out_specs=out_specs, out_shape=out_shape,
        compiler_params=_cp(1), name="ffn_down")(h, w, res, norm_g.reshape(1, d))
    return (outs[0], outs[1]) if emit_x else (None, outs[0])


def _outproj_kernel(a_ref, c_ref, wa_ref, wc_ref, res_ref, g_ref, xo_ref, xn_ref):
    xo = res_ref[...] + (_dot(a_ref[...], wa_ref[...]) + _dot(c_ref[...], wc_ref[...]))
    xo_ref[...] = xo
    xn_ref[...] = _rms(xo, g_ref[...]).astype(xn_ref.dtype)


def _outproj(a, c, wa, wc, res, g, tm):
    m, d = res.shape
    ka, kc = a.shape[1], c.shape[1]
    row = lambda i: (i, 0)
    fixed = lambda i: (0, 0)
    return pl.pallas_call(
        _outproj_kernel, grid=(m // tm,),
        in_specs=[pl.BlockSpec((tm, ka), row), pl.BlockSpec((tm, kc), row),
                  pl.BlockSpec((ka, d), fixed), pl.BlockSpec((kc, d), fixed),
                  pl.BlockSpec((tm, d), row), pl.BlockSpec((1, d), fixed)],
        out_specs=[pl.BlockSpec((tm, d), row), pl.BlockSpec((tm, d), row)],
        out_shape=[jax.ShapeDtypeStruct((m, d), F32), jax.ShapeDtypeStruct((m, d), BF16)],
        compiler_params=_cp(1), name="out_proj")(a, c, wa, wc, res, g.reshape(1, d))


SEL_E1, SEL_E2, SEL_G1, SEL_G2, SEL_R1, SEL_R2 = range(6)


def _router_kernel(x_ref, r_ref, cin_ref, sel_ref, cnt_ref, carry_scr, *, n_valid):
    i = pl.program_id(0)
    tm = x_ref.shape[0]

    @pl.when(i == 0)
    def _():
        carry_scr[...] = cin_ref[...]

    logits = _dot(x_ref[...], r_ref[...])
    lane = lax.broadcasted_iota(jnp.int32, logits.shape, 1).astype(F32)
    lg = jnp.where(lane < N_EXPERTS, logits, -jnp.inf)
    m1 = jnp.max(lg, axis=1, keepdims=True)
    i1 = jnp.min(jnp.where(lg == m1, lane, float(LANES)), axis=1, keepdims=True)
    lg2 = jnp.where(lane == i1, -jnp.inf, lg)
    m2 = jnp.max(lg2, axis=1, keepdims=True)
    i2 = jnp.min(jnp.where(lg2 == m2, lane, float(LANES)), axis=1, keepdims=True)
    e2 = jnp.exp(m2 - m1)
    den = 1.0 + e2
    row = lax.broadcasted_iota(jnp.int32, logits.shape, 0) + i * tm
    hit = jnp.where(((lane == i1) | (lane == i2)) & (row < n_valid), 1.0, 0.0)
    r = lax.broadcasted_iota(jnp.int32, (tm, tm), 0)
    c = lax.broadcasted_iota(jnp.int32, (tm, tm), 1)
    before = (c < r).astype(BF16)
    rank = _dot(before, hit.astype(BF16)) + carry_scr[...]
    carry_scr[...] = carry_scr[...] + jnp.sum(hit, axis=0, keepdims=True)
    cnt_ref[...] = carry_scr[...]
    rank1 = jnp.sum(jnp.where(lane == i1, rank, 0.0), axis=1, keepdims=True)
    rank2 = jnp.sum(jnp.where(lane == i2, rank, 0.0), axis=1, keepdims=True)
    sel = jnp.zeros_like(logits)
    for idx, val in ((SEL_E1, i1), (SEL_E2, i2), (SEL_G1, 1.0 / den), (SEL_G2, e2 / den),
                     (SEL_R1, rank1), (SEL_R2, rank2)):
        sel = jnp.where(lane == idx, val, sel)
    sel_ref[...] = sel


def _router(hn, r_pad, count_in, tm, n_valid):
    m, d = hn.shape
    fixed = lambda i: (0, 0)
    return pl.pallas_call(
        functools.partial(_router_kernel, n_valid=n_valid), grid=(m // tm,),
        in_specs=[pl.BlockSpec((tm, d), lambda i: (i, 0)), pl.BlockSpec((d, LANES), fixed),
                  pl.BlockSpec((1, LANES), fixed)],
        out_specs=[pl.BlockSpec((tm, LANES), lambda i: (i, 0)), pl.BlockSpec((1, LANES), fixed)],
        out_shape=[jax.ShapeDtypeStruct((m, LANES), F32), jax.ShapeDtypeStruct((1, LANES), F32)],
        scratch_shapes=[pltpu.VMEM((1, LANES), F32)],
        compiler_params=_cp(1), name="router")(hn, r_pad, count_in)


def _dispatch_kernel(d1_ref, d2_ref, ps_ref, pc_ref, x_ref, *rest, first):
    xs_ref, zero_scr, sem = rest[-3:]
    tt = x_ref.shape[0]
    base = pl.program_id(0) * tt

    def row_copy(src_row, dst_row):
        return pltpu.make_async_copy(src_row, xs_ref.at[pl.ds(dst_row, 1), :], sem)

    def issue(r, carry):
        src = x_ref.at[pl.ds(r, 1), :]
        row_copy(src, d1_ref[base + r]).start()
        row_copy(src, d2_ref[base + r]).start(priority=1)
        return carry

    def drain(r, carry):
        row_copy(x_ref.at[pl.ds(0, 1), :], 0).wait()
        row_copy(x_ref.at[pl.ds(0, 1), :], 0).wait()
        return carry

    lax.fori_loop(0, tt, issue, 0, unroll=8)
    lax.fori_loop(0, tt, drain, 0, unroll=8)

    if first:
        @pl.when(pl.program_id(0) == 0)
        def _():
            zero_scr[...] = jnp.zeros_like(zero_scr)
            zrow = zero_scr.at[pl.ds(0, 1), :]
            for e in range(N_EXPERTS):
                def fill(r, carry, e=e):
                    row_copy(zrow, ps_ref[e] + r).start()
                    return carry

                def fill_done(r, carry):
                    row_copy(zrow, 0).wait()
                    return carry

                lax.fori_loop(0, pc_ref[e], fill, 0)
                lax.fori_loop(0, pc_ref[e], fill_done, 0)


def _dispatch(d1, d2, pad_start, pad_count, x, xs, tt, n_rows):
    m, d = x.shape
    first = xs is None
    in_specs = [pl.BlockSpec((tt, d), lambda i, *_: (i, 0))]
    args = [d1, d2, pad_start, pad_count, x]
    aliases = {}
    if not first:
        in_specs.append(pl.BlockSpec(memory_space=pl.ANY))
        args.append(xs)
        aliases = {5: 0}
    grid_spec = pltpu.PrefetchScalarGridSpec(
        num_scalar_prefetch=4, grid=(m // tt,), in_specs=in_specs,
        out_specs=pl.BlockSpec(memory_space=pl.ANY),
        scratch_shapes=[pltpu.VMEM((8, d), F32), pltpu.SemaphoreType.DMA(())])
    return pl.pallas_call(
        functools.partial(_dispatch_kernel, first=first), grid_spec=grid_spec,
        out_shape=jax.ShapeDtypeStruct((n_rows, d), F32), input_output_aliases=aliases,
        compiler_params=_cp(1), name="moe_dispatch")(*args)


def _moe_up_kernel(te_ref, na_ref, x_ref, g_ref, wg_ref, wu_ref, o_ref):
    del te_ref

    @pl.when(pl.program_id(1) < na_ref[0])
    def _():
        xn = _rms(x_ref[...], g_ref[...]).astype(BF16)
        a = _dot(xn, wg_ref[0])
        b = _dot(xn, wu_ref[0])
        o_ref[...] = (a * (1.0 / (1.0 + jnp.exp(-a))) * b).astype(o_ref.dtype)


def _moe_up(tile_expert, n_active, xs, g, wg, wu, tm, tn):
    r, d = xs.shape
    f = wg.shape[2]
    tile = lambda t, na: jnp.minimum(t, na[0] - 1)
    w_map = lambda j, t, te, na: (te[tile(t, na)], 0, j)
    grid_spec = pltpu.PrefetchScalarGridSpec(
        num_scalar_prefetch=2, grid=(f // tn, r // tm),
        in_specs=[pl.BlockSpec((tm, d), lambda j, t, te, na: (tile(t, na), 0)),
                  pl.BlockSpec((1, d), lambda j, t, te, na: (0, 0)),
                  pl.BlockSpec((1, d, tn), w_map), pl.BlockSpec((1, d, tn), w_map)],
        out_specs=pl.BlockSpec((tm, tn), lambda j, t, te, na: (tile(t, na), j)))
    return pl.pallas_call(
        _moe_up_kernel, grid_spec=grid_spec, out_shape=jax.ShapeDtypeStruct((r, f), BF16),
        compiler_params=_cp(2), name="moe_up")(tile_expert, n_active, xs, g.reshape(1, d), wg, wu)


def _moe_down_kernel(te_ref, na_ref, h_ref, w_ref, o_ref):
    del te_ref

    @pl.when(pl.program_id(0) < na_ref[0])
    def _():
        o_ref[...] = _dot(h_ref[...], w_ref[0])


def _moe_down(tile_expert, n_active, h, wd, tm):
    r, f = h.shape
    d = wd.shape[2]
    tile = lambda t, na: jnp.minimum(t, na[0] - 1)
    grid_spec = pltpu.PrefetchScalarGridSpec(
        num_scalar_prefetch=2, grid=(r // tm,),
        in_specs=[pl.BlockSpec((tm, f), lambda t, te, na: (tile(t, na), 0)),
                  pl.BlockSpec((1, f, d), lambda t, te, na: (te[tile(t, na)], 0, 0))],
        out_specs=pl.BlockSpec((tm, d), lambda t, te, na: (tile(t, na), 0)))
    return pl.pallas_call(
        _moe_down_kernel, grid_spec=grid_spec, out_shape=jax.ShapeDtypeStruct((r, d), F32),
        compiler_params=_cp(1), name="moe_down")(tile_expert, n_active, h, wd)


def _combine_kernel(d1_ref, d2_ref, ys_ref, res_ref, sel_ref, g_ref, *rest, emit_x):
    y1_scr, y2_scr, sem = rest[-3:]
    outs = rest[:-3]
    tt = res_ref.shape[0]
    base = pl.program_id(0) * tt

    def row_copy(src_row, dst):
        return pltpu.make_async_copy(ys_ref.at[pl.ds(src_row, 1), :], dst, sem)

    def issue(r, carry):
        row_copy(d1_ref[base + r], y1_scr.at[pl.ds(r, 1), :]).start()
        row_copy(d2_ref[base + r], y2_scr.at[pl.ds(r, 1), :]).start(priority=1)
        return carry

    def drain(r, carry):
        row_copy(0, y1_scr.at[pl.ds(0, 1), :]).wait()
        row_copy(0, y2_scr.at[pl.ds(0, 1), :]).wait()
        return carry

    lax.fori_loop(0, tt, issue, 0, unroll=8)
    lax.fori_loop(0, tt, drain, 0, unroll=8)
    sel = sel_ref[...]
    g1 = sel[:, SEL_G1:SEL_G1 + 1]
    g2 = sel[:, SEL_G2:SEL_G2 + 1]
    xo = res_ref[...] + (g1 * y1_scr[...] + g2 * y2_scr[...])
    if emit_x:
        outs[0][...] = xo
    outs[-1][...] = _rms(xo, g_ref[...]).astype(outs[-1].dtype)


def _combine(d1, d2, ys, res, sel, g, tt, emit_x, norm_dtype):
    m, d = res.shape
    row = lambda i, *_: (i, 0)
    out_specs = [pl.BlockSpec((tt, d), row)]
    out_shape = [jax.ShapeDtypeStruct((m, d), norm_dtype)]
    if emit_x:
        out_specs.insert(0, pl.BlockSpec((tt, d), row))
        out_shape.insert(0, jax.ShapeDtypeStruct((m, d), F32))
    grid_spec = pltpu.PrefetchScalarGridSpec(
        num_scalar_prefetch=2, grid=(m // tt,),
        in_specs=[pl.BlockSpec(memory_space=pl.ANY), pl.BlockSpec((tt, d), row),
                  pl.BlockSpec((tt, LANES), row), pl.BlockSpec((1, d), lambda i, *_: (0, 0))],
        out_specs=out_specs,
        scratch_shapes=[pltpu.VMEM((tt, d), F32), pltpu.VMEM((tt, d), F32),
                        pltpu.SemaphoreType.DMA(())])
    outs = pl.pallas_call(
        functools.partial(_combine_kernel, emit_x=emit_x), grid_spec=grid_spec,
        out_shape=out_shape, compiler_params=_cp(1), name="moe_combine")(
            d1, d2, ys, res, sel, g.reshape(1, d))
    return (outs[0], outs[1]) if emit_x else (None, outs[0])


def _moe_ffn(x_p, hn_p, x_s, hn_s, ffn_g, router, wg, wu, wd, next_g, emit_x, norm_dtype,
             tm=512, tn=1408, t_route=512, t_move=512, t_comb=256):
    mp, d = x_p.shape
    ms = x_s.shape[0]
    n_tiles = (TOP_K * (mp + ms)) // tm + N_EXPERTS
    r_pad = _pad_lanes(router).astype(BF16)
    ms_pad = -(-ms // LANES) * LANES
    hn_s_pad = jnp.pad(hn_s, ((0, ms_pad - ms), (0, 0)))
    sel_p, cnt_p = _router(hn_p, r_pad, jnp.zeros((1, LANES), F32), t_route, mp)
    sel_s, cnt = _router(hn_s_pad, r_pad, cnt_p, ms_pad, ms)

    count = cnt[0, :N_EXPERTS].astype(jnp.int32)
    padded = (count + tm - 1) // tm * tm
    g_end = jnp.cumsum(padded)
    g_start = g_end - padded
    tile_expert = jnp.minimum(
        jnp.sum(jnp.arange(n_tiles)[:, None] >= (g_end // tm)[None, :], axis=1),
        N_EXPERTS - 1).astype(jnp.int32)
    n_active = (g_end[-1:] // tm).astype(jnp.int32)

    def dests(sel, rows):
        e1, e2 = sel[:rows, SEL_E1].astype(jnp.int32), sel[:rows, SEL_E2].astype(jnp.int32)
        return (g_start[e1] + sel[:rows, SEL_R1].astype(jnp.int32),
                g_start[e2] + sel[:rows, SEL_R2].astype(jnp.int32))

    d1_p, d2_p = dests(sel_p, mp)
    d1_s, d2_s = dests(sel_s, ms)
    pad_start = g_start + count
    pad_count = padded - count
    xs = _dispatch(d1_p, d2_p, pad_start, pad_count, x_p, None, t_move, n_tiles * tm)
    xs = _dispatch(d1_s, d2_s, pad_start, pad_count, x_s, xs, ms, n_tiles * tm)
    h = _moe_up(tile_expert, n_active, xs, ffn_g, wg, wu, tm, tn)
    ys = _moe_down(tile_expert, n_active, h, wd, tm)
    xo_p, xn_p = _combine(d1_p, d2_p, ys, x_p, sel_p, next_g, t_comb, emit_x, norm_dtype)
    xo_s, xn_s = _combine(d1_s, d2_s, ys, x_s, sel_s, next_g, ms, emit_x, norm_dtype)
    return xo_p, xn_p, xo_s, xn_s


def _forget_kernel(x_ref, w_ref, b_ref, lf_ref, *c_ref, cumsum):
    z = _dot(x_ref[...], w_ref[...]) + b_ref[...]
    lf = jnp.minimum(z, 0.0) - jnp.log1p(jnp.exp(-jnp.abs(z)))
    lf_ref[...] = lf
    if cumsum:
        r = lax.broadcasted_iota(jnp.int32, (LANES, LANES), 0)
        c = lax.broadcasted_iota(jnp.int32, (LANES, LANES), 1)
        tri = (c <= r).astype(F32)
        carry = jnp.zeros((1, LANES), F32)
        for blk in range(lf.shape[0] // LANES):
            rows = slice(blk * LANES, (blk + 1) * LANES)
            cb = jnp.dot(tri, lf[rows, :], preferred_element_type=F32,
                         precision=lax.Precision.HIGHEST) + carry
            carry = cb[LANES - 1:LANES, :]
            c_ref[0][rows, :] = cb


def _forget(xn, wf_pad, bf_pad, rows, cumsum):
    m, d = xn.shape
    n = m // rows
    out_specs = [pl.BlockSpec((rows, LANES), lambda b: (b, 0))]
    out_shape = [jax.ShapeDtypeStruct((m, LANES), F32)]
    if cumsum:
        out_specs.append(pl.BlockSpec((rows, LANES), lambda b: (b, 0)))
        out_shape.append(jax.ShapeDtypeStruct((m, LANES), F32))
    return pl.pallas_call(
        functools.partial(_forget_kernel, cumsum=cumsum), grid=(n,),
        in_specs=[pl.BlockSpec((rows, d), lambda b: (b, 0)),
                  pl.BlockSpec((d, LANES), lambda b: (0, 0)),
                  pl.BlockSpec((1, LANES), lambda b: (0, 0))],
        out_specs=out_specs, out_shape=out_shape,
        compiler_params=_cp(1), name="forget_gate")(xn, wf_pad, bf_pad)


def _group_norm_store(yc, g_ref, o_ref):
    for gi in range(yc.shape[1] // HEAD_DIM):
        sl = slice(gi * HEAD_DIM, (gi + 1) * HEAD_DIM)
        yg = yc[:, sl]
        ms = jnp.mean(yg * yg, axis=-1, keepdims=True)
        o_ref[:, sl] = (yg * lax.rsqrt(ms + RMS_EPS) * g_ref[:, sl]).astype(o_ref.dtype)


def _conv_kernel(x_ref, wb_ref, wc_ref, wh_ref, cw_ref, g_ref, o_ref, tail_ref, halo_scr,
                 *, tiles_per_seq):
    i, j = pl.program_id(0), pl.program_id(1)
    x = x_ref[...]
    gate_b = _dot(x, wb_ref[...])
    u = _dot(x, wc_ref[...]) * _dot(x, wh_ref[...])
    tm = u.shape[0]

    @pl.when(i % tiles_per_seq == 0)
    def _():
        halo_scr[j] = jnp.zeros(halo_scr.shape[1:], F32)

    prev = halo_scr[j]
    row = lax.broadcasted_iota(jnp.int32, u.shape, 0)
    u1 = jnp.where(row == 0, prev[1:2, :], pltpu.roll(u, 1, axis=0))
    u2 = jnp.where(row == 0, prev[0:1, :],
                   jnp.where(row == 1, prev[1:2, :], pltpu.roll(u, 2, axis=0)))
    cw = cw_ref[...]
    y = u2 * cw[0:1, :] + u1 * cw[1:2, :] + u * cw[2:3, :]
    _group_norm_store(gate_b * y, g_ref, o_ref)
    tail = u[tm - (CONV_W - 1):tm, :]
    halo_scr[j] = tail
    tail_ref[0] = tail


def _conv_branch(xn, wb, wc, wh, cw, g, seq, tm, tc):
    m, d = xn.shape
    n_i, n_j = m // tm, CONV_DIM // tc
    col = lambda i, j: (0, j)
    return pl.pallas_call(
        functools.partial(_conv_kernel, tiles_per_seq=seq // tm), grid=(n_i, n_j),
        in_specs=[pl.BlockSpec((tm, d), lambda i, j: (i, 0)),
                  pl.BlockSpec((d, tc), col), pl.BlockSpec((d, tc), col),
                  pl.BlockSpec((d, tc), col), pl.BlockSpec((CONV_W, tc), col),
                  pl.BlockSpec((1, tc), col)],
        out_specs=[pl.BlockSpec((tm, tc), lambda i, j: (i, j)),
                   pl.BlockSpec((1, CONV_W - 1, tc), lambda i, j: (i, 0, j))],
        out_shape=[jax.ShapeDtypeStruct((m, CONV_DIM), BF16),
                   jax.ShapeDtypeStruct((n_i, CONV_W - 1, CONV_DIM), F32)],
        scratch_shapes=[pltpu.VMEM((n_j, CONV_W - 1, tc), F32)],
        compiler_params=_cp(2), name="conv_branch")(xn, wb, wc, wh, cw, g.reshape(1, CONV_DIM))


def _conv_step_kernel(x_ref, wb_ref, wc_ref, wh_ref, cw_ref, g_ref, p0_ref, p1_ref, o_ref, u_ref):
    x = x_ref[...]
    gate_b = _dot(x, wb_ref[...])
    u = _dot(x, wc_ref[...]) * _dot(x, wh_ref[...])
    cw = cw_ref[...]
    y = p0_ref[...] * cw[0:1, :] + p1_ref[...] * cw[1:2, :] + u * cw[2:3, :]
    _group_norm_store(gate_b * y, g_ref, o_ref)
    u_ref[...] = u


def _conv_step(xn, wb, wc, wh, cw, g, p0, p1, tc):
    m, d = xn.shape
    col = lambda j: (0, j)
    return pl.pallas_call(
        _conv_step_kernel, grid=(CONV_DIM // tc,),
        in_specs=[pl.BlockSpec((m, d), lambda j: (0, 0)),
                  pl.BlockSpec((d, tc), col), pl.BlockSpec((d, tc), col),
                  pl.BlockSpec((d, tc), col), pl.BlockSpec((CONV_W, tc), col),
                  pl.BlockSpec((1, tc), col), pl.BlockSpec((m, tc), col),
                  pl.BlockSpec((m, tc), col)],
        out_specs=[pl.BlockSpec((m, tc), col), pl.BlockSpec((m, tc), col)],
        out_shape=[jax.ShapeDtypeStruct((m, CONV_DIM), BF16),
                   jax.ShapeDtypeStruct((m, CONV_DIM), F32)],
        compiler_params=_cp(1), name="conv_step")(xn, wb, wc, wh, cw, g.reshape(1, CONV_DIM), p0, p1)


def _prompt_step(q_ref, k_ref, v_ref, c_ref, m_scr, l_scr, acc_scr, diagonal):
    tq, tk = q_ref.shape[1], k_ref.shape[2]
    for h in range(ATT_HEADS):
        sl = slice(h * HEAD_DIM, (h + 1) * HEAD_DIM)
        k = k_ref[0, 0, :, sl].astype(BF16)
        vt = v_ref[0, 0, :, sl].T.astype(BF16)
        c_keys = c_ref[0, :, h:h + 1]
        for s0 in range(0, tq, Q_STRIP):
            qs = slice(s0, s0 + Q_STRIP)
            nkeys = min(tk, s0 + Q_STRIP) if diagonal else tk
            st = lax.dot_general(k[:nkeys], q_ref[0, qs, sl], (((1,), (1,)), ((), ())),
                                 preferred_element_type=F32)
            st = st * ATT_SCALE - c_keys[:nkeys]
            if diagonal:
                key = lax.broadcasted_iota(jnp.int32, (nkeys, Q_STRIP), 0)
                qry = lax.broadcasted_iota(jnp.int32, (nkeys, Q_STRIP), 1) + s0
                st = jnp.where(key <= qry, st, NEG_BIG)
            m_prev = m_scr[h, :, qs]
            m_new = jnp.maximum(m_prev, jnp.max(st, axis=0, keepdims=True))
            alpha = jnp.exp(m_prev - m_new)
            p = jnp.exp(st - m_new)
            l_scr[h, :, qs] = alpha * l_scr[h, :, qs] + jnp.sum(p, axis=0, keepdims=True)
            acc_scr[h, :, qs] = alpha * acc_scr[h, :, qs] + _dot(vt[:, :nkeys], p.astype(BF16))
            m_scr[h, :, qs] = m_new


def _prompt_finish(g_ref, o_ref, l_scr, acc_scr):
    for h in range(ATT_HEADS):
        sl = slice(h * HEAD_DIM, (h + 1) * HEAD_DIM)
        ot = acc_scr[h] / l_scr[h]
        ms = jnp.mean(ot * ot, axis=0, keepdims=True)
        o = (ot * lax.rsqrt(ms + RMS_EPS)).T
        o_ref[0, :, sl] = (o * g_ref[:, sl]).astype(o_ref.dtype)


def _decode_init(lfn_ref, m_scr, l_scr, acc_scr, carry_scr):
    m_scr[...] = jnp.full(m_scr.shape, NEG_BIG, F32)
    l_scr[...] = jnp.zeros_like(l_scr)
    acc_scr[...] = jnp.zeros_like(acc_scr)
    carry_scr[...] = lfn_ref[0]


def _decode_chunk(q_ref, k_refs, v_refs, lf_refs, m_scr, l_scr, acc_scr, carry_scr, lf_scr):
    npg = PAGES_PER_STEP
    width = lf_scr.shape[1]
    lane = lax.broadcasted_iota(jnp.int32, (npg, width), 1)
    for p in range(npg):
        lf_scr[p:p + 1, :] = lf_refs[p][0, 0]
    lf = lf_scr[...]
    y = lf
    sh = ATT_HEADS
    while sh < width:
        y = y + jnp.where(lane < width - sh, pltpu.roll(y, width - sh, axis=1), 0.0)
        sh *= 2
    excl = y - lf
    z = jnp.where(lane < ATT_HEADS, y, 0.0)
    sh = ATT_HEADS
    while sh < width:
        z = z + pltpu.roll(z, sh, axis=1)
        sh *= 2
    carry = carry_scr[...]
    bias = [None] * npg
    for p in reversed(range(npg)):
        bias[p] = carry + excl[p:p + 1, :]
        carry = carry + z[p:p + 1, :]
    carry_scr[...] = carry

    q = q_ref[0]
    hrow = lax.broadcasted_iota(jnp.int32, (ATT_HEADS, width), 0)
    hlane = lax.broadcasted_iota(jnp.int32, (ATT_HEADS, width), 1)
    own = jnp.bitwise_and(hlane, ATT_HEADS - 1) == hrow
    s_all = []
    for p in range(npg):
        kp = k_refs[p][0, 0].astype(BF16)
        s = lax.dot_general(q, kp, (((1,), (1,)), ((), ())), preferred_element_type=F32)
        s_all.append(jnp.where(own, s * ATT_SCALE + bias[p], NEG_BIG))
    m_prev = m_scr[...]
    m_new = m_prev
    for s in s_all:
        m_new = jnp.maximum(m_new, jnp.max(s, axis=1, keepdims=True))
    alpha = jnp.exp(m_prev - m_new)
    l_new = alpha * l_scr[...]
    acc = alpha * acc_scr[...]
    for p in range(npg):
        pr = jnp.exp(s_all[p] - m_new)
        l_new = l_new + jnp.sum(pr, axis=1, keepdims=True)
        acc = acc + _dot(pr.astype(BF16), v_refs[p][0, 0].astype(BF16))
    m_scr[...] = m_new
    l_scr[...] = l_new
    acc_scr[...] = acc


def _decode_finish(q_ref, kn_ref, vn_ref, g_ref, o_ref, m_scr, l_scr, acc_scr):
    q = q_ref[0].astype(F32)
    kn = kn_ref[0].astype(BF16).astype(F32)
    vn = vn_ref[0].astype(BF16).astype(F32)
    m_past = m_scr[...]
    s_new = jnp.sum(q * kn, axis=1, keepdims=True) * ATT_SCALE
    m_fin = jnp.maximum(m_past, s_new)
    a = jnp.exp(m_past - m_fin)
    pn = jnp.exp(s_new - m_fin)
    o = (a * acc_scr[...] + pn * vn) / (a * l_scr[...] + pn)
    ms = jnp.mean(o * o, axis=-1, keepdims=True)
    o_ref[0] = (o * lax.rsqrt(ms + RMS_EPS) * g_ref[...]).astype(o_ref.dtype)


def _attention_kernel(pt_ref, q_ref, k_ref, v_ref, c_ref, gp_ref, qs_ref, kn_ref, vn_ref, lfn_ref,
                      gs_ref, *rest):
    del pt_ref
    npg = PAGES_PER_STEP
    k_refs, v_refs, lf_refs = rest[:npg], rest[npg:2 * npg], rest[2 * npg:3 * npg]
    (o_ref, os_ref, m_scr, l_scr, acc_scr,
     dm_scr, dl_scr, dacc_scr, carry_scr, lf_scr) = rest[3 * npg:]
    qi, ki = pl.program_id(1), pl.program_id(2)

    def decode():
        _decode_chunk(qs_ref, k_refs, v_refs, lf_refs, dm_scr, dl_scr, dacc_scr, carry_scr, lf_scr)

    @pl.when(ki == 0)
    def _():
        m_scr[...] = jnp.full(m_scr.shape, NEG_BIG, F32)
        l_scr[...] = jnp.zeros_like(l_scr)
        acc_scr[...] = jnp.zeros_like(acc_scr)
        _decode_init(lfn_ref, dm_scr, dl_scr, dacc_scr, carry_scr)

    @pl.when(ki < qi)
    def _():
        _prompt_step(q_ref, k_ref, v_ref, c_ref, m_scr, l_scr, acc_scr, False)
        decode()

    @pl.when(ki == qi)
    def _():
        _prompt_step(q_ref, k_ref, v_ref, c_ref, m_scr, l_scr, acc_scr, True)
        _prompt_finish(gp_ref, o_ref, l_scr, acc_scr)
        decode()

    @pl.when(ki > qi)
    def _():
        decode()

    @pl.when(ki == pl.num_programs(2) - 1)
    def _():
        _decode_finish(qs_ref, kn_ref, vn_ref, gs_ref, os_ref, dm_scr, dl_scr, dacc_scr)


def _attention(layer, q, k, v, c, g, page_table, qs, kn, vn, lfn, cache_k, cache_v, cache_logf,
               blk):
    n, s, _ = q.shape
    tq = tk = blk
    nq = nk = s // blk
    n_dec, n_pages = page_table.shape
    depth, n_pool, page, heads, hd = cache_k.shape
    width = page * heads
    assert n_dec == n * nq and n_pages == nk * PAGES_PER_STEP and blk % Q_STRIP == 0
    ck = cache_k.reshape(depth, n_pool, width, hd)
    cv = cache_v.reshape(depth, n_pool, width, hd)
    cl = cache_logf.reshape(depth, n_pool, 1, width)

    def kv_block(qi, ki):
        return jnp.minimum(ki, (tq // tk) * (qi + 1) - 1)

    def page_map(p):
        return lambda b, qi, ki, pt: (layer, pt[b * nq + qi, (nk - 1 - ki) * PAGES_PER_STEP + p], 0, 0)

    kv_map = lambda b, qi, ki, pt: (layer, b, kv_block(qi, ki), 0)
    q_map = lambda b, qi, ki, pt: (b, qi, 0)
    seq_map = lambda b, qi, ki, pt: (b * nq + qi, 0, 0)
    fixed = lambda b, qi, ki, pt: (0, 0)
    in_specs = [pl.BlockSpec((1, tq, ATT_DIM), q_map),
                pl.BlockSpec((1, 1, tk, ATT_DIM), kv_map), pl.BlockSpec((1, 1, tk, ATT_DIM), kv_map),
                pl.BlockSpec((1, tk, LANES), lambda b, qi, ki, pt: (b, kv_block(qi, ki), 0)),
                pl.BlockSpec((1, ATT_DIM), fixed),
                pl.BlockSpec((1, heads, hd), seq_map), pl.BlockSpec((1, heads, hd), seq_map),
                pl.BlockSpec((1, heads, hd), seq_map), pl.BlockSpec((1, 1, width), seq_map),
                pl.BlockSpec((heads, hd), fixed)]
    in_specs += [pl.BlockSpec((1, 1, width, hd), page_map(p)) for p in range(PAGES_PER_STEP)]
    in_specs += [pl.BlockSpec((1, 1, width, hd), page_map(p)) for p in range(PAGES_PER_STEP)]
    in_specs += [pl.BlockSpec((1, 1, 1, width), page_map(p)) for p in range(PAGES_PER_STEP)]
    grid_spec = pltpu.PrefetchScalarGridSpec(
        num_scalar_prefetch=1, grid=(n, nq, nk), in_specs=in_specs,
        out_specs=[pl.BlockSpec((1, tq, ATT_DIM), q_map), pl.BlockSpec((1, heads, hd), seq_map)],
        scratch_shapes=[pltpu.VMEM((ATT_HEADS, 1, tq), F32), pltpu.VMEM((ATT_HEADS, 1, tq), F32),
                        pltpu.VMEM((ATT_HEADS, HEAD_DIM, tq), F32),
                        pltpu.VMEM((heads, 1), F32), pltpu.VMEM((heads, 1), F32),
                        pltpu.VMEM((heads, hd), F32), pltpu.VMEM((1, width), F32),
                        pltpu.VMEM((PAGES_PER_STEP, width), F32)])
    return pl.pallas_call(
        _attention_kernel, grid_spec=grid_spec,
        out_shape=[jax.ShapeDtypeStruct((n, s, ATT_DIM), BF16),
                   jax.ShapeDtypeStruct((n_dec, heads, hd), BF16)],
        compiler_params=_cp(3), name="attention")(
            page_table, q, k, v, c, g.reshape(1, ATT_DIM), qs, kn, vn, lfn, g.reshape(heads, hd),
            *([ck] * PAGES_PER_STEP), *([cv] * PAGES_PER_STEP), *([cl] * PAGES_PER_STEP))


def _pad_lanes(w):
    return jnp.pad(w, ((0, 0), (0, LANES - w.shape[1])))


def kernel(x_prompt, x_sample, cache_k, cache_v, cache_logf, state_conv, page_table, norm_mix_g,
           w_in, b_forget, conv_w, g_att_out, g_conv_out, w_out, norm_ffn_g, dense_w_gate,
           dense_w_up, dense_w_down, moe_router, moe_w_gate, moe_w_up, moe_w_down, final_norm_g):
    n, seq, d = x_prompt.shape
    n_dec, dec_seq, _ = x_sample.shape
    assert dec_seq == 1 and d == D_MODEL
    depth = w_in.shape[0]
    mp = n * seq
    tm_p, tm_s = 512, n_dec

    xp = x_prompt.reshape(mp, d)
    xs = x_sample.reshape(n_dec, d)
    xnp = _rmsnorm(xp, norm_mix_g[0], tm_p)
    xns = _rmsnorm(xs, norm_mix_g[0], tm_s)

    outs = {key: [] for key in ("fp", "cp", "ks", "vs", "fs", "cs")}
    yp = ys = k_all = v_all = None
    for l in range(depth):
        wl = w_in[l]
        o = 0
        wq = wl[:, o:o + ATT_DIM].astype(BF16); o += ATT_DIM
        wk = wl[:, o:o + ATT_DIM].astype(BF16); o += ATT_DIM
        wv = wl[:, o:o + ATT_DIM].astype(BF16); o += ATT_DIM
        wf = _pad_lanes(wl[:, o:o + ATT_HEADS]).astype(BF16); o += ATT_HEADS
        wb = wl[:, o:o + CONV_DIM].astype(BF16); o += CONV_DIM
        wc = wl[:, o:o + CONV_DIM].astype(BF16); o += CONV_DIM
        wh = wl[:, o:o + CONV_DIM].astype(BF16)
        bf = _pad_lanes(b_forget[l].reshape(1, ATT_HEADS))
        wo_a = w_out[l][:ATT_DIM].astype(BF16)
        wo_c = w_out[l][ATT_DIM:].astype(BF16)

        q_p = _mm(xnp, wq, BF16, 1024, 1024, "q_proj")
        k_all = _mm_layer(xnp, wk, k_all, l, depth, 1024, 1024, "k_proj")
        v_all = _mm_layer(xnp, wv, v_all, l, depth, 1024, 1024, "v_proj")
        lf_p, c_p = _forget(xnp, wf, bf, seq, True)
        conv_p, tails = _conv_branch(xnp, wb, wc, wh, conv_w[l], g_conv_out[l], seq, 512, 1024)
        outs["fp"].append(lf_p[:, :ATT_HEADS].reshape(n, seq, ATT_HEADS))
        tiles_per_seq = seq // 512
        outs["cp"].append(tails[tiles_per_seq - 1::tiles_per_seq])

        q_s = _mm(xns, wq, BF16, tm_s, 512, "q_proj_s")
        k_s = _mm(xns, wk, F32, tm_s, 512, "k_proj_s")
        v_s = _mm(xns, wv, F32, tm_s, 512, "v_proj_s")
        lf_s = _forget(xns, wf, bf, n_dec, False)[0]
        st = state_conv[l]
        conv_s, u_s = _conv_step(xns, wb, wc, wh, conv_w[l], g_conv_out[l], st[:, 0], st[:, 1], 512)
        lfn = jnp.tile(lf_s[:, :ATT_HEADS], (1, cache_k.shape[2])).reshape(n_dec, 1, -1)

        att_p, att_s = _attention(
            l, q_p.reshape(n, seq, ATT_DIM), k_all.reshape(depth, n, seq, ATT_DIM),
            v_all.reshape(depth, n, seq, ATT_DIM), c_p.reshape(n, seq, LANES), g_att_out[l],
            page_table, q_s.reshape(n_dec, ATT_HEADS, HEAD_DIM),
            k_s.reshape(n_dec, ATT_HEADS, HEAD_DIM), v_s.reshape(n_dec, ATT_HEADS, HEAD_DIM), lfn,
            cache_k, cache_v, cache_logf, 512)
        xp, hnp = _outproj(att_p.reshape(mp, ATT_DIM), conv_p, wo_a, wo_c, xp, norm_ffn_g[l], tm_p)
        xs, hns = _outproj(att_s.reshape(n_dec, ATT_DIM), conv_s, wo_a, wo_c, xs, norm_ffn_g[l], tm_s)
        outs["ks"].append(k_s.reshape(n_dec, 1, ATT_HEADS, HEAD_DIM))
        outs["vs"].append(v_s.reshape(n_dec, 1, ATT_HEADS, HEAD_DIM))
        outs["fs"].append(lf_s[:, :ATT_HEADS].reshape(n_dec, 1, ATT_HEADS))
        outs["cs"].append(jnp.stack([st[:, 1], u_s], axis=1))

        last = l == depth - 1
        next_g = final_norm_g if last else norm_mix_g[l + 1]
        next_dtype = F32 if last else BF16
        i = l // 2
        if l % 2 == 0:
            wg = dense_w_gate[i].astype(BF16)
            wu = dense_w_up[i].astype(BF16)
            wd = dense_w_down[i].astype(BF16)
            hp = _swiglu(hnp, wg, wu, 1024, 512)
            xp, nxp = _down(hp, wd, xp, next_g, 256, not last, next_dtype)
            hs = _swiglu(hns, wg, wu, tm_s, 512)
            xs, nxs = _down(hs, wd, xs, next_g, tm_s, not last, next_dtype)
        else:
            xp, nxp, xs, nxs = _moe_ffn(
                xp, hnp, xs, hns, norm_ffn_g[l], moe_router[i], moe_w_gate[i].astype(BF16),
                moe_w_up[i].astype(BF16), moe_w_down[i].astype(BF16), next_g, not last, next_dtype)
        if last:
            yp, ys = nxp, nxs
        else:
            xnp, xns = nxp, nxs

    return (yp.reshape(n, seq, d), ys.reshape(n_dec, 1, d),
            k_all.reshape(depth, n, seq, ATT_HEADS, HEAD_DIM),
            v_all.reshape(depth, n, seq, ATT_HEADS, HEAD_DIM),
            jnp.stack(outs["fp"]), jnp.stack(outs["cp"]),
            jnp.stack(outs["ks"]), jnp.stack(outs["vs"]), jnp.stack(outs["fs"]), jnp.stack(outs["cs"]))
```

```python
import functools

import jax
import jax.numpy as jnp
from jax import lax
from jax.experimental import pallas as pl
from jax.experimental.pallas import tpu as pltpu

D_MODEL = 2048
HEAD_DIM = 128
ATT_HEADS = 8
ATT_DIM = ATT_HEADS * HEAD_DIM
CONV_DIM = D_MODEL - ATT_DIM
CONV_W = 3
N_EXPERTS = 8
TOP_K = 2
ATT_SCALE = HEAD_DIM ** -0.5
RMS_EPS = 1e-6
LOG2_E = 1.4426950408889634
LANES = 128
NEG_BIG = -1e30
PAGES_PER_STEP = 16
Q_STRIP = 128
VMEM_LIMIT = 56 * 1024 * 1024

F32 = jnp.float32
BF16 = jnp.bfloat16


def _cp(n_axes):
    return pltpu.CompilerParams(dimension_semantics=("arbitrary",) * n_axes,
                                vmem_limit_bytes=VMEM_LIMIT)


def _rms(x, g):
    return x * lax.rsqrt(jnp.mean(x * x, axis=-1, keepdims=True) + RMS_EPS) * g


def _dot(a, b):
    return jnp.dot(a, b, preferred_element_type=F32)


def _rmsnorm_kernel(x_ref, g_ref, o_ref):
    o_ref[...] = _rms(x_ref[...], g_ref[...]).astype(o_ref.dtype)


def _rmsnorm(x, g, tm, out_dtype=BF16):
    m, d = x.shape
    return pl.pallas_call(
        _rmsnorm_kernel, grid=(m // tm,),
        in_specs=[pl.BlockSpec((tm, d), lambda i: (i, 0)),
                  pl.BlockSpec((1, d), lambda i: (0, 0))],
        out_specs=pl.BlockSpec((tm, d), lambda i: (i, 0)),
        out_shape=jax.ShapeDtypeStruct((m, d), out_dtype),
        compiler_params=_cp(1), name="rmsnorm")(x, g.reshape(1, d))


def _mm_kernel(x_ref, w_ref, *rest):
    o_ref = rest[-1]
    o_ref[...] = _dot(x_ref[...], w_ref[...]).astype(o_ref.dtype).reshape(o_ref.shape)


def _mm_layer(x, w, stacked, layer, depth, tm, tn, name):
    m, k = x.shape
    n = w.shape[1]
    in_specs = [pl.BlockSpec((tm, k), lambda i, j: (i, 0)),
                pl.BlockSpec((k, tn), lambda i, j: (0, j))]
    args, aliases = [x, w], {}
    if stacked is not None:
        in_specs.append(pl.BlockSpec(memory_space=pl.ANY))
        args.append(stacked)
        aliases = {2: 0}
    return pl.pallas_call(
        _mm_kernel, grid=(m // tm, n // tn), in_specs=in_specs,
        out_specs=pl.BlockSpec((1, tm, tn), lambda i, j: (layer, i, j)),
        out_shape=jax.ShapeDtypeStruct((depth, m, n), F32), input_output_aliases=aliases,
        compiler_params=_cp(2), name=name)(*args)


def _mm(x, w, out_dtype, tm, tn, name):
    m, k = x.shape
    n = w.shape[1]
    return pl.pallas_call(
        _mm_kernel, grid=(m // tm, n // tn),
        in_specs=[pl.BlockSpec((tm, k), lambda i, j: (i, 0)),
                  pl.BlockSpec((k, tn), lambda i, j: (0, j))],
        out_specs=pl.BlockSpec((tm, tn), lambda i, j: (i, j)),
        out_shape=jax.ShapeDtypeStruct((m, n), out_dtype),
        compiler_params=_cp(2), name=name)(x, w)


def _swiglu_kernel(x_ref, wg_ref, wu_ref, o_ref):
    x = x_ref[...]
    a = _dot(x, wg_ref[...])
    b = _dot(x, wu_ref[...])
    o_ref[...] = (a * (1.0 / (1.0 + jnp.exp(-a))) * b).astype(o_ref.dtype)


def _swiglu(x, wg, wu, tm, tn):
    m, k = x.shape
    n = wg.shape[1]
    return pl.pallas_call(
        _swiglu_kernel, grid=(m // tm, n // tn),
        in_specs=[pl.BlockSpec((tm, k), lambda i, j: (i, 0)),
                  pl.BlockSpec((k, tn), lambda i, j: (0, j)),
                  pl.BlockSpec((k, tn), lambda i, j: (0, j))],
        out_specs=pl.BlockSpec((tm, tn), lambda i, j: (i, j)),
        out_shape=jax.ShapeDtypeStruct((m, n), BF16),
        compiler_params=_cp(2), name="swiglu_up")(x, wg, wu)


def _down_kernel(h_ref, w_ref, res_ref, g_ref, *outs, emit_x):
    xo = res_ref[...] + _dot(h_ref[...], w_ref[...])
    if emit_x:
        outs[0][...] = xo
    outs[-1][...] = _rms(xo, g_ref[...]).astype(outs[-1].dtype)


def _down(h, w, res, norm_g, tm, emit_x, norm_dtype):
    m, kdim = h.shape
    d = w.shape[1]
    row = lambda i: (i, 0)
    fixed = lambda i: (0, 0)
    out_specs = [pl.BlockSpec((tm, d), row)]
    out_shape = [jax.ShapeDtypeStruct((m, d), norm_dtype)]
    if emit_x:
        out_specs.insert(0, pl.BlockSpec((tm, d), row))
        out_shape.insert(0, jax.ShapeDtypeStruct((m, d), F32))
    outs = pl.pallas_call(
        functools.partial(_down_kernel, emit_x=emit_x), grid=(m // tm,),
        in_specs=[pl.BlockSpec((tm, kdim), row),
                  pl.BlockSpec((kdim, d), fixed, pipeline_mode=pl.Buffered(1)),
                  pl.BlockSpec((tm, d), row), pl.BlockSpec((1, d), fixed)],
        out_specs=out_specs, out_shape=out_shape,
        compiler_params=_cp(1), name="ffn_down")(h, w, res, norm_g.reshape(1, d))
    return (outs[0], outs[1]) if emit_x else (None, outs[0])


def _outproj_kernel(a_ref, c_ref, wa_ref, wc_ref, res_ref, g_ref, xo_ref, xn_ref):
    xo = res_ref[...] + (_dot(a_ref[...], wa_ref[...]) + _dot(c_ref[...], wc_ref[...]))
    xo_ref[...] = xo
    xn_ref[...] = _rms(xo, g_ref[...]).astype(xn_ref.dtype)


def _outproj(a, c, wa, wc, res, g, tm):
    m, d = res.shape
    ka, kc = a.shape[1], c.shape[1]
    row = lambda i: (i, 0)
    fixed = lambda i: (0, 0)
    return pl.pallas_call(
        _outproj_kernel, grid=(m // tm,),
        in_specs=[pl.BlockSpec((tm, ka), row), pl.BlockSpec((tm, kc), row),
                  pl.BlockSpec((ka, d), fixed), pl.BlockSpec((kc, d), fixed),
                  pl.BlockSpec((tm, d), row), pl.BlockSpec((1, d), fixed)],
        out_specs=[pl.BlockSpec((tm, d), row), pl.BlockSpec((tm, d), row)],
        out_shape=[jax.ShapeDtypeStruct((m, d), F32), jax.ShapeDtypeStruct((m, d), BF16)],
        compiler_params=_cp(1), name="out_proj")(a, c, wa, wc, res, g.reshape(1, d))


SEL_E1, SEL_E2, SEL_G1, SEL_G2, SEL_R1, SEL_R2 = range(6)


def _router_kernel(x_ref, r_ref, cin_ref, sel_ref, cnt_ref, carry_scr, *, n_valid):
    i = pl.program_id(0)
    tm = x_ref.shape[0]

    @pl.when(i == 0)
    def _():
        carry_scr[...] = cin_ref[...]

    logits = _dot(x_ref[...], r_ref[...])
    lane = lax.broadcasted_iota(jnp.int32, logits.shape, 1).astype(F32)
    lg = jnp.where(lane < N_EXPERTS, logits, -jnp.inf)
    m1 = jnp.max(lg, axis=1, keepdims=True)
    i1 = jnp.min(jnp.where(lg == m1, lane, float(LANES)), axis=1, keepdims=True)
    lg2 = jnp.where(lane == i1, -jnp.inf, lg)
    m2 = jnp.max(lg2, axis=1, keepdims=True)
    i2 = jnp.min(jnp.where(lg2 == m2, lane, float(LANES)), axis=1, keepdims=True)
    e2 = jnp.exp(m2 - m1)
    den = 1.0 + e2
    row = lax.broadcasted_iota(jnp.int32, logits.shape, 0) + i * tm
    hit = jnp.where(((lane == i1) | (lane == i2)) & (row < n_valid), 1.0, 0.0)
    r = lax.broadcasted_iota(jnp.int32, (tm, tm), 0)
    c = lax.broadcasted_iota(jnp.int32, (tm, tm), 1)
    before = (c < r).astype(BF16)
    rank = _dot(before, hit.astype(BF16)) + carry_scr[...]
    carry_scr[...] = carry_scr[...] + jnp.sum(hit, axis=0, keepdims=True)
    cnt_ref[...] = carry_scr[...]
    rank1 = jnp.sum(jnp.where(lane == i1, rank, 0.0), axis=1, keepdims=True)
    rank2 = jnp.sum(jnp.where(lane == i2, rank, 0.0), axis=1, keepdims=True)
    sel = jnp.zeros_like(logits)
    for idx, val in ((SEL_E1, i1), (SEL_E2, i2), (SEL_G1, 1.0 / den), (SEL_G2, e2 / den),
                     (SEL_R1, rank1), (SEL_R2, rank2)):
        sel = jnp.where(lane == idx, val, sel)
    sel_ref[...] = sel


def _router(hn, r_pad, count_in, tm, n_valid):
    m, d = hn.shape
    fixed = lambda i: (0, 0)
    return pl.pallas_call(
        functools.partial(_router_kernel, n_valid=n_valid), grid=(m // tm,),
        in_specs=[pl.BlockSpec((tm, d), lambda i: (i, 0)), pl.BlockSpec((d, LANES), fixed),
                  pl.BlockSpec((1, LANES), fixed)],
        out_specs=[pl.BlockSpec((tm, LANES), lambda i: (i, 0)), pl.BlockSpec((1, LANES), fixed)],
        out_shape=[jax.ShapeDtypeStruct((m, LANES), F32), jax.ShapeDtypeStruct((1, LANES), F32)],
        scratch_shapes=[pltpu.VMEM((1, LANES), F32)],
        compiler_params=_cp(1), name="router")(hn, r_pad, count_in)


def _dispatch_kernel(d1_ref, d2_ref, ps_ref, pc_ref, x_ref, *rest, first):
    xs_ref, zero_scr, sem = rest[-3:]
    tt = x_ref.shape[0]
    base = pl.program_id(0) * tt

    def row_copy(src_row, dst_row):
        return pltpu.make_async_copy(src_row, xs_ref.at[pl.ds(dst_row, 1), :], sem)

    def issue(r, carry):
        src = x_ref.at[pl.ds(r, 1), :]
        row_copy(src, d1_ref[base + r]).start()
        row_copy(src, d2_ref[base + r]).start(priority=1)
        return carry

    def drain(r, carry):
        row_copy(x_ref.at[pl.ds(0, 1), :], 0).wait()
        row_copy(x_ref.at[pl.ds(0, 1), :], 0).wait()
        return carry

    lax.fori_loop(0, tt, issue, 0, unroll=8)
    lax.fori_loop(0, tt, drain, 0, unroll=8)

    if first:
        @pl.when(pl.program_id(0) == 0)
        def _():
            zero_scr[...] = jnp.zeros_like(zero_scr)
            zrow = zero_scr.at[pl.ds(0, 1), :]
            for e in range(N_EXPERTS):
                def fill(r, carry, e=e):
                    row_copy(zrow, ps_ref[e] + r).start()
                    return carry

                def fill_done(r, carry):
                    row_copy(zrow, 0).wait()
                    return carry

                lax.fori_loop(0, pc_ref[e], fill, 0)
                lax.fori_loop(0, pc_ref[e], fill_done, 0)


def _dispatch(d1, d2, pad_start, pad_count, x, xs, tt, n_rows):
    m, d = x.shape
    first = xs is None
    in_specs = [pl.BlockSpec((tt, d), lambda i, *_: (i, 0))]
    args = [d1, d2, pad_start, pad_count, x]
    aliases = {}
    if not first:
        in_specs.append(pl.BlockSpec(memory_space=pl.ANY))
        args.append(xs)
        aliases = {5: 0}
    grid_spec = pltpu.PrefetchScalarGridSpec(
        num_scalar_prefetch=4, grid=(m // tt,), in_specs=in_specs,
        out_specs=pl.BlockSpec(memory_space=pl.ANY),
        scratch_shapes=[pltpu.VMEM((8, d), F32), pltpu.SemaphoreType.DMA(())])
    return pl.pallas_call(
        functools.partial(_dispatch_kernel, first=first), grid_spec=grid_spec,
        out_shape=jax.ShapeDtypeStruct((n_rows, d), F32), input_output_aliases=aliases,
        compiler_params=_cp(1), name="moe_dispatch")(*args)


def _moe_up_kernel(te_ref, na_ref, x_ref, g_ref, wg_ref, wu_ref, o_ref):
    del te_ref

    @pl.when(pl.program_id(1) < na_ref[0])
    def _():
        xn = _rms(x_ref[...], g_ref[...]).astype(BF16)
        a = _dot(xn, wg_ref[0])
        b = _dot(xn, wu_ref[0])
        o_ref[...] = (a * (1.0 / (1.0 + jnp.exp(-a))) * b).astype(o_ref.dtype)


def _moe_up(tile_expert, n_active, xs, g, wg, wu, tm, tn):
    r, d = xs.shape
    f = wg.shape[2]
    tile = lambda t, na: jnp.minimum(t, na[0] - 1)
    w_map = lambda j, t, te, na: (te[tile(t, na)], 0, j)
    grid_spec = pltpu.PrefetchScalarGridSpec(
        num_scalar_prefetch=2, grid=(f // tn, r // tm),
        in_specs=[pl.BlockSpec((tm, d), lambda j, t, te, na: (tile(t, na), 0)),
                  pl.BlockSpec((1, d), lambda j, t, te, na: (0, 0)),
                  pl.BlockSpec((1, d, tn), w_map), pl.BlockSpec((1, d, tn), w_map)],
        out_specs=pl.BlockSpec((tm, tn), lambda j, t, te, na: (tile(t, na), j)))
    return pl.pallas_call(
        _moe_up_kernel, grid_spec=grid_spec, out_shape=jax.ShapeDtypeStruct((r, f), BF16),
        compiler_params=_cp(2), name="moe_up")(tile_expert, n_active, xs, g.reshape(1, d), wg, wu)


def _moe_down_kernel(te_ref, na_ref, h_ref, w_ref, o_ref):
    del te_ref

    @pl.when(pl.program_id(0) < na_ref[0])
    def _():
        o_ref[...] = _dot(h_ref[...], w_ref[0])


def _moe_down(tile_expert, n_active, h, wd, tm):
    r, f = h.shape
    d = wd.shape[2]
    tile = lambda t, na: jnp.minimum(t, na[0] - 1)
    grid_spec = pltpu.PrefetchScalarGridSpec(
        num_scalar_prefetch=2, grid=(r // tm,),
        in_specs=[pl.BlockSpec((tm, f), lambda t, te, na: (tile(t, na), 0)),
                  pl.BlockSpec((1, f, d), lambda t, te, na: (te[tile(t, na)], 0, 0))],
        out_specs=pl.BlockSpec((tm, d), lambda t, te, na: (tile(t, na), 0)))
    return pl.pallas_call(
        _moe_down_kernel, grid_spec=grid_spec, out_shape=jax.ShapeDtypeStruct((r, d), F32),
        compiler_params=_cp(1), name="moe_down")(tile_expert, n_active, h, wd)


def _combine_kernel(d1_ref, d2_ref, ys_ref, res_ref, sel_ref, g_ref, *rest, emit_x):
    y1_scr, y2_scr, sem = rest[-3:]
    outs = rest[:-3]
    tt = res_ref.shape[0]
    base = pl.program_id(0) * tt

    def row_copy(src_row, dst):
        return pltpu.make_async_copy(ys_ref.at[pl.ds(src_row, 1), :], dst, sem)

    def issue(r, carry):
        row_copy(d1_ref[base + r], y1_scr.at[pl.ds(r, 1), :]).start()
        row_copy(d2_ref[base + r], y2_scr.at[pl.ds(r, 1), :]).start(priority=1)
        return carry

    def drain(r, carry):
        row_copy(0, y1_scr.at[pl.ds(0, 1), :]).wait()
        row_copy(0, y2_scr.at[pl.ds(0, 1), :]).wait()
        return carry

    lax.fori_loop(0, tt, issue, 0, unroll=8)
    lax.fori_loop(0, tt, drain, 0, unroll=8)
    sel = sel_ref[...]
    g1 = sel[:, SEL_G1:SEL_G1 + 1]
    g2 = sel[:, SEL_G2:SEL_G2 + 1]
    xo = res_ref[...] + (g1 * y1_scr[...] + g2 * y2_scr[...])
    if emit_x:
        outs[0][...] = xo
    outs[-1][...] = _rms(xo, g_ref[...]).astype(outs[-1].dtype)


def _combine(d1, d2, ys, res, sel, g, tt, emit_x, norm_dtype):
    m, d = res.shape
    row = lambda i, *_: (i, 0)
    out_specs = [pl.BlockSpec((tt, d), row)]
    out_shape = [jax.ShapeDtypeStruct((m, d), norm_dtype)]
    if emit_x:
        out_specs.insert(0, pl.BlockSpec((tt, d), row))
        out_shape.insert(0, jax.ShapeDtypeStruct((m, d), F32))
    grid_spec = pltpu.PrefetchScalarGridSpec(
        num_scalar_prefetch=2, grid=(m // tt,),
        in_specs=[pl.BlockSpec(memory_space=pl.ANY), pl.BlockSpec((tt, d), row),
                  pl.BlockSpec((tt, LANES), row), pl.BlockSpec((1, d), lambda i, *_: (0, 0))],
        out_specs=out_specs,
        scratch_shapes=[pltpu.VMEM((tt, d), F32), pltpu.VMEM((tt, d), F32),
                        pltpu.SemaphoreType.DMA(())])
    outs = pl.pallas_call(
        functools.partial(_combine_kernel, emit_x=emit_x), grid_spec=grid_spec,
        out_shape=out_shape, compiler_params=_cp(1), name="moe_combine")(
            d1, d2, ys, res, sel, g.reshape(1, d))
    return (outs[0], outs[1]) if emit_x else (None, outs[0])


def _moe_ffn(x_p, hn_p, x_s, hn_s, ffn_g, router, wg, wu, wd, next_g, emit_x, norm_dtype,
             tm=512, tn=1408, t_route=512, t_move=512, t_comb=256):
    mp, d = x_p.shape
    ms = x_s.shape[0]
    n_tiles = (TOP_K * (mp + ms)) // tm + N_EXPERTS
    r_pad = _pad_lanes(router).astype(BF16)
    ms_pad = -(-ms // LANES) * LANES
    hn_s_pad = jnp.pad(hn_s, ((0, ms_pad - ms), (0, 0)))
    sel_p, cnt_p = _router(hn_p, r_pad, jnp.zeros((1, LANES), F32), t_route, mp)
    sel_s, cnt = _router(hn_s_pad, r_pad, cnt_p, ms_pad, ms)

    count = cnt[0, :N_EXPERTS].astype(jnp.int32)
    padded = (count + tm - 1) // tm * tm
    g_end = jnp.cumsum(padded)
    g_start = g_end - padded
    tile_expert = jnp.minimum(
        jnp.sum(jnp.arange(n_tiles)[:, None] >= (g_end // tm)[None, :], axis=1),
        N_EXPERTS - 1).astype(jnp.int32)
    n_active = (g_end[-1:] // tm).astype(jnp.int32)

    def dests(sel, rows):
        e1, e2 = sel[:rows, SEL_E1].astype(jnp.int32), sel[:rows, SEL_E2].astype(jnp.int32)
        return (g_start[e1] + sel[:rows, SEL_R1].astype(jnp.int32),
                g_start[e2] + sel[:rows, SEL_R2].astype(jnp.int32))

    d1_p, d2_p = dests(sel_p, mp)
    d1_s, d2_s = dests(sel_s, ms)
    pad_start = g_start + count
    pad_count = padded - count
    xs = _dispatch(d1_p, d2_p, pad_start, pad_count, x_p, None, t_move, n_tiles * tm)
    xs = _dispatch(d1_s, d2_s, pad_start, pad_count, x_s, xs, ms, n_tiles * tm)
    h = _moe_up(tile_expert, n_active, xs, ffn_g, wg, wu, tm, tn)
    ys = _moe_down(tile_expert, n_active, h, wd, tm)
    xo_p, xn_p = _combine(d1_p, d2_p, ys, x_p, sel_p, next_g, t_comb, emit_x, norm_dtype)
    xo_s, xn_s = _combine(d1_s, d2_s, ys, x_s, sel_s, next_g, ms, emit_x, norm_dtype)
    return xo_p, xn_p, xo_s, xn_s


def _forget_kernel(x_ref, w_ref, b_ref, lf_ref, *c_ref, cumsum):
    z = _dot(x_ref[...], w_ref[...]) + b_ref[...]
    lf = jnp.minimum(z, 0.0) - jnp.log1p(jnp.exp(-jnp.abs(z)))
    lf_ref[...] = lf
    if cumsum:
        r = lax.broadcasted_iota(jnp.int32, (LANES, LANES), 0)
        c = lax.broadcasted_iota(jnp.int32, (LANES, LANES), 1)
        tri = (c <= r).astype(F32)
        carry = jnp.zeros((1, LANES), F32)
        for blk in range(lf.shape[0] // LANES):
            rows = slice(blk * LANES, (blk + 1) * LANES)
            cb = jnp.dot(tri, lf[rows, :], preferred_element_type=F32,
                         precision=lax.Precision.HIGHEST) + carry
            carry = cb[LANES - 1:LANES, :]
            c_ref[0][rows, :] = cb


def _forget(xn, wf_pad, bf_pad, rows, cumsum):
    m, d = xn.shape
    n = m // rows
    out_specs = [pl.BlockSpec((rows, LANES), lambda b: (b, 0))]
    out_shape = [jax.ShapeDtypeStruct((m, LANES), F32)]
    if cumsum:
        out_specs.append(pl.BlockSpec((rows, LANES), lambda b: (b, 0)))
        out_shape.append(jax.ShapeDtypeStruct((m, LANES), F32))
    return pl.pallas_call(
        functools.partial(_forget_kernel, cumsum=cumsum), grid=(n,),
        in_specs=[pl.BlockSpec((rows, d), lambda b: (b, 0)),
                  pl.BlockSpec((d, LANES), lambda b: (0, 0)),
                  pl.BlockSpec((1, LANES), lambda b: (0, 0))],
        out_specs=out_specs, out_shape=out_shape,
        compiler_params=_cp(1), name="forget_gate")(xn, wf_pad, bf_pad)


def _group_norm_store(yc, g_ref, o_ref):
    for gi in range(yc.shape[1] // HEAD_DIM):
        sl = slice(gi * HEAD_DIM, (gi + 1) * HEAD_DIM)
        yg = yc[:, sl]
        ms = jnp.mean(yg * yg, axis=-1, keepdims=True)
        o_ref[:, sl] = (yg * lax.rsqrt(ms + RMS_EPS) * g_ref[:, sl]).astype(o_ref.dtype)


def _conv_kernel(x_ref, wb_ref, wc_ref, wh_ref, cw_ref, g_ref, o_ref, tail_ref, halo_scr,
                 *, tiles_per_seq):
    i, j = pl.program_id(0), pl.program_id(1)
    x = x_ref[...]
    gate_b = _dot(x, wb_ref[...])
    u = _dot(x, wc_ref[...]) * _dot(x, wh_ref[...])
    tm = u.shape[0]

    @pl.when(i % tiles_per_seq == 0)
    def _():
        halo_scr[j] = jnp.zeros(halo_scr.shape[1:], F32)

    prev = halo_scr[j]
    row = lax.broadcasted_iota(jnp.int32, u.shape, 0)
    u1 = jnp.where(row == 0, prev[1:2, :], pltpu.roll(u, 1, axis=0))
    u2 = jnp.where(row == 0, prev[0:1, :],
                   jnp.where(row == 1, prev[1:2, :], pltpu.roll(u, 2, axis=0)))
    cw = cw_ref[...]
    y = u2 * cw[0:1, :] + u1 * cw[1:2, :] + u * cw[2:3, :]
    _group_norm_store(gate_b * y, g_ref, o_ref)
    tail = u[tm - (CONV_W - 1):tm, :]
    halo_scr[j] = tail
    tail_ref[0] = tail


def _conv_branch(xn, wb, wc, wh, cw, g, seq, tm, tc):
    m, d = xn.shape
    n_i, n_j = m // tm, CONV_DIM // tc
    col = lambda i, j: (0, j)
    return pl.pallas_call(
        functools.partial(_conv_kernel, tiles_per_seq=seq // tm), grid=(n_i, n_j),
        in_specs=[pl.BlockSpec((tm, d), lambda i, j: (i, 0)),
                  pl.BlockSpec((d, tc), col), pl.BlockSpec((d, tc), col),
                  pl.BlockSpec((d, tc), col), pl.BlockSpec((CONV_W, tc), col),
                  pl.BlockSpec((1, tc), col)],
        out_specs=[pl.BlockSpec((tm, tc), lambda i, j: (i, j)),
                   pl.BlockSpec((1, CONV_W - 1, tc), lambda i, j: (i, 0, j))],
        out_shape=[jax.ShapeDtypeStruct((m, CONV_DIM), BF16),
                   jax.ShapeDtypeStruct((n_i, CONV_W - 1, CONV_DIM), F32)],
        scratch_shapes=[pltpu.VMEM((n_j, CONV_W - 1, tc), F32)],
        compiler_params=_cp(2), name="conv_branch")(xn, wb, wc, wh, cw, g.reshape(1, CONV_DIM))


def _conv_step_kernel(x_ref, wb_ref, wc_ref, wh_ref, cw_ref, g_ref, p0_ref, p1_ref, o_ref, u_ref):
    x = x_ref[...]
    gate_b = _dot(x, wb_ref[...])
    u = _dot(x, wc_ref[...]) * _dot(x, wh_ref[...])
    cw = cw_ref[...]
    y = p0_ref[...] * cw[0:1, :] + p1_ref[...] * cw[1:2, :] + u * cw[2:3, :]
    _group_norm_store(gate_b * y, g_ref, o_ref)
    u_ref[...] = u


def _conv_step(xn, wb, wc, wh, cw, g, p0, p1, tc):
    m, d = xn.shape
    col = lambda j: (0, j)
    return pl.pallas_call(
        _conv_step_kernel, grid=(CONV_DIM // tc,),
        in_specs=[pl.BlockSpec((m, d), lambda j: (0, 0)),
                  pl.BlockSpec((d, tc), col), pl.BlockSpec((d, tc), col),
                  pl.BlockSpec((d, tc), col), pl.BlockSpec((CONV_W, tc), col),
                  pl.BlockSpec((1, tc), col), pl.BlockSpec((m, tc), col),
                  pl.BlockSpec((m, tc), col)],
        out_specs=[pl.BlockSpec((m, tc), col), pl.BlockSpec((m, tc), col)],
        out_shape=[jax.ShapeDtypeStruct((m, CONV_DIM), BF16),
                   jax.ShapeDtypeStruct((m, CONV_DIM), F32)],
        compiler_params=_cp(1), name="conv_step")(xn, wb, wc, wh, cw, g.reshape(1, CONV_DIM), p0, p1)


def _prompt_step(q_ref, k_ref, v_ref, c_ref, m_scr, l_scr, acc_scr, diagonal):
    tq, tk = q_ref.shape[1], k_ref.shape[2]
    if diagonal:
        causal = (lax.broadcasted_iota(jnp.int32, (Q_STRIP, Q_STRIP), 0)
                  <= lax.broadcasted_iota(jnp.int32, (Q_STRIP, Q_STRIP), 1))
    for h in range(ATT_HEADS):
        sl = slice(h * HEAD_DIM, (h + 1) * HEAD_DIM)
        k = k_ref[0, 0, :, sl].astype(BF16)
        vt = v_ref[0, 0, :, sl].T.astype(BF16)
        c_keys = jnp.broadcast_to(c_ref[0, :, h:h + 1] * LOG2_E, (tk, Q_STRIP))
        for s0 in range(0, tq, Q_STRIP):
            qs = slice(s0, s0 + Q_STRIP)
            nkeys = min(tk, s0 + Q_STRIP) if diagonal else tk
            st = lax.dot_general(k[:nkeys], q_ref[0, qs, sl], (((1,), (1,)), ((), ())),
                                 preferred_element_type=F32)
            st = st * (ATT_SCALE * LOG2_E) - c_keys[:nkeys]
            if diagonal:
                tail = jnp.where(causal, st[s0:], NEG_BIG)
                st = tail if s0 == 0 else jnp.concatenate([st[:s0], tail], axis=0)
            m_prev = m_scr[h, :, qs]
            m_new = jnp.maximum(m_prev, jnp.max(st, axis=0, keepdims=True))
            alpha = jnp.exp2(m_prev - m_new)
            p = jnp.exp2(st - m_new)
            l_scr[h, :, qs] = alpha * l_scr[h, :, qs] + jnp.sum(p, axis=0, keepdims=True)
            acc_scr[h, :, qs] = alpha * acc_scr[h, :, qs] + _dot(vt[:, :nkeys], p.astype(BF16))
            m_scr[h, :, qs] = m_new


def _prompt_finish(g_ref, o_ref, l_scr, acc_scr):
    for h in range(ATT_HEADS):
        sl = slice(h * HEAD_DIM, (h + 1) * HEAD_DIM)
        ot = acc_scr[h] / l_scr[h]
        ms = jnp.mean(ot * ot, axis=0, keepdims=True)
        o = (ot * lax.rsqrt(ms + RMS_EPS)).T
        o_ref[0, :, sl] = (o * g_ref[:, sl]).astype(o_ref.dtype)


def _decode_init(lfn_ref, m_scr, l_scr, acc_scr, carry_scr):
    m_scr[...] = jnp.full(m_scr.shape, NEG_BIG, F32)
    l_scr[...] = jnp.zeros_like(l_scr)
    acc_scr[...] = jnp.zeros_like(acc_scr)
    carry_scr[...] = lfn_ref[0]


def _decode_chunk(q_ref, k_refs, v_refs, lf_refs, m_scr, l_scr, acc_scr, carry_scr, lf_scr):
    npg = PAGES_PER_STEP
    width = lf_scr.shape[1]
    lane = lax.broadcasted_iota(jnp.int32, (npg, width), 1)
    for p in range(npg):
        lf_scr[p:p + 1, :] = lf_refs[p][0, 0]
    lf = lf_scr[...]
    y = lf
    sh = ATT_HEADS
    while sh < width:
        y = y + jnp.where(lane < width - sh, pltpu.roll(y, width - sh, axis=1), 0.0)
        sh *= 2
    excl = y - lf
    z = jnp.where(lane < ATT_HEADS, y, 0.0)
    sh = ATT_HEADS
    while sh < width:
        z = z + pltpu.roll(z, sh, axis=1)
        sh *= 2
    carry = carry_scr[...]
    bias = [None] * npg
    for p in reversed(range(npg)):
        bias[p] = carry + excl[p:p + 1, :]
        carry = carry + z[p:p + 1, :]
    carry_scr[...] = carry

    q = q_ref[0]
    hrow = lax.broadcasted_iota(jnp.int32, (ATT_HEADS, width), 0)
    hlane = lax.broadcasted_iota(jnp.int32, (ATT_HEADS, width), 1)
    own = jnp.bitwise_and(hlane, ATT_HEADS - 1) == hrow
    s_all = []
    for p in range(npg):
        kp = k_refs[p][0, 0].astype(BF16)
        s = lax.dot_general(q, kp, (((1,), (1,)), ((), ())), preferred_element_type=F32)
        s_all.append(jnp.where(own, s * ATT_SCALE + bias[p], NEG_BIG))
    m_prev = m_scr[...]
    m_new = m_prev
    for s in s_all:
        m_new = jnp.maximum(m_new, jnp.max(s, axis=1, keepdims=True))
    alpha = jnp.exp(m_prev - m_new)
    l_new = alpha * l_scr[...]
    acc = alpha * acc_scr[...]
    for p in range(npg):
        pr = jnp.exp(s_all[p] - m_new)
        l_new = l_new + jnp.sum(pr, axis=1, keepdims=True)
        acc = acc + _dot(pr.astype(BF16), v_refs[p][0, 0].astype(BF16))
    m_scr[...] = m_new
    l_scr[...] = l_new
    acc_scr[...] = acc


def _decode_finish(q_ref, kn_ref, vn_ref, g_ref, o_ref, m_scr, l_scr, acc_scr):
    q = q_ref[0].astype(F32)
    kn = kn_ref[0].astype(BF16).astype(F32)
    vn = vn_ref[0].astype(BF16).astype(F32)
    m_past = m_scr[...]
    s_new = jnp.sum(q * kn, axis=1, keepdims=True) * ATT_SCALE
    m_fin = jnp.maximum(m_past, s_new)
    a = jnp.exp(m_past - m_fin)
    pn = jnp.exp(s_new - m_fin)
    o = (a * acc_scr[...] + pn * vn) / (a * l_scr[...] + pn)
    ms = jnp.mean(o * o, axis=-1, keepdims=True)
    o_ref[0] = (o * lax.rsqrt(ms + RMS_EPS) * g_ref[...]).astype(o_ref.dtype)


def _attention_kernel(pt_ref, q_ref, k_ref, v_ref, c_ref, gp_ref, qs_ref, kn_ref, vn_ref, lfn_ref,
                      gs_ref, *rest):
    del pt_ref
    npg = PAGES_PER_STEP
    k_refs, v_refs, lf_refs = rest[:npg], rest[npg:2 * npg], rest[2 * npg:3 * npg]
    (o_ref, os_ref, m_scr, l_scr, acc_scr,
     dm_scr, dl_scr, dacc_scr, carry_scr, lf_scr) = rest[3 * npg:]
    qi, ki = pl.program_id(1), pl.program_id(2)

    def decode():
        _decode_chunk(qs_ref, k_refs, v_refs, lf_refs, dm_scr, dl_scr, dacc_scr, carry_scr, lf_scr)

    @pl.when(ki == 0)
    def _():
        m_scr[...] = jnp.full(m_scr.shape, NEG_BIG, F32)
        l_scr[...] = jnp.zeros_like(l_scr)
        acc_scr[...] = jnp.zeros_like(acc_scr)
        _decode_init(lfn_ref, dm_scr, dl_scr, dacc_scr, carry_scr)

    @pl.when(ki < qi)
    def _():
        _prompt_step(q_ref, k_ref, v_ref, c_ref, m_scr, l_scr, acc_scr, False)
        decode()

    @pl.when(ki == qi)
    def _():
        _prompt_step(q_ref, k_ref, v_ref, c_ref, m_scr, l_scr, acc_scr, True)
        _prompt_finish(gp_ref, o_ref, l_scr, acc_scr)
        decode()

    @pl.when(ki > qi)
    def _():
        decode()

    @pl.when(ki == pl.num_programs(2) - 1)
    def _():
        _decode_finish(qs_ref, kn_ref, vn_ref, gs_ref, os_ref, dm_scr, dl_scr, dacc_scr)


def _attention(layer, q, k, v, c, g, page_table, qs, kn, vn, lfn, cache_k, cache_v, cache_logf,
               blk):
    n, s, _ = q.shape
    tq = tk = blk
    nq = nk = s // blk
    n_dec, n_pages = page_table.shape
    depth, n_pool, page, heads, hd = cache_k.shape
    width = page * heads
    assert n_dec == n * nq and n_pages == nk * PAGES_PER_STEP and blk % Q_STRIP == 0
    ck = cache_k.reshape(depth, n_pool, width, hd)
    cv = cache_v.reshape(depth, n_pool, width, hd)
    cl = cache_logf.reshape(depth, n_pool, 1, width)

    def kv_block(qi, ki):
        return jnp.minimum(ki, (tq // tk) * (qi + 1) - 1)

    def page_map(p):
        return lambda b, qi, ki, pt: (layer, pt[b * nq + qi, (nk - 1 - ki) * PAGES_PER_STEP + p], 0, 0)

    kv_map = lambda b, qi, ki, pt: (layer, b, kv_block(qi, ki), 0)
    q_map = lambda b, qi, ki, pt: (b, qi, 0)
    seq_map = lambda b, qi, ki, pt: (b * nq + qi, 0, 0)
    fixed = lambda b, qi, ki, pt: (0, 0)
    in_specs = [pl.BlockSpec((1, tq, ATT_DIM), q_map),
                pl.BlockSpec((1, 1, tk, ATT_DIM), kv_map), pl.BlockSpec((1, 1, tk, ATT_DIM), kv_map),
                pl.BlockSpec((1, tk, LANES), lambda b, qi, ki, pt: (b, kv_block(qi, ki), 0)),
                pl.BlockSpec((1, ATT_DIM), fixed),
                pl.BlockSpec((1, heads, hd), seq_map), pl.BlockSpec((1, heads, hd), seq_map),
                pl.BlockSpec((1, heads, hd), seq_map), pl.BlockSpec((1, 1, width), seq_map),
                pl.BlockSpec((heads, hd), fixed)]
    in_specs += [pl.BlockSpec((1, 1, width, hd), page_map(p)) for p in range(PAGES_PER_STEP)]
    in_specs += [pl.BlockSpec((1, 1, width, hd), page_map(p)) for p in range(PAGES_PER_STEP)]
    in_specs += [pl.BlockSpec((1, 1, 1, width), page_map(p)) for p in range(PAGES_PER_STEP)]
    grid_spec = pltpu.PrefetchScalarGridSpec(
        num_scalar_prefetch=1, grid=(n, nq, nk), in_specs=in_specs,
        out_specs=[pl.BlockSpec((1, tq, ATT_DIM), q_map), pl.BlockSpec((1, heads, hd), seq_map)],
        scratch_shapes=[pltpu.VMEM((ATT_HEADS, 1, tq), F32), pltpu.VMEM((ATT_HEADS, 1, tq), F32),
                        pltpu.VMEM((ATT_HEADS, HEAD_DIM, tq), F32),
                        pltpu.VMEM((heads, 1), F32), pltpu.VMEM((heads, 1), F32),
                        pltpu.VMEM((heads, hd), F32), pltpu.VMEM((1, width), F32),
                        pltpu.VMEM((PAGES_PER_STEP, width), F32)])
    return pl.pallas_call(
        _attention_kernel, grid_spec=grid_spec,
        out_shape=[jax.ShapeDtypeStruct((n, s, ATT_DIM), BF16),
                   jax.ShapeDtypeStruct((n_dec, heads, hd), BF16)],
        compiler_params=_cp(3), name="attention")(
            page_table, q, k, v, c, g.reshape(1, ATT_DIM), qs, kn, vn, lfn, g.reshape(heads, hd),
            *([ck] * PAGES_PER_STEP), *([cv] * PAGES_PER_STEP), *([cl] * PAGES_PER_STEP))


def _pad_lanes(w):
    return jnp.pad(w, ((0, 0), (0, LANES - w.shape[1])))


def kernel(x_prompt, x_sample, cache_k, cache_v, cache_logf, state_conv, page_table, norm_mix_g,
           w_in, b_forget, conv_w, g_att_out, g_conv_out, w_out, norm_ffn_g, dense_w_gate,
           dense_w_up, dense_w_down, moe_router, moe_w_gate, moe_w_up, moe_w_down, final_norm_g):
    n, seq, d = x_prompt.shape
    n_dec, dec_seq, _ = x_sample.shape
    assert dec_seq == 1 and d == D_MODEL
    depth = w_in.shape[0]
    mp = n * seq
    tm_p, tm_s = 512, n_dec

    xp = x_prompt.reshape(mp, d)
    xs = x_sample.reshape(n_dec, d)
    xnp = _rmsnorm(xp, norm_mix_g[0], tm_p)
    xns = _rmsnorm(xs, norm_mix_g[0], tm_s)

    outs = {key: [] for key in ("fp", "cp", "ks", "vs", "fs", "cs")}
    yp = ys = k_all = v_all = None
    for l in range(depth):
        wl = w_in[l]
        o = 0
        wq = wl[:, o:o + ATT_DIM].astype(BF16); o += ATT_DIM
        wk = wl[:, o:o + ATT_DIM].astype(BF16); o += ATT_DIM
        wv = wl[:, o:o + ATT_DIM].astype(BF16); o += ATT_DIM
        wf = _pad_lanes(wl[:, o:o + ATT_HEADS]).astype(BF16); o += ATT_HEADS
        wb = wl[:, o:o + CONV_DIM].astype(BF16); o += CONV_DIM
        wc = wl[:, o:o + CONV_DIM].astype(BF16); o += CONV_DIM
        wh = wl[:, o:o + CONV_DIM].astype(BF16)
        bf = _pad_lanes(b_forget[l].reshape(1, ATT_HEADS))
        wo_a = w_out[l][:ATT_DIM].astype(BF16)
        wo_c = w_out[l][ATT_DIM:].astype(BF16)

        q_p = _mm(xnp, wq, BF16, 1024, 1024, "q_proj")
        k_all = _mm_layer(xnp, wk, k_all, l, depth, 1024, 1024, "k_proj")
        v_all = _mm_layer(xnp, wv, v_all, l, depth, 1024, 1024, "v_proj")
        lf_p, c_p = _forget(xnp, wf, bf, seq, True)
        conv_p, tails = _conv_branch(xnp, wb, wc, wh, conv_w[l], g_conv_out[l], seq, 512, 1024)
        outs["fp"].append(lf_p[:, :ATT_HEADS].reshape(n, seq, ATT_HEADS))
        tiles_per_seq = seq // 512
        outs["cp"].append(tails[tiles_per_seq - 1::tiles_per_seq])

        q_s = _mm(xns, wq, BF16, tm_s, 512, "q_proj_s")
        k_s = _mm(xns, wk, F32, tm_s, 512, "k_proj_s")
        v_s = _mm(xns, wv, F32, tm_s, 512, "v_proj_s")
        lf_s = _forget(xns, wf, bf, n_dec, False)[0]
        st = state_conv[l]
        conv_s, u_s = _conv_step(xns, wb, wc, wh, conv_w[l], g_conv_out[l], st[:, 0], st[:, 1], 512)
        lfn = jnp.tile(lf_s[:, :ATT_HEADS], (1, cache_k.shape[2])).reshape(n_dec, 1, -1)

        att_p, att_s = _attention(
            l, q_p.reshape(n, seq, ATT_DIM), k_all.reshape(depth, n, seq, ATT_DIM),
            v_all.reshape(depth, n, seq, ATT_DIM), c_p.reshape(n, seq, LANES), g_att_out[l],
            page_table, q_s.reshape(n_dec, ATT_HEADS, HEAD_DIM),
            k_s.reshape(n_dec, ATT_HEADS, HEAD_DIM), v_s.reshape(n_dec, ATT_HEADS, HEAD_DIM), lfn,
            cache_k, cache_v, cache_logf, 512)
        xp, hnp = _outproj(att_p.reshape(mp, ATT_DIM), conv_p, wo_a, wo_c, xp, norm_ffn_g[l], tm_p)
        xs, hns = _outproj(att_s.reshape(n_dec, ATT_DIM), conv_s, wo_a, wo_c, xs, norm_ffn_g[l], tm_s)
        outs["ks"].append(k_s.reshape(n_dec, 1, ATT_HEADS, HEAD_DIM))
        outs["vs"].append(v_s.reshape(n_dec, 1, ATT_HEADS, HEAD_DIM))
        outs["fs"].append(lf_s[:, :ATT_HEADS].reshape(n_dec, 1, ATT_HEADS))
        outs["cs"].append(jnp.stack([st[:, 1], u_s], axis=1))

        last = l == depth - 1
        next_g = final_norm_g if last else norm_mix_g[l + 1]
        next_dtype = F32 if last else BF16
        i = l // 2
        if l % 2 == 0:
            wg = dense_w_gate[i].astype(BF16)
            wu = dense_w_up[i].astype(BF16)
            wd = dense_w_down[i].astype(BF16)
            hp = _swiglu(hnp, wg, wu, 1024, 512)
            xp, nxp = _down(hp, wd, xp, next_g, 256, not last, next_dtype)
            hs = _swiglu(hns, wg, wu, tm_s, 512)
            xs, nxs = _down(hs, wd, xs, next_g, tm_s, not last, next_dtype)
        else:
            xp, nxp, xs, nxs = _moe_ffn(
                xp, hnp, xs, hns, norm_ffn_g[l], moe_router[i], moe_w_gate[i].astype(BF16),
                moe_w_up[i].astype(BF16), moe_w_down[i].astype(BF16), next_g, not last, next_dtype)
        if last:
            yp, ys = nxp, nxs
        else:
            xnp, xns = nxp, nxs

    return (yp.reshape(n, seq, d), ys.reshape(n_dec, 1, d),
            k_all.reshape(depth, n, seq, ATT_HEADS, HEAD_DIM),
            v_all.reshape(depth, n, seq, ATT_HEADS, HEAD_DIM),
            jnp.stack(outs["fp"]), jnp.stack(outs["cp"]),
            jnp.stack(outs["ks"]), jnp.stack(outs["vs"]), jnp.stack(outs["fs"]), jnp.stack(outs["cs"]))
```

```python
import functools

import jax
import jax.numpy as jnp
from jax import lax
from jax.experimental import pallas as pl
from jax.experimental.pallas import tpu as pltpu

D_MODEL = 2048
HEAD_DIM = 128
ATT_HEADS = 8
ATT_DIM = ATT_HEADS * HEAD_DIM
CONV_DIM = D_MODEL - ATT_DIM
CONV_W = 3
N_EXPERTS = 8
TOP_K = 2
ATT_SCALE = HEAD_DIM ** -0.5
RMS_EPS = 1e-6
LOG2_E = 1.4426950408889634
LANES = 128
NEG_BIG = -1e30
PAGES_PER_STEP = 16
Q_STRIP = 128
VMEM_LIMIT = 56 * 1024 * 1024

F32 = jnp.float32
BF16 = jnp.bfloat16


def _cp(n_axes):
    return pltpu.CompilerParams(dimension_semantics=("arbitrary",) * n_axes,
                                vmem_limit_bytes=VMEM_LIMIT)


def _rms(x, g):
    return x * lax.rsqrt(jnp.mean(x * x, axis=-1, keepdims=True) + RMS_EPS) * g


def _dot(a, b):
    return jnp.dot(a, b, preferred_element_type=F32)


def _rmsnorm_kernel(x_ref, g_ref, o_ref):
    o_ref[...] = _rms(x_ref[...], g_ref[...]).astype(o_ref.dtype)


def _rmsnorm(x, g, tm, out_dtype=BF16):
    m, d = x.shape
    return pl.pallas_call(
        _rmsnorm_kernel, grid=(m // tm,),
        in_specs=[pl.BlockSpec((tm, d), lambda i: (i, 0)),
                  pl.BlockSpec((1, d), lambda i: (0, 0))],
        out_specs=pl.BlockSpec((tm, d), lambda i: (i, 0)),
        out_shape=jax.ShapeDtypeStruct((m, d), out_dtype),
        compiler_params=_cp(1), name="rmsnorm")(x, g.reshape(1, d))


def _mm_kernel(x_ref, w_ref, *rest):
    o_ref = rest[-1]
    o_ref[...] = _dot(x_ref[...], w_ref[...]).astype(o_ref.dtype).reshape(o_ref.shape)


def _mm_layer(x, w, stacked, layer, depth, tm, tn, name):
    m, k = x.shape
    n = w.shape[1]
    in_specs = [pl.BlockSpec((tm, k), lambda i, j: (i, 0)),
                pl.BlockSpec((k, tn), lambda i, j: (0, j))]
    args, aliases = [x, w], {}
    if stacked is not None:
        in_specs.append(pl.BlockSpec(memory_space=pl.ANY))
        args.append(stacked)
        aliases = {2: 0}
    return pl.pallas_call(
        _mm_kernel, grid=(m // tm, n // tn), in_specs=in_specs,
        out_specs=pl.BlockSpec((1, tm, tn), lambda i, j: (layer, i, j)),
        out_shape=jax.ShapeDtypeStruct((depth, m, n), F32), input_output_aliases=aliases,
        compiler_params=_cp(2), name=name)(*args)


def _mm(x, w, out_dtype, tm, tn, name):
    m, k = x.shape
    n = w.shape[1]
    return pl.pallas_call(
        _mm_kernel, grid=(m // tm, n // tn),
        in_specs=[pl.BlockSpec((tm, k), lambda i, j: (i, 0)),
                  pl.BlockSpec((k, tn), lambda i, j: (0, j))],
        out_specs=pl.BlockSpec((tm, tn), lambda i, j: (i, j)),
        out_shape=jax.ShapeDtypeStruct((m, n), out_dtype),
        compiler_params=_cp(2), name=name)(x, w)


def _norm_mm_kernel(x_ref, g_ref, w_ref, xn_ref, o_ref):
    xn = _rms(x_ref[...], g_ref[...]).astype(xn_ref.dtype)
    xn_ref[...] = xn
    o_ref[...] = _dot(xn, w_ref[...]).astype(o_ref.dtype)


def _norm_mm(x, g, w, tm, name):
    m, d = x.shape
    n = w.shape[1]
    row = lambda i: (i, 0)
    fixed = lambda i: (0, 0)
    return pl.pallas_call(
        _norm_mm_kernel, grid=(m // tm,),
        in_specs=[pl.BlockSpec((tm, d), row), pl.BlockSpec((1, d), fixed),
                  pl.BlockSpec((d, n), fixed)],
        out_specs=[pl.BlockSpec((tm, d), row), pl.BlockSpec((tm, n), row)],
        out_shape=[jax.ShapeDtypeStruct((m, d), BF16), jax.ShapeDtypeStruct((m, n), BF16)],
        compiler_params=_cp(1), name=name)(x, g.reshape(1, d), w)


def _swiglu_kernel(x_ref, wg_ref, wu_ref, o_ref):
    x = x_ref[...]
    a = _dot(x, wg_ref[...])
    b = _dot(x, wu_ref[...])
    o_ref[...] = (a * (1.0 / (1.0 + jnp.exp(-a))) * b).astype(o_ref.dtype)


def _swiglu(x, wg, wu, tm, tn):
    m, k = x.shape
    n = wg.shape[1]
    return pl.pallas_call(
        _swiglu_kernel, grid=(m // tm, n // tn),
        in_specs=[pl.BlockSpec((tm, k), lambda i, j: (i, 0)),
                  pl.BlockSpec((k, tn), lambda i, j: (0, j)),
                  pl.BlockSpec((k, tn), lambda i, j: (0, j))],
        out_specs=pl.BlockSpec((tm, tn), lambda i, j: (i, j)),
        out_shape=jax.ShapeDtypeStruct((m, n), BF16),
        compiler_params=_cp(2), name="swiglu_up")(x, wg, wu)


def _down_kernel(h_ref, w_ref, res_ref, g_ref, *outs, emit_x):
    xo = res_ref[...] + _dot(h_ref[...], w_ref[...])
    if emit_x:
        outs[0][...] = xo
    outs[-1][...] = _rms(xo, g_ref[...]).astype(outs[-1].dtype)


def _down(h, w, res, norm_g, tm, emit_x, norm_dtype):
    m, kdim = h.shape
    d = w.shape[1]
    row = lambda i: (i, 0)
    fixed = lambda i: (0, 0)
    out_specs = [pl.BlockSpec((tm, d), row)]
    out_shape = [jax.ShapeDtypeStruct((m, d), norm_dtype)]
    if emit_x:
        out_specs.insert(0, pl.BlockSpec((tm, d), row))
        out_shape.insert(0, jax.ShapeDtypeStruct((m, d), F32))
    outs = pl.pallas_call(
        functools.partial(_down_kernel, emit_x=emit_x), grid=(m // tm,),
        in_specs=[pl.BlockSpec((tm, kdim), row),
                  pl.BlockSpec((kdim, d), fixed, pipeline_mode=pl.Buffered(1)),
                  pl.BlockSpec((tm, d), row), pl.BlockSpec((1, d), fixed)],
        out_specs=out_specs, out_shape=out_shape,
        compiler_params=_cp(1), name="ffn_down")(h, w, res, norm_g.reshape(1, d))
    return (outs[0], outs[1]) if emit_x else (None, outs[0])


def _outproj_kernel(a_ref, c_ref, wa_ref, wc_ref, res_ref, g_ref, xo_ref, xn_ref):
    xo = res_ref[...] + (_dot(a_ref[...], wa_ref[...]) + _dot(c_ref[...], wc_ref[...]))
    xo_ref[...] = xo
    xn_ref[...] = _rms(xo, g_ref[...]).astype(xn_ref.dtype)


def _outproj(a, c, wa, wc, res, g, tm):
    m, d = res.shape
    ka, kc = a.shape[1], c.shape[1]
    row = lambda i: (i, 0)
    fixed = lambda i: (0, 0)
    return pl.pallas_call(
        _outproj_kernel, grid=(m // tm,),
        in_specs=[pl.BlockSpec((tm, ka), row), pl.BlockSpec((tm, kc), row),
                  pl.BlockSpec((ka, d), fixed), pl.BlockSpec((kc, d), fixed),
                  pl.BlockSpec((tm, d), row), pl.BlockSpec((1, d), fixed)],
        out_specs=[pl.BlockSpec((tm, d), row), pl.BlockSpec((tm, d), row)],
        out_shape=[jax.ShapeDtypeStruct((m, d), F32), jax.ShapeDtypeStruct((m, d), BF16)],
        compiler_params=_cp(1), name="out_proj")(a, c, wa, wc, res, g.reshape(1, d))


SEL_E1, SEL_E2, SEL_G1, SEL_G2, SEL_R1, SEL_R2 = range(6)


def _router_kernel(x_ref, r_ref, cin_ref, sel_ref, cnt_ref, carry_scr, *, n_valid):
    i = pl.program_id(0)
    tm = x_ref.shape[0]

    @pl.when(i == 0)
    def _():
        carry_scr[...] = cin_ref[...]

    logits = _dot(x_ref[...], r_ref[...])
    lane = lax.broadcasted_iota(jnp.int32, logits.shape, 1).astype(F32)
    lg = jnp.where(lane < N_EXPERTS, logits, -jnp.inf)
    m1 = jnp.max(lg, axis=1, keepdims=True)
    i1 = jnp.min(jnp.where(lg == m1, lane, float(LANES)), axis=1, keepdims=True)
    lg2 = jnp.where(lane == i1, -jnp.inf, lg)
    m2 = jnp.max(lg2, axis=1, keepdims=True)
    i2 = jnp.min(jnp.where(lg2 == m2, lane, float(LANES)), axis=1, keepdims=True)
    e2 = jnp.exp(m2 - m1)
    den = 1.0 + e2
    row = lax.broadcasted_iota(jnp.int32, logits.shape, 0) + i * tm
    hit = jnp.where(((lane == i1) | (lane == i2)) & (row < n_valid), 1.0, 0.0)
    r = lax.broadcasted_iota(jnp.int32, (tm, tm), 0)
    c = lax.broadcasted_iota(jnp.int32, (tm, tm), 1)
    before = (c < r).astype(BF16)
    rank = _dot(before, hit.astype(BF16)) + carry_scr[...]
    carry_scr[...] = carry_scr[...] + jnp.sum(hit, axis=0, keepdims=True)
    cnt_ref[...] = carry_scr[...]
    rank1 = jnp.sum(jnp.where(lane == i1, rank, 0.0), axis=1, keepdims=True)
    rank2 = jnp.sum(jnp.where(lane == i2, rank, 0.0), axis=1, keepdims=True)
    sel = jnp.zeros_like(logits)
    for idx, val in ((SEL_E1, i1), (SEL_E2, i2), (SEL_G1, 1.0 / den), (SEL_G2, e2 / den),
                     (SEL_R1, rank1), (SEL_R2, rank2)):
        sel = jnp.where(lane == idx, val, sel)
    sel_ref[...] = sel


def _router(hn, r_pad, count_in, tm, n_valid):
    m, d = hn.shape
    fixed = lambda i: (0, 0)
    return pl.pallas_call(
        functools.partial(_router_kernel, n_valid=n_valid), grid=(m // tm,),
        in_specs=[pl.BlockSpec((tm, d), lambda i: (i, 0)), pl.BlockSpec((d, LANES), fixed),
                  pl.BlockSpec((1, LANES), fixed)],
        out_specs=[pl.BlockSpec((tm, LANES), lambda i: (i, 0)), pl.BlockSpec((1, LANES), fixed)],
        out_shape=[jax.ShapeDtypeStruct((m, LANES), F32), jax.ShapeDtypeStruct((1, LANES), F32)],
        scratch_shapes=[pltpu.VMEM((1, LANES), F32)],
        compiler_params=_cp(1), name="router")(hn, r_pad, count_in)


def _dispatch_kernel(d1_ref, d2_ref, ps_ref, pc_ref, x_ref, *rest, first):
    xs_ref, zero_scr, sem = rest[-3:]
    tt = x_ref.shape[0]
    base = pl.program_id(0) * tt

    def row_copy(src_row, dst_row):
        return pltpu.make_async_copy(src_row, xs_ref.at[pl.ds(dst_row, 1), :], sem)

    def issue(r, carry):
        src = x_ref.at[pl.ds(r, 1), :]
        row_copy(src, d1_ref[base + r]).start()
        row_copy(src, d2_ref[base + r]).start(priority=1)
        return carry

    def drain(r, carry):
        row_copy(x_ref.at[pl.ds(0, 1), :], 0).wait()
        row_copy(x_ref.at[pl.ds(0, 1), :], 0).wait()
        return carry

    lax.fori_loop(0, tt, issue, 0, unroll=8)
    lax.fori_loop(0, tt, drain, 0, unroll=8)

    if first:
        @pl.when(pl.program_id(0) == 0)
        def _():
            zero_scr[...] = jnp.zeros_like(zero_scr)
            zrow = zero_scr.at[pl.ds(0, 1), :]
            for e in range(N_EXPERTS):
                def fill(r, carry, e=e):
                    row_copy(zrow, ps_ref[e] + r).start()
                    return carry

                def fill_done(r, carry):
                    row_copy(zrow, 0).wait()
                    return carry

                lax.fori_loop(0, pc_ref[e], fill, 0)
                lax.fori_loop(0, pc_ref[e], fill_done, 0)


def _dispatch(d1, d2, pad_start, pad_count, x, xs, tt, n_rows):
    m, d = x.shape
    first = xs is None
    in_specs = [pl.BlockSpec((tt, d), lambda i, *_: (i, 0))]
    args = [d1, d2, pad_start, pad_count, x]
    aliases = {}
    if not first:
        in_specs.append(pl.BlockSpec(memory_space=pl.ANY))
        args.append(xs)
        aliases = {5: 0}
    grid_spec = pltpu.PrefetchScalarGridSpec(
        num_scalar_prefetch=4, grid=(m // tt,), in_specs=in_specs,
        out_specs=pl.BlockSpec(memory_space=pl.ANY),
        scratch_shapes=[pltpu.VMEM((8, d), F32), pltpu.SemaphoreType.DMA(())])
    return pl.pallas_call(
        functools.partial(_dispatch_kernel, first=first), grid_spec=grid_spec,
        out_shape=jax.ShapeDtypeStruct((n_rows, d), F32), input_output_aliases=aliases,
        compiler_params=_cp(1), name="moe_dispatch")(*args)


def _moe_up_kernel(te_ref, na_ref, x_ref, g_ref, wg_ref, wu_ref, o_ref):
    del te_ref

    @pl.when(pl.program_id(1) < na_ref[0])
    def _():
        xn = _rms(x_ref[...], g_ref[...]).astype(BF16)
        a = _dot(xn, wg_ref[0])
        b = _dot(xn, wu_ref[0])
        o_ref[...] = (a * (1.0 / (1.0 + jnp.exp(-a))) * b).astype(o_ref.dtype)


def _moe_up(tile_expert, n_active, xs, g, wg, wu, tm, tn):
    r, d = xs.shape
    f = wg.shape[2]
    tile = lambda t, na: jnp.minimum(t, na[0] - 1)
    w_map = lambda j, t, te, na: (te[tile(t, na)], 0, j)
    grid_spec = pltpu.PrefetchScalarGridSpec(
        num_scalar_prefetch=2, grid=(f // tn, r // tm),
        in_specs=[pl.BlockSpec((tm, d), lambda j, t, te, na: (tile(t, na), 0)),
                  pl.BlockSpec((1, d), lambda j, t, te, na: (0, 0)),
                  pl.BlockSpec((1, d, tn), w_map), pl.BlockSpec((1, d, tn), w_map)],
        out_specs=pl.BlockSpec((tm, tn), lambda j, t, te, na: (tile(t, na), j)))
    return pl.pallas_call(
        _moe_up_kernel, grid_spec=grid_spec, out_shape=jax.ShapeDtypeStruct((r, f), BF16),
        compiler_params=_cp(2), name="moe_up")(tile_expert, n_active, xs, g.reshape(1, d), wg, wu)


def _moe_down_kernel(te_ref, na_ref, h_ref, w_ref, o_ref):
    del te_ref

    @pl.when(pl.program_id(0) < na_ref[0])
    def _():
        o_ref[...] = _dot(h_ref[...], w_ref[0])


def _moe_down(tile_expert, n_active, h, wd, tm):
    r, f = h.shape
    d = wd.shape[2]
    tile = lambda t, na: jnp.minimum(t, na[0] - 1)
    grid_spec = pltpu.PrefetchScalarGridSpec(
        num_scalar_prefetch=2, grid=(r // tm,),
        in_specs=[pl.BlockSpec((tm, f), lambda t, te, na: (tile(t, na), 0)),
                  pl.BlockSpec((1, f, d), lambda t, te, na: (te[tile(t, na)], 0, 0))],
        out_specs=pl.BlockSpec((tm, d), lambda t, te, na: (tile(t, na), 0)))
    return pl.pallas_call(
        _moe_down_kernel, grid_spec=grid_spec, out_shape=jax.ShapeDtypeStruct((r, d), F32),
        compiler_params=_cp(1), name="moe_down")(tile_expert, n_active, h, wd)


def _combine_kernel(d1_ref, d2_ref, ys_ref, res_ref, sel_ref, g_ref, *rest, emit_x):
    y1_scr, y2_scr, sem = rest[-3:]
    outs = rest[:-3]
    tt = res_ref.shape[0]
    base = pl.program_id(0) * tt

    def row_copy(src_row, dst):
        return pltpu.make_async_copy(ys_ref.at[pl.ds(src_row, 1), :], dst, sem)

    def issue(r, carry):
        row_copy(d1_ref[base + r], y1_scr.at[pl.ds(r, 1), :]).start()
        row_copy(d2_ref[base + r], y2_scr.at[pl.ds(r, 1), :]).start(priority=1)
        return carry

    def drain(r, carry):
        row_copy(0, y1_scr.at[pl.ds(0, 1), :]).wait()
        row_copy(0, y2_scr.at[pl.ds(0, 1), :]).wait()
        return carry

    lax.fori_loop(0, tt, issue, 0, unroll=8)
    lax.fori_loop(0, tt, drain, 0, unroll=8)
    sel = sel_ref[...]
    g1 = sel[:, SEL_G1:SEL_G1 + 1]
    g2 = sel[:, SEL_G2:SEL_G2 + 1]
    xo = res_ref[...] + (g1 * y1_scr[...] + g2 * y2_scr[...])
    if emit_x:
        outs[0][...] = xo
    outs[-1][...] = _rms(xo, g_ref[...]).astype(outs[-1].dtype)


def _combine(d1, d2, ys, res, sel, g, tt, emit_x, norm_dtype):
    m, d = res.shape
    row = lambda i, *_: (i, 0)
    out_specs = [pl.BlockSpec((tt, d), row)]
    out_shape = [jax.ShapeDtypeStruct((m, d), norm_dtype)]
    if emit_x:
        out_specs.insert(0, pl.BlockSpec((tt, d), row))
        out_shape.insert(0, jax.ShapeDtypeStruct((m, d), F32))
    grid_spec = pltpu.PrefetchScalarGridSpec(
        num_scalar_prefetch=2, grid=(m // tt,),
        in_specs=[pl.BlockSpec(memory_space=pl.ANY), pl.BlockSpec((tt, d), row),
                  pl.BlockSpec((tt, LANES), row), pl.BlockSpec((1, d), lambda i, *_: (0, 0))],
        out_specs=out_specs,
        scratch_shapes=[pltpu.VMEM((tt, d), F32), pltpu.VMEM((tt, d), F32),
                        pltpu.SemaphoreType.DMA(())])
    outs = pl.pallas_call(
        functools.partial(_combine_kernel, emit_x=emit_x), grid_spec=grid_spec,
        out_shape=out_shape, compiler_params=_cp(1), name="moe_combine")(
            d1, d2, ys, res, sel, g.reshape(1, d))
    return (outs[0], outs[1]) if emit_x else (None, outs[0])


def _moe_ffn(x_p, hn_p, x_s, hn_s, ffn_g, router, wg, wu, wd, next_g, emit_x, norm_dtype,
             tm=512, tn=1408, t_route=512, t_move=512, t_comb=256):
    mp, d = x_p.shape
    ms = x_s.shape[0]
    n_tiles = (TOP_K * (mp + ms)) // tm + N_EXPERTS
    r_pad = _pad_lanes(router).astype(BF16)
    ms_pad = -(-ms // LANES) * LANES
    hn_s_pad = jnp.pad(hn_s, ((0, ms_pad - ms), (0, 0)))
    sel_p, cnt_p = _router(hn_p, r_pad, jnp.zeros((1, LANES), F32), t_route, mp)
    sel_s, cnt = _router(hn_s_pad, r_pad, cnt_p, ms_pad, ms)

    count = cnt[0, :N_EXPERTS].astype(jnp.int32)
    padded = (count + tm - 1) // tm * tm
    g_end = jnp.cumsum(padded)
    g_start = g_end - padded
    tile_expert = jnp.minimum(
        jnp.sum(jnp.arange(n_tiles)[:, None] >= (g_end // tm)[None, :], axis=1),
        N_EXPERTS - 1).astype(jnp.int32)
    n_active = (g_end[-1:] // tm).astype(jnp.int32)

    def dests(sel, rows):
        e1, e2 = sel[:rows, SEL_E1].astype(jnp.int32), sel[:rows, SEL_E2].astype(jnp.int32)
        return (g_start[e1] + sel[:rows, SEL_R1].astype(jnp.int32),
                g_start[e2] + sel[:rows, SEL_R2].astype(jnp.int32))

    d1_p, d2_p = dests(sel_p, mp)
    d1_s, d2_s = dests(sel_s, ms)
    pad_start = g_start + count
    pad_count = padded - count
    xs = _dispatch(d1_p, d2_p, pad_start, pad_count, x_p, None, t_move, n_tiles * tm)
    xs = _dispatch(d1_s, d2_s, pad_start, pad_count, x_s, xs, ms, n_tiles * tm)
    h = _moe_up(tile_expert, n_active, xs, ffn_g, wg, wu, tm, tn)
    ys = _moe_down(tile_expert, n_active, h, wd, tm)
    xo_p, xn_p = _combine(d1_p, d2_p, ys, x_p, sel_p, next_g, t_comb, emit_x, norm_dtype)
    xo_s, xn_s = _combine(d1_s, d2_s, ys, x_s, sel_s, next_g, ms, emit_x, norm_dtype)
    return xo_p, xn_p, xo_s, xn_s


def _forget_kernel(x_ref, w_ref, b_ref, lf_ref, *c_ref, cumsum):
    z = _dot(x_ref[...], w_ref[...]) + b_ref[...]
    lf = jnp.minimum(z, 0.0) - jnp.log1p(jnp.exp(-jnp.abs(z)))
    lf_ref[...] = lf
    if cumsum:
        r = lax.broadcasted_iota(jnp.int32, (LANES, LANES), 0)
        c = lax.broadcasted_iota(jnp.int32, (LANES, LANES), 1)
        tri = (c <= r).astype(F32)
        carry = jnp.zeros((1, LANES), F32)
        for blk in range(lf.shape[0] // LANES):
            rows = slice(blk * LANES, (blk + 1) * LANES)
            cb = jnp.dot(tri, lf[rows, :], preferred_element_type=F32,
                         precision=lax.Precision.HIGHEST) + carry
            carry = cb[LANES - 1:LANES, :]
            c_ref[0][rows, :] = cb


def _forget(xn, wf_pad, bf_pad, rows, cumsum):
    m, d = xn.shape
    n = m // rows
    out_specs = [pl.BlockSpec((rows, LANES), lambda b: (b, 0))]
    out_shape = [jax.ShapeDtypeStruct((m, LANES), F32)]
    if cumsum:
        out_specs.append(pl.BlockSpec((rows, LANES), lambda b: (b, 0)))
        out_shape.append(jax.ShapeDtypeStruct((m, LANES), F32))
    return pl.pallas_call(
        functools.partial(_forget_kernel, cumsum=cumsum), grid=(n,),
        in_specs=[pl.BlockSpec((rows, d), lambda b: (b, 0)),
                  pl.BlockSpec((d, LANES), lambda b: (0, 0)),
                  pl.BlockSpec((1, LANES), lambda b: (0, 0))],
        out_specs=out_specs, out_shape=out_shape,
        compiler_params=_cp(1), name="forget_gate")(xn, wf_pad, bf_pad)


def _group_norm_store(yc, g_ref, o_ref):
    for gi in range(yc.shape[1] // HEAD_DIM):
        sl = slice(gi * HEAD_DIM, (gi + 1) * HEAD_DIM)
        yg = yc[:, sl]
        ms = jnp.mean(yg * yg, axis=-1, keepdims=True)
        o_ref[:, sl] = (yg * lax.rsqrt(ms + RMS_EPS) * g_ref[:, sl]).astype(o_ref.dtype)


def _conv_kernel(x_ref, wb_ref, wc_ref, wh_ref, cw_ref, g_ref, o_ref, tail_ref, halo_scr,
                 *, tiles_per_seq):
    i, j = pl.program_id(0), pl.program_id(1)
    x = x_ref[...]
    gate_b = _dot(x, wb_ref[...])
    u = _dot(x, wc_ref[...]) * _dot(x, wh_ref[...])
    tm = u.shape[0]

    @pl.when(i % tiles_per_seq == 0)
    def _():
        halo_scr[j] = jnp.zeros(halo_scr.shape[1:], F32)

    prev = halo_scr[j]
    row = lax.broadcasted_iota(jnp.int32, u.shape, 0)
    u1 = jnp.where(row == 0, prev[1:2, :], pltpu.roll(u, 1, axis=0))
    u2 = jnp.where(row == 0, prev[0:1, :],
                   jnp.where(row == 1, prev[1:2, :], pltpu.roll(u, 2, axis=0)))
    cw = cw_ref[...]
    y = u2 * cw[0:1, :] + u1 * cw[1:2, :] + u * cw[2:3, :]
    _group_norm_store(gate_b * y, g_ref, o_ref)
    tail = u[tm - (CONV_W - 1):tm, :]
    halo_scr[j] = tail
    tail_ref[0] = tail


def _conv_branch(xn, wb, wc, wh, cw, g, seq, tm, tc):
    m, d = xn.shape
    n_i, n_j = m // tm, CONV_DIM // tc
    col = lambda i, j: (0, j)
    return pl.pallas_call(
        functools.partial(_conv_kernel, tiles_per_seq=seq // tm), grid=(n_i, n_j),
        in_specs=[pl.BlockSpec((tm, d), lambda i, j: (i, 0)),
                  pl.BlockSpec((d, tc), col), pl.BlockSpec((d, tc), col),
                  pl.BlockSpec((d, tc), col), pl.BlockSpec((CONV_W, tc), col),
                  pl.BlockSpec((1, tc), col)],
        out_specs=[pl.BlockSpec((tm, tc), lambda i, j: (i, j)),
                   pl.BlockSpec((1, CONV_W - 1, tc), lambda i, j: (i, 0, j))],
        out_shape=[jax.ShapeDtypeStruct((m, CONV_DIM), BF16),
                   jax.ShapeDtypeStruct((n_i, CONV_W - 1, CONV_DIM), F32)],
        scratch_shapes=[pltpu.VMEM((n_j, CONV_W - 1, tc), F32)],
        compiler_params=_cp(2), name="conv_branch")(xn, wb, wc, wh, cw, g.reshape(1, CONV_DIM))


def _conv_step_kernel(x_ref, wb_ref, wc_ref, wh_ref, cw_ref, g_ref, p0_ref, p1_ref, o_ref, u_ref):
    x = x_ref[...]
    gate_b = _dot(x, wb_ref[...])
    u = _dot(x, wc_ref[...]) * _dot(x, wh_ref[...])
    cw = cw_ref[...]
    y = p0_ref[...] * cw[0:1, :] + p1_ref[...] * cw[1:2, :] + u * cw[2:3, :]
    _group_norm_store(gate_b * y, g_ref, o_ref)
    u_ref[...] = u


def _conv_step(xn, wb, wc, wh, cw, g, p0, p1, tc):
    m, d = xn.shape
    col = lambda j: (0, j)
    return pl.pallas_call(
        _conv_step_kernel, grid=(CONV_DIM // tc,),
        in_specs=[pl.BlockSpec((m, d), lambda j: (0, 0)),
                  pl.BlockSpec((d, tc), col), pl.BlockSpec((d, tc), col),
                  pl.BlockSpec((d, tc), col), pl.BlockSpec((CONV_W, tc), col),
                  pl.BlockSpec((1, tc), col), pl.BlockSpec((m, tc), col),
                  pl.BlockSpec((m, tc), col)],
        out_specs=[pl.BlockSpec((m, tc), col), pl.BlockSpec((m, tc), col)],
        out_shape=[jax.ShapeDtypeStruct((m, CONV_DIM), BF16),
                   jax.ShapeDtypeStruct((m, CONV_DIM), F32)],
        compiler_params=_cp(1), name="conv_step")(xn, wb, wc, wh, cw, g.reshape(1, CONV_DIM), p0, p1)


def _prompt_step(q_ref, k_ref, v_ref, c_ref, m_scr, l_scr, acc_scr, diagonal):
    tq, tk = q_ref.shape[1], k_ref.shape[2]
    if diagonal:
        causal = (lax.broadcasted_iota(jnp.int32, (Q_STRIP, Q_STRIP), 0)
                  <= lax.broadcasted_iota(jnp.int32, (Q_STRIP, Q_STRIP), 1))
    for h in range(ATT_HEADS):
        sl = slice(h * HEAD_DIM, (h + 1) * HEAD_DIM)
        k = k_ref[0, 0, :, sl].astype(BF16)
        vt = v_ref[0, 0, :, sl].T.astype(BF16)
        c_keys = jnp.broadcast_to(c_ref[0, :, h:h + 1] * LOG2_E, (tk, Q_STRIP))
        for s0 in range(0, tq, Q_STRIP):
            qs = slice(s0, s0 + Q_STRIP)
            nkeys = min(tk, s0 + Q_STRIP) if diagonal else tk
            st = lax.dot_general(k[:nkeys], q_ref[0, qs, sl], (((1,), (1,)), ((), ())),
                                 preferred_element_type=F32)
            st = st * (ATT_SCALE * LOG2_E) - c_keys[:nkeys]
            if diagonal:
                tail = jnp.where(causal, st[s0:], NEG_BIG)
                st = tail if s0 == 0 else jnp.concatenate([st[:s0], tail], axis=0)
            m_prev = m_scr[h, :, qs]
            m_new = jnp.maximum(m_prev, jnp.max(st, axis=0, keepdims=True))
            alpha = jnp.exp2(m_prev - m_new)
            p = jnp.exp2(st - m_new)
            l_scr[h, :, qs] = alpha * l_scr[h, :, qs] + jnp.sum(p, axis=0, keepdims=True)
            acc_scr[h, :, qs] = alpha * acc_scr[h, :, qs] + _dot(vt[:, :nkeys], p.astype(BF16))
            m_scr[h, :, qs] = m_new


def _prompt_finish(g_ref, o_ref, l_scr, acc_scr):
    for h in range(ATT_HEADS):
        sl = slice(h * HEAD_DIM, (h + 1) * HEAD_DIM)
        ot = acc_scr[h] / l_scr[h]
        ms = jnp.mean(ot * ot, axis=0, keepdims=True)
        o = (ot * lax.rsqrt(ms + RMS_EPS)).T
        o_ref[0, :, sl] = (o * g_ref[:, sl]).astype(o_ref.dtype)


def _decode_init(lfn_ref, m_scr, l_scr, acc_scr, carry_scr):
    m_scr[...] = jnp.full(m_scr.shape, NEG_BIG, F32)
    l_scr[...] = jnp.zeros_like(l_scr)
    acc_scr[...] = jnp.zeros_like(acc_scr)
    carry_scr[...] = lfn_ref[0]


def _decode_chunk(q_ref, k_refs, v_refs, lf_refs, m_scr, l_scr, acc_scr, carry_scr, lf_scr):
    npg = PAGES_PER_STEP
    width = lf_scr.shape[1]
    lane = lax.broadcasted_iota(jnp.int32, (npg, width), 1)
    for p in range(npg):
        lf_scr[p:p + 1, :] = lf_refs[p][0, 0]
    lf = lf_scr[...]
    y = lf
    sh = ATT_HEADS
    while sh < width:
        y = y + jnp.where(lane < width - sh, pltpu.roll(y, width - sh, axis=1), 0.0)
        sh *= 2
    excl = y - lf
    z = jnp.where(lane < ATT_HEADS, y, 0.0)
    sh = ATT_HEADS
    while sh < width:
        z = z + pltpu.roll(z, sh, axis=1)
        sh *= 2
    carry = carry_scr[...]
    bias = [None] * npg
    for p in reversed(range(npg)):
        bias[p] = carry + excl[p:p + 1, :]
        carry = carry + z[p:p + 1, :]
    carry_scr[...] = carry

    q = q_ref[0]
    hrow = lax.broadcasted_iota(jnp.int32, (ATT_HEADS, width), 0)
    hlane = lax.broadcasted_iota(jnp.int32, (ATT_HEADS, width), 1)
    own = jnp.bitwise_and(hlane, ATT_HEADS - 1) == hrow
    s_all = []
    for p in range(npg):
        kp = k_refs[p][0, 0].astype(BF16)
        s = lax.dot_general(q, kp, (((1,), (1,)), ((), ())), preferred_element_type=F32)
        s_all.append(jnp.where(own, s * ATT_SCALE + bias[p], NEG_BIG))
    m_prev = m_scr[...]
    m_new = m_prev
    for s in s_all:
        m_new = jnp.maximum(m_new, jnp.max(s, axis=1, keepdims=True))
    alpha = jnp.exp(m_prev - m_new)
    l_new = alpha * l_scr[...]
    acc = alpha * acc_scr[...]
    for p in range(npg):
        pr = jnp.exp(s_all[p] - m_new)
        l_new = l_new + jnp.sum(pr, axis=1, keepdims=True)
        acc = acc + _dot(pr.astype(BF16), v_refs[p][0, 0].astype(BF16))
    m_scr[...] = m_new
    l_scr[...] = l_new
    acc_scr[...] = acc


def _decode_finish(q_ref, kn_ref, vn_ref, g_ref, o_ref, m_scr, l_scr, acc_scr):
    q = q_ref[0].astype(F32)
    kn = kn_ref[0].astype(BF16).astype(F32)
    vn = vn_ref[0].astype(BF16).astype(F32)
    m_past = m_scr[...]
    s_new = jnp.sum(q * kn, axis=1, keepdims=True) * ATT_SCALE
    m_fin = jnp.maximum(m_past, s_new)
    a = jnp.exp(m_past - m_fin)
    pn = jnp.exp(s_new - m_fin)
    o = (a * acc_scr[...] + pn * vn) / (a * l_scr[...] + pn)
    ms = jnp.mean(o * o, axis=-1, keepdims=True)
    o_ref[0] = (o * lax.rsqrt(ms + RMS_EPS) * g_ref[...]).astype(o_ref.dtype)


def _attention_kernel(pt_ref, q_ref, k_ref, v_ref, c_ref, gp_ref, qs_ref, kn_ref, vn_ref, lfn_ref,
                      gs_ref, *rest):
    del pt_ref
    npg = PAGES_PER_STEP
    k_refs, v_refs, lf_refs = rest[:npg], rest[npg:2 * npg], rest[2 * npg:3 * npg]
    (o_ref, os_ref, m_scr, l_scr, acc_scr,
     dm_scr, dl_scr, dacc_scr, carry_scr, lf_scr) = rest[3 * npg:]
    qi, ki = pl.program_id(1), pl.program_id(2)

    def decode():
        _decode_chunk(qs_ref, k_refs, v_refs, lf_refs, dm_scr, dl_scr, dacc_scr, carry_scr, lf_scr)

    @pl.when(ki == 0)
    def _():
        m_scr[...] = jnp.full(m_scr.shape, NEG_BIG, F32)
        l_scr[...] = jnp.zeros_like(l_scr)
        acc_scr[...] = jnp.zeros_like(acc_scr)
        _decode_init(lfn_ref, dm_scr, dl_scr, dacc_scr, carry_scr)

    @pl.when(ki < qi)
    def _():
        _prompt_step(q_ref, k_ref, v_ref, c_ref, m_scr, l_scr, acc_scr, False)
        decode()

    @pl.when(ki == qi)
    def _():
        _prompt_step(q_ref, k_ref, v_ref, c_ref, m_scr, l_scr, acc_scr, True)
        _prompt_finish(gp_ref, o_ref, l_scr, acc_scr)
        decode()

    @pl.when(ki > qi)
    def _():
        decode()

    @pl.when(ki == pl.num_programs(2) - 1)
    def _():
        _decode_finish(qs_ref, kn_ref, vn_ref, gs_ref, os_ref, dm_scr, dl_scr, dacc_scr)


def _attention(layer, q, k, v, c, g, page_table, qs, kn, vn, lfn, cache_k, cache_v, cache_logf,
               blk):
    n, s, _ = q.shape
    tq = tk = blk
    nq = nk = s // blk
    n_dec, n_pages = page_table.shape
    depth, n_pool, page, heads, hd = cache_k.shape
    width = page * heads
    assert n_dec == n * nq and n_pages == nk * PAGES_PER_STEP and blk % Q_STRIP == 0
    ck = cache_k.reshape(depth, n_pool, width, hd)
    cv = cache_v.reshape(depth, n_pool, width, hd)
    cl = cache_logf.reshape(depth, n_pool, 1, width)

    def kv_block(qi, ki):
        return jnp.minimum(ki, (tq // tk) * (qi + 1) - 1)

    def page_map(p):
        return lambda b, qi, ki, pt: (layer, pt[b * nq + qi, (nk - 1 - ki) * PAGES_PER_STEP + p], 0, 0)

    kv_map = lambda b, qi, ki, pt: (layer, b, kv_block(qi, ki), 0)
    q_map = lambda b, qi, ki, pt: (b, qi, 0)
    seq_map = lambda b, qi, ki, pt: (b * nq + qi, 0, 0)
    fixed = lambda b, qi, ki, pt: (0, 0)
    in_specs = [pl.BlockSpec((1, tq, ATT_DIM), q_map),
                pl.BlockSpec((1, 1, tk, ATT_DIM), kv_map), pl.BlockSpec((1, 1, tk, ATT_DIM), kv_map),
                pl.BlockSpec((1, tk, LANES), lambda b, qi, ki, pt: (b, kv_block(qi, ki), 0)),
                pl.BlockSpec((1, ATT_DIM), fixed),
                pl.BlockSpec((1, heads, hd), seq_map), pl.BlockSpec((1, heads, hd), seq_map),
                pl.BlockSpec((1, heads, hd), seq_map), pl.BlockSpec((1, 1, width), seq_map),
                pl.BlockSpec((heads, hd), fixed)]
    in_specs += [pl.BlockSpec((1, 1, width, hd), page_map(p)) for p in range(PAGES_PER_STEP)]
    in_specs += [pl.BlockSpec((1, 1, width, hd), page_map(p)) for p in range(PAGES_PER_STEP)]
    in_specs += [pl.BlockSpec((1, 1, 1, width), page_map(p)) for p in range(PAGES_PER_STEP)]
    grid_spec = pltpu.PrefetchScalarGridSpec(
        num_scalar_prefetch=1, grid=(n, nq, nk), in_specs=in_specs,
        out_specs=[pl.BlockSpec((1, tq, ATT_DIM), q_map), pl.BlockSpec((1, heads, hd), seq_map)],
        scratch_shapes=[pltpu.VMEM((ATT_HEADS, 1, tq), F32), pltpu.VMEM((ATT_HEADS, 1, tq), F32),
                        pltpu.VMEM((ATT_HEADS, HEAD_DIM, tq), F32),
                        pltpu.VMEM((heads, 1), F32), pltpu.VMEM((heads, 1), F32),
                        pltpu.VMEM((heads, hd), F32), pltpu.VMEM((1, width), F32),
                        pltpu.VMEM((PAGES_PER_STEP, width), F32)])
    return pl.pallas_call(
        _attention_kernel, grid_spec=grid_spec,
        out_shape=[jax.ShapeDtypeStruct((n, s, ATT_DIM), BF16),
                   jax.ShapeDtypeStruct((n_dec, heads, hd), BF16)],
        compiler_params=_cp(3), name="attention")(
            page_table, q, k, v, c, g.reshape(1, ATT_DIM), qs, kn, vn, lfn, g.reshape(heads, hd),
            *([ck] * PAGES_PER_STEP), *([cv] * PAGES_PER_STEP), *([cl] * PAGES_PER_STEP))


def _pad_lanes(w):
    return jnp.pad(w, ((0, 0), (0, LANES - w.shape[1])))


def kernel(x_prompt, x_sample, cache_k, cache_v, cache_logf, state_conv, page_table, norm_mix_g,
           w_in, b_forget, conv_w, g_att_out, g_conv_out, w_out, norm_ffn_g, dense_w_gate,
           dense_w_up, dense_w_down, moe_router, moe_w_gate, moe_w_up, moe_w_down, final_norm_g):
    n, seq, d = x_prompt.shape
    n_dec, dec_seq, _ = x_sample.shape
    assert dec_seq == 1 and d == D_MODEL
    depth = w_in.shape[0]
    mp = n * seq
    tm_p, tm_s = 512, n_dec

    xp = x_prompt.reshape(mp, d)
    xs = x_sample.reshape(n_dec, d)
    xnp = None
    xns = _rmsnorm(xs, norm_mix_g[0], tm_s)

    outs = {key: [] for key in ("fp", "cp", "ks", "vs", "fs", "cs")}
    yp = ys = k_all = v_all = None
    for l in range(depth):
        wl = w_in[l]
        o = 0
        wq = wl[:, o:o + ATT_DIM].astype(BF16); o += ATT_DIM
        wk = wl[:, o:o + ATT_DIM].astype(BF16); o += ATT_DIM
        wv = wl[:, o:o + ATT_DIM].astype(BF16); o += ATT_DIM
        wf = _pad_lanes(wl[:, o:o + ATT_HEADS]).astype(BF16); o += ATT_HEADS
        wb = wl[:, o:o + CONV_DIM].astype(BF16); o += CONV_DIM
        wc = wl[:, o:o + CONV_DIM].astype(BF16); o += CONV_DIM
        wh = wl[:, o:o + CONV_DIM].astype(BF16)
        bf = _pad_lanes(b_forget[l].reshape(1, ATT_HEADS))
        wo_a = w_out[l][:ATT_DIM].astype(BF16)
        wo_c = w_out[l][ATT_DIM:].astype(BF16)

        if xnp is None:
            xnp, q_p = _norm_mm(xp, norm_mix_g[l], wq, 1024, "norm_q_proj")
        else:
            q_p = _mm(xnp, wq, BF16, 1024, 1024, "q_proj")
        k_all = _mm_layer(xnp, wk, k_all, l, depth, 1024, 1024, "k_proj")
        v_all = _mm_layer(xnp, wv, v_all, l, depth, 1024, 1024, "v_proj")
        lf_p, c_p = _forget(xnp, wf, bf, seq, True)
        conv_p, tails = _conv_branch(xnp, wb, wc, wh, conv_w[l], g_conv_out[l], seq, 512, 1024)
        outs["fp"].append(lf_p[:, :ATT_HEADS].reshape(n, seq, ATT_HEADS))
        tiles_per_seq = seq // 512
        outs["cp"].append(tails[tiles_per_seq - 1::tiles_per_seq])

        q_s = _mm(xns, wq, BF16, tm_s, 512, "q_proj_s")
        k_s = _mm(xns, wk, F32, tm_s, 512, "k_proj_s")
        v_s = _mm(xns, wv, F32, tm_s, 512, "v_proj_s")
        lf_s = _forget(xns, wf, bf, n_dec, False)[0]
        st = state_conv[l]
        conv_s, u_s = _conv_step(xns, wb, wc, wh, conv_w[l], g_conv_out[l], st[:, 0], st[:, 1], 512)
        lfn = jnp.tile(lf_s[:, :ATT_HEADS], (1, cache_k.shape[2])).reshape(n_dec, 1, -1)

        att_p, att_s = _attention(
            l, q_p.reshape(n, seq, ATT_DIM), k_all.reshape(depth, n, seq, ATT_DIM),
            v_all.reshape(depth, n, seq, ATT_DIM), c_p.reshape(n, seq, LANES), g_att_out[l],
            page_table, q_s.reshape(n_dec, ATT_HEADS, HEAD_DIM),
            k_s.reshape(n_dec, ATT_HEADS, HEAD_DIM), v_s.reshape(n_dec, ATT_HEADS, HEAD_DIM), lfn,
            cache_k, cache_v, cache_logf, 512)
        xp, hnp = _outproj(att_p.reshape(mp, ATT_DIM), conv_p, wo_a, wo_c, xp, norm_ffn_g[l], tm_p)
        xs, hns = _outproj(att_s.reshape(n_dec, ATT_DIM), conv_s, wo_a, wo_c, xs, norm_ffn_g[l], tm_s)
        outs["ks"].append(k_s.reshape(n_dec, 1, ATT_HEADS, HEAD_DIM))
        outs["vs"].append(v_s.reshape(n_dec, 1, ATT_HEADS, HEAD_DIM))
        outs["fs"].append(lf_s[:, :ATT_HEADS].reshape(n_dec, 1, ATT_HEADS))
        outs["cs"].append(jnp.stack([st[:, 1], u_s], axis=1))

        last = l == depth - 1
        next_g = final_norm_g if last else norm_mix_g[l + 1]
        next_dtype = F32 if last else BF16
        i = l // 2
        if l % 2 == 0:
            wg = dense_w_gate[i].astype(BF16)
            wu = dense_w_up[i].astype(BF16)
            wd = dense_w_down[i].astype(BF16)
            hp = _swiglu(hnp, wg, wu, 1024, 512)
            xp, nxp = _down(hp, wd, xp, next_g, 256, not last, next_dtype)
            hs = _swiglu(hns, wg, wu, tm_s, 512)
            xs, nxs = _down(hs, wd, xs, next_g, tm_s, not last, next_dtype)
        else:
            xp, nxp, xs, nxs = _moe_ffn(
                xp, hnp, xs, hns, norm_ffn_g[l], moe_router[i], moe_w_gate[i].astype(BF16),
                moe_w_up[i].astype(BF16), moe_w_down[i].astype(BF16), next_g, not last, next_dtype)
        if last:
            yp, ys = nxp, nxs
        else:
            xnp, xns = nxp, nxs

    return (yp.reshape(n, seq, d), ys.reshape(n_dec, 1, d),
            k_all.reshape(depth, n, seq, ATT_HEADS, HEAD_DIM),
            v_all.reshape(depth, n, seq, ATT_HEADS, HEAD_DIM),
            jnp.stack(outs["fp"]), jnp.stack(outs["cp"]),
            jnp.stack(outs["ks"]), jnp.stack(outs["vs"]), jnp.stack(outs["fs"]), jnp.stack(outs["cs"]))
```

```python
import functools

import jax
import jax.numpy as jnp
from jax import lax
from jax.experimental import pallas as pl
from jax.experimental.pallas import tpu as pltpu

D_MODEL = 2048
HEAD_DIM = 128
ATT_HEADS = 8
ATT_DIM = ATT_HEADS * HEAD_DIM
CONV_DIM = D_MODEL - ATT_DIM
CONV_W = 3
N_EXPERTS = 8
TOP_K = 2
ATT_SCALE = HEAD_DIM ** -0.5
RMS_EPS = 1e-6
LOG2_E = 1.4426950408889634
LANES = 128
NEG_BIG = -1e30
PAGES_PER_STEP = 16
Q_STRIP = 128
VMEM_LIMIT = 56 * 1024 * 1024

F32 = jnp.float32
BF16 = jnp.bfloat16


def _cp(n_axes):
    return pltpu.CompilerParams(dimension_semantics=("arbitrary",) * n_axes,
                                vmem_limit_bytes=VMEM_LIMIT)


def _rms(x, g):
    return x * lax.rsqrt(jnp.mean(x * x, axis=-1, keepdims=True) + RMS_EPS) * g


def _dot(a, b):
    return jnp.dot(a, b, preferred_element_type=F32)


def _rmsnorm_kernel(x_ref, g_ref, o_ref):
    o_ref[...] = _rms(x_ref[...], g_ref[...]).astype(o_ref.dtype)


def _rmsnorm(x, g, tm, out_dtype=BF16):
    m, d = x.shape
    return pl.pallas_call(
        _rmsnorm_kernel, grid=(m // tm,),
        in_specs=[pl.BlockSpec((tm, d), lambda i: (i, 0)),
                  pl.BlockSpec((1, d), lambda i: (0, 0))],
        out_specs=pl.BlockSpec((tm, d), lambda i: (i, 0)),
        out_shape=jax.ShapeDtypeStruct((m, d), out_dtype),
        compiler_params=_cp(1), name="rmsnorm")(x, g.reshape(1, d))


def _mm_kernel(x_ref, w_ref, *rest):
    o_ref = rest[-1]
    o_ref[...] = _dot(x_ref[...], w_ref[...]).astype(o_ref.dtype).reshape(o_ref.shape)


def _mm_layer(x, w, stacked, layer, depth, tm, tn, name):
    m, k = x.shape
    n = w.shape[1]
    in_specs = [pl.BlockSpec((tm, k), lambda i, j: (i, 0)),
                pl.BlockSpec((k, tn), lambda i, j: (0, j))]
    args, aliases = [x, w], {}
    if stacked is not None:
        in_specs.append(pl.BlockSpec(memory_space=pl.ANY))
        args.append(stacked)
        aliases = {2: 0}
    return pl.pallas_call(
        _mm_kernel, grid=(m // tm, n // tn), in_specs=in_specs,
        out_specs=pl.BlockSpec((1, tm, tn), lambda i, j: (layer, i, j)),
        out_shape=jax.ShapeDtypeStruct((depth, m, n), F32), input_output_aliases=aliases,
        compiler_params=_cp(2), name=name)(*args)


def _mm(x, w, out_dtype, tm, tn, name):
    m, k = x.shape
    n = w.shape[1]
    return pl.pallas_call(
        _mm_kernel, grid=(m // tm, n // tn),
        in_specs=[pl.BlockSpec((tm, k), lambda i, j: (i, 0)),
                  pl.BlockSpec((k, tn), lambda i, j: (0, j))],
        out_specs=pl.BlockSpec((tm, tn), lambda i, j: (i, j)),
        out_shape=jax.ShapeDtypeStruct((m, n), out_dtype),
        compiler_params=_cp(2), name=name)(x, w)


def _norm_mm_kernel(x_ref, g_ref, w_ref, xn_ref, o_ref):
    xn = _rms(x_ref[...], g_ref[...]).astype(xn_ref.dtype)
    xn_ref[...] = xn
    o_ref[...] = _dot(xn, w_ref[...]).astype(o_ref.dtype)


def _norm_mm(x, g, w, tm, name):
    m, d = x.shape
    n = w.shape[1]
    row = lambda i: (i, 0)
    fixed = lambda i: (0, 0)
    return pl.pallas_call(
        _norm_mm_kernel, grid=(m // tm,),
        in_specs=[pl.BlockSpec((tm, d), row), pl.BlockSpec((1, d), fixed),
                  pl.BlockSpec((d, n), fixed)],
        out_specs=[pl.BlockSpec((tm, d), row), pl.BlockSpec((tm, n), row)],
        out_shape=[jax.ShapeDtypeStruct((m, d), BF16), jax.ShapeDtypeStruct((m, n), BF16)],
        compiler_params=_cp(1), name=name)(x, g.reshape(1, d), w)


def _swiglu_kernel(x_ref, wg_ref, wu_ref, o_ref):
    x = x_ref[...]
    a = _dot(x, wg_ref[...])
    b = _dot(x, wu_ref[...])
    o_ref[...] = (a * (1.0 / (1.0 + jnp.exp(-a))) * b).astype(o_ref.dtype)


def _swiglu(x, wg, wu, tm, tn):
    m, k = x.shape
    n = wg.shape[1]
    return pl.pallas_call(
        _swiglu_kernel, grid=(m // tm, n // tn),
        in_specs=[pl.BlockSpec((tm, k), lambda i, j: (i, 0)),
                  pl.BlockSpec((k, tn), lambda i, j: (0, j)),
                  pl.BlockSpec((k, tn), lambda i, j: (0, j))],
        out_specs=pl.BlockSpec((tm, tn), lambda i, j: (i, j)),
        out_shape=jax.ShapeDtypeStruct((m, n), BF16),
        compiler_params=_cp(2), name="swiglu_up")(x, wg, wu)


def _down_kernel(h_ref, w_ref, res_ref, g_ref, *outs, emit_x):
    xo = res_ref[...] + _dot(h_ref[...], w_ref[...])
    if emit_x:
        outs[0][...] = xo
    outs[-1][...] = _rms(xo, g_ref[...]).astype(outs[-1].dtype)


def _down(h, w, res, norm_g, tm, emit_x, norm_dtype):
    m, kdim = h.shape
    d = w.shape[1]
    row = lambda i: (i, 0)
    fixed = lambda i: (0, 0)
    out_specs = [pl.BlockSpec((tm, d), row)]
    out_shape = [jax.ShapeDtypeStruct((m, d), norm_dtype)]
    if emit_x:
        out_specs.insert(0, pl.BlockSpec((tm, d), row))
        out_shape.insert(0, jax.ShapeDtypeStruct((m, d), F32))
    outs = pl.pallas_call(
        functools.partial(_down_kernel, emit_x=emit_x), grid=(m // tm,),
        in_specs=[pl.BlockSpec((tm, kdim), row),
                  pl.BlockSpec((kdim, d), fixed, pipeline_mode=pl.Buffered(1)),
                  pl.BlockSpec((tm, d), row), pl.BlockSpec((1, d), fixed)],
        out_specs=out_specs, out_shape=out_shape,
        compiler_params=_cp(1), name="ffn_down")(h, w, res, norm_g.reshape(1, d))
    return (outs[0], outs[1]) if emit_x else (None, outs[0])


def _outproj_kernel(a_ref, c_ref, wa_ref, wc_ref, res_ref, g_ref, xo_ref, xn_ref):
    xo = res_ref[...] + (_dot(a_ref[...], wa_ref[...]) + _dot(c_ref[...], wc_ref[...]))
    xo_ref[...] = xo
    xn_ref[...] = _rms(xo, g_ref[...]).astype(xn_ref.dtype)


def _outproj(a, c, wa, wc, res, g, tm):
    m, d = res.shape
    ka, kc = a.shape[1], c.shape[1]
    row = lambda i: (i, 0)
    fixed = lambda i: (0, 0)
    return pl.pallas_call(
        _outproj_kernel, grid=(m // tm,),
        in_specs=[pl.BlockSpec((tm, ka), row), pl.BlockSpec((tm, kc), row),
                  pl.BlockSpec((ka, d), fixed), pl.BlockSpec((kc, d), fixed),
                  pl.BlockSpec((tm, d), row), pl.BlockSpec((1, d), fixed)],
        out_specs=[pl.BlockSpec((tm, d), row), pl.BlockSpec((tm, d), row)],
        out_shape=[jax.ShapeDtypeStruct((m, d), F32), jax.ShapeDtypeStruct((m, d), BF16)],
        compiler_params=_cp(1), name="out_proj")(a, c, wa, wc, res, g.reshape(1, d))


SEL_E1, SEL_E2, SEL_G1, SEL_G2, SEL_R1, SEL_R2 = range(6)


def _router_kernel(x_ref, r_ref, cin_ref, sel_ref, cnt_ref, carry_scr, *, n_valid):
    i = pl.program_id(0)
    tm = x_ref.shape[0]

    @pl.when(i == 0)
    def _():
        carry_scr[...] = cin_ref[...]

    logits = _dot(x_ref[...], r_ref[...])
    lane = lax.broadcasted_iota(jnp.int32, logits.shape, 1).astype(F32)
    lg = jnp.where(lane < N_EXPERTS, logits, -jnp.inf)
    m1 = jnp.max(lg, axis=1, keepdims=True)
    i1 = jnp.min(jnp.where(lg == m1, lane, float(LANES)), axis=1, keepdims=True)
    lg2 = jnp.where(lane == i1, -jnp.inf, lg)
    m2 = jnp.max(lg2, axis=1, keepdims=True)
    i2 = jnp.min(jnp.where(lg2 == m2, lane, float(LANES)), axis=1, keepdims=True)
    e2 = jnp.exp(m2 - m1)
    den = 1.0 + e2
    row = lax.broadcasted_iota(jnp.int32, logits.shape, 0) + i * tm
    hit = jnp.where(((lane == i1) | (lane == i2)) & (row < n_valid), 1.0, 0.0)
    r = lax.broadcasted_iota(jnp.int32, (tm, tm), 0)
    c = lax.broadcasted_iota(jnp.int32, (tm, tm), 1)
    before = (c < r).astype(BF16)
    rank = _dot(before, hit.astype(BF16)) + carry_scr[...]
    carry_scr[...] = carry_scr[...] + jnp.sum(hit, axis=0, keepdims=True)
    cnt_ref[...] = carry_scr[...]
    rank1 = jnp.sum(jnp.where(lane == i1, rank, 0.0), axis=1, keepdims=True)
    rank2 = jnp.sum(jnp.where(lane == i2, rank, 0.0), axis=1, keepdims=True)
    sel = jnp.zeros_like(logits)
    for idx, val in ((SEL_E1, i1), (SEL_E2, i2), (SEL_G1, 1.0 / den), (SEL_G2, e2 / den),
                     (SEL_R1, rank1), (SEL_R2, rank2)):
        sel = jnp.where(lane == idx, val, sel)
    sel_ref[...] = sel


def _router(hn, r_pad, count_in, tm, n_valid):
    m, d = hn.shape
    fixed = lambda i: (0, 0)
    return pl.pallas_call(
        functools.partial(_router_kernel, n_valid=n_valid), grid=(m // tm,),
        in_specs=[pl.BlockSpec((tm, d), lambda i: (i, 0)), pl.BlockSpec((d, LANES), fixed),
                  pl.BlockSpec((1, LANES), fixed)],
        out_specs=[pl.BlockSpec((tm, LANES), lambda i: (i, 0)), pl.BlockSpec((1, LANES), fixed)],
        out_shape=[jax.ShapeDtypeStruct((m, LANES), F32), jax.ShapeDtypeStruct((1, LANES), F32)],
        scratch_shapes=[pltpu.VMEM((1, LANES), F32)],
        compiler_params=_cp(1), name="router")(hn, r_pad, count_in)


def _dispatch_kernel(d1_ref, d2_ref, ps_ref, pc_ref, x_ref, *rest, first):
    xs_ref, zero_scr, sem = rest[-3:]
    tt = x_ref.shape[0]
    base = pl.program_id(0) * tt

    def row_copy(src_row, dst_row):
        return pltpu.make_async_copy(src_row, xs_ref.at[pl.ds(dst_row, 1), :], sem)

    def issue(r, carry):
        src = x_ref.at[pl.ds(r, 1), :]
        row_copy(src, d1_ref[base + r]).start()
        row_copy(src, d2_ref[base + r]).start(priority=1)
        return carry

    def drain(r, carry):
        row_copy(x_ref.at[pl.ds(0, 1), :], 0).wait()
        row_copy(x_ref.at[pl.ds(0, 1), :], 0).wait()
        return carry

    lax.fori_loop(0, tt, issue, 0, unroll=8)
    lax.fori_loop(0, tt, drain, 0, unroll=8)

    if first:
        @pl.when(pl.program_id(0) == 0)
        def _():
            zero_scr[...] = jnp.zeros_like(zero_scr)
            zrow = zero_scr.at[pl.ds(0, 1), :]
            for e in range(N_EXPERTS):
                def fill(r, carry, e=e):
                    row_copy(zrow, ps_ref[e] + r).start()
                    return carry

                def fill_done(r, carry):
                    row_copy(zrow, 0).wait()
                    return carry

                lax.fori_loop(0, pc_ref[e], fill, 0)
                lax.fori_loop(0, pc_ref[e], fill_done, 0)


def _dispatch(d1, d2, pad_start, pad_count, x, xs, tt, n_rows):
    m, d = x.shape
    first = xs is None
    in_specs = [pl.BlockSpec((tt, d), lambda i, *_: (i, 0))]
    args = [d1, d2, pad_start, pad_count, x]
    aliases = {}
    if not first:
        in_specs.append(pl.BlockSpec(memory_space=pl.ANY))
        args.append(xs)
        aliases = {5: 0}
    grid_spec = pltpu.PrefetchScalarGridSpec(
        num_scalar_prefetch=4, grid=(m // tt,), in_specs=in_specs,
        out_specs=pl.BlockSpec(memory_space=pl.ANY),
        scratch_shapes=[pltpu.VMEM((8, d), F32), pltpu.SemaphoreType.DMA(())])
    return pl.pallas_call(
        functools.partial(_dispatch_kernel, first=first), grid_spec=grid_spec,
        out_shape=jax.ShapeDtypeStruct((n_rows, d), F32), input_output_aliases=aliases,
        compiler_params=_cp(1), name="moe_dispatch")(*args)


def _moe_up_kernel(te_ref, na_ref, x_ref, g_ref, wg_ref, wu_ref, o_ref):
    del te_ref

    @pl.when(pl.program_id(1) < na_ref[0])
    def _():
        xn = _rms(x_ref[...], g_ref[...]).astype(BF16)
        a = _dot(xn, wg_ref[0])
        b = _dot(xn, wu_ref[0])
        o_ref[...] = (a * (1.0 / (1.0 + jnp.exp(-a))) * b).astype(o_ref.dtype)


def _moe_up(tile_expert, n_active, xs, g, wg, wu, tm, tn):
    r, d = xs.shape
    f = wg.shape[2]
    tile = lambda t, na: jnp.minimum(t, na[0] - 1)
    w_map = lambda j, t, te, na: (te[tile(t, na)], 0, j)
    grid_spec = pltpu.PrefetchScalarGridSpec(
        num_scalar_prefetch=2, grid=(f // tn, r // tm),
        in_specs=[pl.BlockSpec((tm, d), lambda j, t, te, na: (tile(t, na), 0)),
                  pl.BlockSpec((1, d), lambda j, t, te, na: (0, 0)),
                  pl.BlockSpec((1, d, tn), w_map), pl.BlockSpec((1, d, tn), w_map)],
        out_specs=pl.BlockSpec((tm, tn), lambda j, t, te, na: (tile(t, na), j)))
    return pl.pallas_call(
        _moe_up_kernel, grid_spec=grid_spec, out_shape=jax.ShapeDtypeStruct((r, f), BF16),
        compiler_params=_cp(2), name="moe_up")(tile_expert, n_active, xs, g.reshape(1, d), wg, wu)


def _moe_down_kernel(te_ref, na_ref, h_ref, w_ref, o_ref):
    del te_ref

    @pl.when(pl.program_id(0) < na_ref[0])
    def _():
        o_ref[...] = _dot(h_ref[...], w_ref[0])


def _moe_down(tile_expert, n_active, h, wd, tm):
    r, f = h.shape
    d = wd.shape[2]
    tile = lambda t, na: jnp.minimum(t, na[0] - 1)
    grid_spec = pltpu.PrefetchScalarGridSpec(
        num_scalar_prefetch=2, grid=(r // tm,),
        in_specs=[pl.BlockSpec((tm, f), lambda t, te, na: (tile(t, na), 0)),
                  pl.BlockSpec((1, f, d), lambda t, te, na: (te[tile(t, na)], 0, 0))],
        out_specs=pl.BlockSpec((tm, d), lambda t, te, na: (tile(t, na), 0)))
    return pl.pallas_call(
        _moe_down_kernel, grid_spec=grid_spec, out_shape=jax.ShapeDtypeStruct((r, d), F32),
        compiler_params=_cp(1), name="moe_down")(tile_expert, n_active, h, wd)


def _combine_kernel(d1_ref, d2_ref, ys_ref, res_ref, sel_ref, g_ref, *rest, emit_x):
    y1_scr, y2_scr, sem = rest[-3:]
    outs = rest[:-3]
    tt = res_ref.shape[0]
    base = pl.program_id(0) * tt

    def row_copy(src_row, dst):
        return pltpu.make_async_copy(ys_ref.at[pl.ds(src_row, 1), :], dst, sem)

    def issue(r, carry):
        row_copy(d1_ref[base + r], y1_scr.at[pl.ds(r, 1), :]).start()
        row_copy(d2_ref[base + r], y2_scr.at[pl.ds(r, 1), :]).start(priority=1)
        return carry

    def drain(r, carry):
        row_copy(0, y1_scr.at[pl.ds(0, 1), :]).wait()
        row_copy(0, y2_scr.at[pl.ds(0, 1), :]).wait()
        return carry

    lax.fori_loop(0, tt, issue, 0, unroll=8)
    lax.fori_loop(0, tt, drain, 0, unroll=8)
    sel = sel_ref[...]
    g1 = sel[:, SEL_G1:SEL_G1 + 1]
    g2 = sel[:, SEL_G2:SEL_G2 + 1]
    xo = res_ref[...] + (g1 * y1_scr[...] + g2 * y2_scr[...])
    if emit_x:
        outs[0][...] = xo
    outs[-1][...] = _rms(xo, g_ref[...]).astype(outs[-1].dtype)


def _combine(d1, d2, ys, res, sel, g, tt, emit_x, norm_dtype):
    m, d = res.shape
    row = lambda i, *_: (i, 0)
    out_specs = [pl.BlockSpec((tt, d), row)]
    out_shape = [jax.ShapeDtypeStruct((m, d), norm_dtype)]
    if emit_x:
        out_specs.insert(0, pl.BlockSpec((tt, d), row))
        out_shape.insert(0, jax.ShapeDtypeStruct((m, d), F32))
    grid_spec = pltpu.PrefetchScalarGridSpec(
        num_scalar_prefetch=2, grid=(m // tt,),
        in_specs=[pl.BlockSpec(memory_space=pl.ANY), pl.BlockSpec((tt, d), row),
                  pl.BlockSpec((tt, LANES), row), pl.BlockSpec((1, d), lambda i, *_: (0, 0))],
        out_specs=out_specs,
        scratch_shapes=[pltpu.VMEM((tt, d), F32), pltpu.VMEM((tt, d), F32),
                        pltpu.SemaphoreType.DMA(())])
    outs = pl.pallas_call(
        functools.partial(_combine_kernel, emit_x=emit_x), grid_spec=grid_spec,
        out_shape=out_shape, compiler_params=_cp(1), name="moe_combine")(
            d1, d2, ys, res, sel, g.reshape(1, d))
    return (outs[0], outs[1]) if emit_x else (None, outs[0])


def _moe_ffn(x_p, hn_p, x_s, hn_s, ffn_g, router, wg, wu, wd, next_g, emit_x, norm_dtype,
             tm=512, tn=1408, t_route=512, t_move=1024, t_comb=512):
    mp, d = x_p.shape
    ms = x_s.shape[0]
    n_tiles = (TOP_K * (mp + ms)) // tm + N_EXPERTS
    r_pad = _pad_lanes(router).astype(BF16)
    ms_pad = -(-ms // LANES) * LANES
    hn_s_pad = jnp.pad(hn_s, ((0, ms_pad - ms), (0, 0)))
    sel_p, cnt_p = _router(hn_p, r_pad, jnp.zeros((1, LANES), F32), t_route, mp)
    sel_s, cnt = _router(hn_s_pad, r_pad, cnt_p, ms_pad, ms)

    count = cnt[0, :N_EXPERTS].astype(jnp.int32)
    padded = (count + tm - 1) // tm * tm
    g_end = jnp.cumsum(padded)
    g_start = g_end - padded
    tile_expert = jnp.minimum(
        jnp.sum(jnp.arange(n_tiles)[:, None] >= (g_end // tm)[None, :], axis=1),
        N_EXPERTS - 1).astype(jnp.int32)
    n_active = (g_end[-1:] // tm).astype(jnp.int32)

    def dests(sel, rows):
        e1, e2 = sel[:rows, SEL_E1].astype(jnp.int32), sel[:rows, SEL_E2].astype(jnp.int32)
        return (g_start[e1] + sel[:rows, SEL_R1].astype(jnp.int32),
                g_start[e2] + sel[:rows, SEL_R2].astype(jnp.int32))

    d1_p, d2_p = dests(sel_p, mp)
    d1_s, d2_s = dests(sel_s, ms)
    pad_start = g_start + count
    pad_count = padded - count
    xs = _dispatch(d1_p, d2_p, pad_start, pad_count, x_p, None, t_move, n_tiles * tm)
    xs = _dispatch(d1_s, d2_s, pad_start, pad_count, x_s, xs, ms, n_tiles * tm)
    h = _moe_up(tile_expert, n_active, xs, ffn_g, wg, wu, tm, tn)
    ys = _moe_down(tile_expert, n_active, h, wd, tm)
    xo_p, xn_p = _combine(d1_p, d2_p, ys, x_p, sel_p, next_g, t_comb, emit_x, norm_dtype)
    xo_s, xn_s = _combine(d1_s, d2_s, ys, x_s, sel_s, next_g, ms, emit_x, norm_dtype)
    return xo_p, xn_p, xo_s, xn_s


def _forget_kernel(x_ref, w_ref, b_ref, lf_ref, *c_ref, cumsum):
    z = _dot(x_ref[...], w_ref[...]) + b_ref[...]
    lf = jnp.minimum(z, 0.0) - jnp.log1p(jnp.exp(-jnp.abs(z)))
    lf_ref[...] = lf
    if cumsum:
        r = lax.broadcasted_iota(jnp.int32, (LANES, LANES), 0)
        c = lax.broadcasted_iota(jnp.int32, (LANES, LANES), 1)
        tri = (c <= r).astype(F32)
        carry = jnp.zeros((1, LANES), F32)
        for blk in range(lf.shape[0] // LANES):
            rows = slice(blk * LANES, (blk + 1) * LANES)
            cb = jnp.dot(tri, lf[rows, :], preferred_element_type=F32,
                         precision=lax.Precision.HIGHEST) + carry
            carry = cb[LANES - 1:LANES, :]
            c_ref[0][rows, :] = cb


def _forget(xn, wf_pad, bf_pad, rows, cumsum):
    m, d = xn.shape
    n = m // rows
    out_specs = [pl.BlockSpec((rows, LANES), lambda b: (b, 0))]
    out_shape = [jax.ShapeDtypeStruct((m, LANES), F32)]
    if cumsum:
        out_specs.append(pl.BlockSpec((rows, LANES), lambda b: (b, 0)))
        out_shape.append(jax.ShapeDtypeStruct((m, LANES), F32))
    return pl.pallas_call(
        functools.partial(_forget_kernel, cumsum=cumsum), grid=(n,),
        in_specs=[pl.BlockSpec((rows, d), lambda b: (b, 0)),
                  pl.BlockSpec((d, LANES), lambda b: (0, 0)),
                  pl.BlockSpec((1, LANES), lambda b: (0, 0))],
        out_specs=out_specs, out_shape=out_shape,
        compiler_params=_cp(1), name="forget_gate")(xn, wf_pad, bf_pad)


def _group_norm_store(yc, g_ref, o_ref):
    for gi in range(yc.shape[1] // HEAD_DIM):
        sl = slice(gi * HEAD_DIM, (gi + 1) * HEAD_DIM)
        yg = yc[:, sl]
        ms = jnp.mean(yg * yg, axis=-1, keepdims=True)
        o_ref[:, sl] = (yg * lax.rsqrt(ms + RMS_EPS) * g_ref[:, sl]).astype(o_ref.dtype)


def _conv_kernel(x_ref, wb_ref, wc_ref, wh_ref, cw_ref, g_ref, o_ref, tail_ref, halo_scr,
                 *, tiles_per_seq):
    i, j = pl.program_id(0), pl.program_id(1)
    x = x_ref[...]
    gate_b = _dot(x, wb_ref[...])
    u = _dot(x, wc_ref[...]) * _dot(x, wh_ref[...])
    tm = u.shape[0]

    @pl.when(i % tiles_per_seq == 0)
    def _():
        halo_scr[j] = jnp.zeros(halo_scr.shape[1:], F32)

    prev = halo_scr[j]
    row = lax.broadcasted_iota(jnp.int32, u.shape, 0)
    u1 = jnp.where(row == 0, prev[1:2, :], pltpu.roll(u, 1, axis=0))
    u2 = jnp.where(row == 0, prev[0:1, :],
                   jnp.where(row == 1, prev[1:2, :], pltpu.roll(u, 2, axis=0)))
    cw = cw_ref[...]
    y = u2 * cw[0:1, :] + u1 * cw[1:2, :] + u * cw[2:3, :]
    _group_norm_store(gate_b * y, g_ref, o_ref)
    tail = u[tm - (CONV_W - 1):tm, :]
    halo_scr[j] = tail
    tail_ref[0] = tail


def _conv_branch(xn, wb, wc, wh, cw, g, seq, tm, tc):
    m, d = xn.shape
    n_i, n_j = m // tm, CONV_DIM // tc
    col = lambda i, j: (0, j)
    return pl.pallas_call(
        functools.partial(_conv_kernel, tiles_per_seq=seq // tm), grid=(n_i, n_j),
        in_specs=[pl.BlockSpec((tm, d), lambda i, j: (i, 0)),
                  pl.BlockSpec((d, tc), col), pl.BlockSpec((d, tc), col),
                  pl.BlockSpec((d, tc), col), pl.BlockSpec((CONV_W, tc), col),
                  pl.BlockSpec((1, tc), col)],
        out_specs=[pl.BlockSpec((tm, tc), lambda i, j: (i, j)),
                   pl.BlockSpec((1, CONV_W - 1, tc), lambda i, j: (i, 0, j))],
        out_shape=[jax.ShapeDtypeStruct((m, CONV_DIM), BF16),
                   jax.ShapeDtypeStruct((n_i, CONV_W - 1, CONV_DIM), F32)],
        scratch_shapes=[pltpu.VMEM((n_j, CONV_W - 1, tc), F32)],
        compiler_params=_cp(2), name="conv_branch")(xn, wb, wc, wh, cw, g.reshape(1, CONV_DIM))


def _conv_step_kernel(x_ref, wb_ref, wc_ref, wh_ref, cw_ref, g_ref, p0_ref, p1_ref, o_ref, u_ref):
    x = x_ref[...]
    gate_b = _dot(x, wb_ref[...])
    u = _dot(x, wc_ref[...]) * _dot(x, wh_ref[...])
    cw = cw_ref[...]
    y = p0_ref[...] * cw[0:1, :] + p1_ref[...] * cw[1:2, :] + u * cw[2:3, :]
    _group_norm_store(gate_b * y, g_ref, o_ref)
    u_ref[...] = u


def _conv_step(xn, wb, wc, wh, cw, g, p0, p1, tc):
    m, d = xn.shape
    col = lambda j: (0, j)
    return pl.pallas_call(
        _conv_step_kernel, grid=(CONV_DIM // tc,),
        in_specs=[pl.BlockSpec((m, d), lambda j: (0, 0)),
                  pl.BlockSpec((d, tc), col), pl.BlockSpec((d, tc), col),
                  pl.BlockSpec((d, tc), col), pl.BlockSpec((CONV_W, tc), col),
                  pl.BlockSpec((1, tc), col), pl.BlockSpec((m, tc), col),
                  pl.BlockSpec((m, tc), col)],
        out_specs=[pl.BlockSpec((m, tc), col), pl.BlockSpec((m, tc), col)],
        out_shape=[jax.ShapeDtypeStruct((m, CONV_DIM), BF16),
                   jax.ShapeDtypeStruct((m, CONV_DIM), F32)],
        compiler_params=_cp(1), name="conv_step")(xn, wb, wc, wh, cw, g.reshape(1, CONV_DIM), p0, p1)


def _prompt_step(q_ref, k_ref, v_ref, c_ref, m_scr, l_scr, acc_scr, diagonal):
    tq, tk = q_ref.shape[1], k_ref.shape[2]
    if diagonal:
        causal = (lax.broadcasted_iota(jnp.int32, (Q_STRIP, Q_STRIP), 0)
                  <= lax.broadcasted_iota(jnp.int32, (Q_STRIP, Q_STRIP), 1))
    for h in range(ATT_HEADS):
        sl = slice(h * HEAD_DIM, (h + 1) * HEAD_DIM)
        k = k_ref[0, 0, :, sl].astype(BF16)
        vt = v_ref[0, 0, :, sl].T.astype(BF16)
        c_keys = jnp.broadcast_to(c_ref[0, :, h:h + 1] * LOG2_E, (tk, Q_STRIP))
        for s0 in range(0, tq, Q_STRIP):
            qs = slice(s0, s0 + Q_STRIP)
            nkeys = min(tk, s0 + Q_STRIP) if diagonal else tk
            st = lax.dot_general(k[:nkeys], q_ref[0, qs, sl], (((1,), (1,)), ((), ())),
                                 preferred_element_type=F32)
            st = st * (ATT_SCALE * LOG2_E) - c_keys[:nkeys]
            if diagonal:
                tail = jnp.where(causal, st[s0:], NEG_BIG)
                st = tail if s0 == 0 else jnp.concatenate([st[:s0], tail], axis=0)
            m_prev = m_scr[h, :, qs]
            m_new = jnp.maximum(m_prev, jnp.max(st, axis=0, keepdims=True))
            alpha = jnp.exp2(m_prev - m_new)
            p = jnp.exp2(st - m_new)
            l_scr[h, :, qs] = alpha * l_scr[h, :, qs] + jnp.sum(p, axis=0, keepdims=True)
            acc_scr[h, :, qs] = alpha * acc_scr[h, :, qs] + _dot(vt[:, :nkeys], p.astype(BF16))
            m_scr[h, :, qs] = m_new


def _prompt_finish(g_ref, o_ref, l_scr, acc_scr):
    for h in range(ATT_HEADS):
        sl = slice(h * HEAD_DIM, (h + 1) * HEAD_DIM)
        ot = acc_scr[h] / l_scr[h]
        ms = jnp.mean(ot * ot, axis=0, keepdims=True)
        o = (ot * lax.rsqrt(ms + RMS_EPS)).T
        o_ref[0, :, sl] = (o * g_ref[:, sl]).astype(o_ref.dtype)


def _decode_init(lfn_ref, m_scr, l_scr, acc_scr, carry_scr):
    m_scr[...] = jnp.full(m_scr.shape, NEG_BIG, F32)
    l_scr[...] = jnp.zeros_like(l_scr)
    acc_scr[...] = jnp.zeros_like(acc_scr)
    carry_scr[...] = lfn_ref[0]


def _decode_chunk(q_ref, k_refs, v_refs, lf_refs, m_scr, l_scr, acc_scr, carry_scr, lf_scr):
    npg = PAGES_PER_STEP
    width = lf_scr.shape[1]
    lane = lax.broadcasted_iota(jnp.int32, (npg, width), 1)
    for p in range(npg):
        lf_scr[p:p + 1, :] = lf_refs[p][0, 0]
    lf = lf_scr[...]
    y = lf
    sh = ATT_HEADS
    while sh < width:
        y = y + jnp.where(lane < width - sh, pltpu.roll(y, width - sh, axis=1), 0.0)
        sh *= 2
    excl = y - lf
    z = jnp.where(lane < ATT_HEADS, y, 0.0)
    sh = ATT_HEADS
    while sh < width:
        z = z + pltpu.roll(z, sh, axis=1)
        sh *= 2
    carry = carry_scr[...]
    bias = [None] * npg
    for p in reversed(range(npg)):
        bias[p] = carry + excl[p:p + 1, :]
        carry = carry + z[p:p + 1, :]
    carry_scr[...] = carry

    q = q_ref[0]
    hrow = lax.broadcasted_iota(jnp.int32, (ATT_HEADS, width), 0)
    hlane = lax.broadcasted_iota(jnp.int32, (ATT_HEADS, width), 1)
    own = jnp.bitwise_and(hlane, ATT_HEADS - 1) == hrow
    s_all = []
    for p in range(npg):
        kp = k_refs[p][0, 0].astype(BF16)
        s = lax.dot_general(q, kp, (((1,), (1,)), ((), ())), preferred_element_type=F32)
        s_all.append(jnp.where(own, s * ATT_SCALE + bias[p], NEG_BIG))
    m_prev = m_scr[...]
    m_new = m_prev
    for s in s_all:
        m_new = jnp.maximum(m_new, jnp.max(s, axis=1, keepdims=True))
    alpha = jnp.exp(m_prev - m_new)
    l_new = alpha * l_scr[...]
    acc = alpha * acc_scr[...]
    for p in range(npg):
        pr = jnp.exp(s_all[p] - m_new)
        l_new = l_new + jnp.sum(pr, axis=1, keepdims=True)
        acc = acc + _dot(pr.astype(BF16), v_refs[p][0, 0].astype(BF16))
    m_scr[...] = m_new
    l_scr[...] = l_new
    acc_scr[...] = acc


def _decode_finish(q_ref, kn_ref, vn_ref, g_ref, o_ref, m_scr, l_scr, acc_scr):
    q = q_ref[0].astype(F32)
    kn = kn_ref[0].astype(BF16).astype(F32)
    vn = vn_ref[0].astype(BF16).astype(F32)
    m_past = m_scr[...]
    s_new = jnp.sum(q * kn, axis=1, keepdims=True) * ATT_SCALE
    m_fin = jnp.maximum(m_past, s_new)
    a = jnp.exp(m_past - m_fin)
    pn = jnp.exp(s_new - m_fin)
    o = (a * acc_scr[...] + pn * vn) / (a * l_scr[...] + pn)
    ms = jnp.mean(o * o, axis=-1, keepdims=True)
    o_ref[0] = (o * lax.rsqrt(ms + RMS_EPS) * g_ref[...]).astype(o_ref.dtype)


def _attention_kernel(pt_ref, q_ref, k_ref, v_ref, c_ref, gp_ref, qs_ref, kn_ref, vn_ref, lfn_ref,
                      gs_ref, *rest):
    del pt_ref
    npg = PAGES_PER_STEP
    k_refs, v_refs, lf_refs = rest[:npg], rest[npg:2 * npg], rest[2 * npg:3 * npg]
    (o_ref, os_ref, m_scr, l_scr, acc_scr,
     dm_scr, dl_scr, dacc_scr, carry_scr, lf_scr) = rest[3 * npg:]
    qi, ki = pl.program_id(1), pl.program_id(2)

    def decode():
        _decode_chunk(qs_ref, k_refs, v_refs, lf_refs, dm_scr, dl_scr, dacc_scr, carry_scr, lf_scr)

    @pl.when(ki == 0)
    def _():
        m_scr[...] = jnp.full(m_scr.shape, NEG_BIG, F32)
        l_scr[...] = jnp.zeros_like(l_scr)
        acc_scr[...] = jnp.zeros_like(acc_scr)
        _decode_init(lfn_ref, dm_scr, dl_scr, dacc_scr, carry_scr)

    @pl.when(ki < qi)
    def _():
        _prompt_step(q_ref, k_ref, v_ref, c_ref, m_scr, l_scr, acc_scr, False)
        decode()

    @pl.when(ki == qi)
    def _():
        _prompt_step(q_ref, k_ref, v_ref, c_ref, m_scr, l_scr, acc_scr, True)
        _prompt_finish(gp_ref, o_ref, l_scr, acc_scr)
        decode()

    @pl.when(ki > qi)
    def _():
        decode()

    @pl.when(ki == pl.num_programs(2) - 1)
    def _():
        _decode_finish(qs_ref, kn_ref, vn_ref, gs_ref, os_ref, dm_scr, dl_scr, dacc_scr)


def _attention(layer, q, k, v, c, g, page_table, qs, kn, vn, lfn, cache_k, cache_v, cache_logf,
               blk):
    n, s, _ = q.shape
    tq = tk = blk
    nq = nk = s // blk
    n_dec, n_pages = page_table.shape
    depth, n_pool, page, heads, hd = cache_k.shape
    width = page * heads
    assert n_dec == n * nq and n_pages == nk * PAGES_PER_STEP and blk % Q_STRIP == 0
    ck = cache_k.reshape(depth, n_pool, width, hd)
    cv = cache_v.reshape(depth, n_pool, width, hd)
    cl = cache_logf.reshape(depth, n_pool, 1, width)

    def kv_block(qi, ki):
        return jnp.minimum(ki, (tq // tk) * (qi + 1) - 1)

    def page_map(p):
        return lambda b, qi, ki, pt: (layer, pt[b * nq + qi, (nk - 1 - ki) * PAGES_PER_STEP + p], 0, 0)

    kv_map = lambda b, qi, ki, pt: (layer, b, kv_block(qi, ki), 0)
    q_map = lambda b, qi, ki, pt: (b, qi, 0)
    seq_map = lambda b, qi, ki, pt: (b * nq + qi, 0, 0)
    fixed = lambda b, qi, ki, pt: (0, 0)
    in_specs = [pl.BlockSpec((1, tq, ATT_DIM), q_map),
                pl.BlockSpec((1, 1, tk, ATT_DIM), kv_map), pl.BlockSpec((1, 1, tk, ATT_DIM), kv_map),
                pl.BlockSpec((1, tk, LANES), lambda b, qi, ki, pt: (b, kv_block(qi, ki), 0)),
                pl.BlockSpec((1, ATT_DIM), fixed),
                pl.BlockSpec((1, heads, hd), seq_map), pl.BlockSpec((1, heads, hd), seq_map),
                pl.BlockSpec((1, heads, hd), seq_map), pl.BlockSpec((1, 1, width), seq_map),
                pl.BlockSpec((heads, hd), fixed)]
    in_specs += [pl.BlockSpec((1, 1, width, hd), page_map(p)) for p in range(PAGES_PER_STEP)]
    in_specs += [pl.BlockSpec((1, 1, width, hd), page_map(p)) for p in range(PAGES_PER_STEP)]
    in_specs += [pl.BlockSpec((1, 1, 1, width), page_map(p)) for p in range(PAGES_PER_STEP)]
    grid_spec = pltpu.PrefetchScalarGridSpec(
        num_scalar_prefetch=1, grid=(n, nq, nk), in_specs=in_specs,
        out_specs=[pl.BlockSpec((1, tq, ATT_DIM), q_map), pl.BlockSpec((1, heads, hd), seq_map)],
        scratch_shapes=[pltpu.VMEM((ATT_HEADS, 1, tq), F32), pltpu.VMEM((ATT_HEADS, 1, tq), F32),
                        pltpu.VMEM((ATT_HEADS, HEAD_DIM, tq), F32),
                        pltpu.VMEM((heads, 1), F32), pltpu.VMEM((heads, 1), F32),
                        pltpu.VMEM((heads, hd), F32), pltpu.VMEM((1, width), F32),
                        pltpu.VMEM((PAGES_PER_STEP, width), F32)])
    return pl.pallas_call(
        _attention_kernel, grid_spec=grid_spec,
        out_shape=[jax.ShapeDtypeStruct((n, s, ATT_DIM), BF16),
                   jax.ShapeDtypeStruct((n_dec, heads, hd), BF16)],
        compiler_params=_cp(3), name="attention")(
            page_table, q, k, v, c, g.reshape(1, ATT_DIM), qs, kn, vn, lfn, g.reshape(heads, hd),
            *([ck] * PAGES_PER_STEP), *([cv] * PAGES_PER_STEP), *([cl] * PAGES_PER_STEP))


def _pad_lanes(w):
    return jnp.pad(w, ((0, 0), (0, LANES - w.shape[1])))


def kernel(x_prompt, x_sample, cache_k, cache_v, cache_logf, state_conv, page_table, norm_mix_g,
           w_in, b_forget, conv_w, g_att_out, g_conv_out, w_out, norm_ffn_g, dense_w_gate,
           dense_w_up, dense_w_down, moe_router, moe_w_gate, moe_w_up, moe_w_down, final_norm_g):
    n, seq, d = x_prompt.shape
    n_dec, dec_seq, _ = x_sample.shape
    assert dec_seq == 1 and d == D_MODEL
    depth = w_in.shape[0]
    mp = n * seq
    tm_p, tm_s = 512, n_dec

    xp = x_prompt.reshape(mp, d)
    xs = x_sample.reshape(n_dec, d)
    xnp = None
    xns = _rmsnorm(xs, norm_mix_g[0], tm_s)

    outs = {key: [] for key in ("fp", "cp", "ks", "vs", "fs", "cs")}
    yp = ys = k_all = v_all = None
    for l in range(depth):
        wl = w_in[l]
        o = 0
        wq = wl[:, o:o + ATT_DIM].astype(BF16); o += ATT_DIM
        wk = wl[:, o:o + ATT_DIM].astype(BF16); o += ATT_DIM
        wv = wl[:, o:o + ATT_DIM].astype(BF16); o += ATT_DIM
        wf = _pad_lanes(wl[:, o:o + ATT_HEADS]).astype(BF16); o += ATT_HEADS
        wb = wl[:, o:o + CONV_DIM].astype(BF16); o += CONV_DIM
        wc = wl[:, o:o + CONV_DIM].astype(BF16); o += CONV_DIM
        wh = wl[:, o:o + CONV_DIM].astype(BF16)
        bf = _pad_lanes(b_forget[l].reshape(1, ATT_HEADS))
        wo_a = w_out[l][:ATT_DIM].astype(BF16)
        wo_c = w_out[l][ATT_DIM:].astype(BF16)

        if xnp is None:
            xnp, q_p = _norm_mm(xp, norm_mix_g[l], wq, 1024, "norm_q_proj")
        else:
            q_p = _mm(xnp, wq, BF16, 1024, 1024, "q_proj")
        k_all = _mm_layer(xnp, wk, k_all, l, depth, 1024, 1024, "k_proj")
        v_all = _mm_layer(xnp, wv, v_all, l, depth, 1024, 1024, "v_proj")
        lf_p, c_p = _forget(xnp, wf, bf, seq, True)
        conv_p, tails = _conv_branch(xnp, wb, wc, wh, conv_w[l], g_conv_out[l], seq, 512, 1024)
        outs["fp"].append(lf_p[:, :ATT_HEADS].reshape(n, seq, ATT_HEADS))
        tiles_per_seq = seq // 512
        outs["cp"].append(tails[tiles_per_seq - 1::tiles_per_seq])

        q_s = _mm(xns, wq, BF16, tm_s, 512, "q_proj_s")
        k_s = _mm(xns, wk, F32, tm_s, 512, "k_proj_s")
        v_s = _mm(xns, wv, F32, tm_s, 512, "v_proj_s")
        lf_s = _forget(xns, wf, bf, n_dec, False)[0]
        st = state_conv[l]
        conv_s, u_s = _conv_step(xns, wb, wc, wh, conv_w[l], g_conv_out[l], st[:, 0], st[:, 1], 512)
        lfn = jnp.tile(lf_s[:, :ATT_HEADS], (1, cache_k.shape[2])).reshape(n_dec, 1, -1)

        att_p, att_s = _attention(
            l, q_p.reshape(n, seq, ATT_DIM), k_all.reshape(depth, n, seq, ATT_DIM),
            v_all.reshape(depth, n, seq, ATT_DIM), c_p.reshape(n, seq, LANES), g_att_out[l],
            page_table, q_s.reshape(n_dec, ATT_HEADS, HEAD_DIM),
            k_s.reshape(n_dec, ATT_HEADS, HEAD_DIM), v_s.reshape(n_dec, ATT_HEADS, HEAD_DIM), lfn,
            cache_k, cache_v, cache_logf, 512)
        xp, hnp = _outproj(att_p.reshape(mp, ATT_DIM), conv_p, wo_a, wo_c, xp, norm_ffn_g[l], tm_p)
        xs, hns = _outproj(att_s.reshape(n_dec, ATT_DIM), conv_s, wo_a, wo_c, xs, norm_ffn_g[l], tm_s)
        outs["ks"].append(k_s.reshape(n_dec, 1, ATT_HEADS, HEAD_DIM))
        outs["vs"].append(v_s.reshape(n_dec, 1, ATT_HEADS, HEAD_DIM))
        outs["fs"].append(lf_s[:, :ATT_HEADS].reshape(n_dec, 1, ATT_HEADS))
        outs["cs"].append(jnp.stack([st[:, 1], u_s], axis=1))

        last = l == depth - 1
        next_g = final_norm_g if last else norm_mix_g[l + 1]
        next_dtype = F32 if last else BF16
        i = l // 2
        if l % 2 == 0:
            wg = dense_w_gate[i].astype(BF16)
            wu = dense_w_up[i].astype(BF16)
            wd = dense_w_down[i].astype(BF16)
            hp = _swiglu(hnp, wg, wu, 1024, 512)
            xp, nxp = _down(hp, wd, xp, next_g, 256, not last, next_dtype)
            hs = _swiglu(hns, wg, wu, tm_s, 512)
            xs, nxs = _down(hs, wd, xs, next_g, tm_s, not last, next_dtype)
        else:
            xp, nxp, xs, nxs = _moe_ffn(
                xp, hnp, xs, hns, norm_ffn_g[l], moe_router[i], moe_w_gate[i].astype(BF16),
                moe_w_up[i].astype(BF16), moe_w_down[i].astype(BF16), next_g, not last, next_dtype)
        if last:
            yp, ys = nxp, nxs
        else:
            xnp, xns = nxp, nxs

    return (yp.reshape(n, seq, d), ys.reshape(n_dec, 1, d),
            k_all.reshape(depth, n, seq, ATT_HEADS, HEAD_DIM),
            v_all.reshape(depth, n, seq, ATT_HEADS, HEAD_DIM),
            jnp.stack(outs["fp"]), jnp.stack(outs["cp"]),
            jnp.stack(outs["ks"]), jnp.stack(outs["vs"]), jnp.stack(outs["fs"]), jnp.stack(outs["cs"]))
```
